```python
import math
import jax
import jax.numpy as jnp
from jax import lax
import numpy as np

D_MODEL = 4096
BATCH = 1
SEQ = 8192
DEPTH = 4

CTX_LEN = 256
GRID_W = 64
N_MIXERS = 3
EPS = 1e-6
F32 = jnp.float32

MLA_HEADS = D_MODEL // 128
MLA_Q_RANK = D_MODEL // 4
MLA_KV_RANK = 512
MLA_NOPE = 128
MLA_ROPE = 64
MLA_V = 128
MLA_WIDTH = MLA_HEADS * MLA_V
MLA_SPLITS = (MLA_Q_RANK, MLA_Q_RANK + MLA_KV_RANK, MLA_Q_RANK + MLA_KV_RANK + MLA_ROPE)
MLA_IN = MLA_SPLITS[2] + MLA_WIDTH
MLA_SCALE = (MLA_NOPE + MLA_ROPE) ** -0.5
ROPE_BASE = 10000.0
Q_BLOCK = 128

GDN_K_HEADS = D_MODEL // 128
GDN_V_HEADS = 2 * GDN_K_HEADS
GDN_DK = 128
GDN_DV = 128
GDN_KW = GDN_K_HEADS * GDN_DK
GDN_VW = GDN_V_HEADS * GDN_DV
GDN_QKV = 2 * GDN_KW + GDN_VW
GDN_IN = GDN_QKV + 4 * GDN_V_HEADS + GDN_VW
GDN_CONV = 5
GDN_CHUNK = 64

POOL_WINDOWS = (2, 4, 8, 16)
POOL_WIDTH = D_MODEL
POOL_GROUP = POOL_WIDTH // len(POOL_WINDOWS)

kernel_name = 'hybrid_mla_gdn_pool_prefix_dit'


def rms_norm(x, g):
    xf = x.astype(F32)
    y = xf * lax.rsqrt(jnp.mean(xf * xf, axis=-1, keepdims=True) + EPS)
    return (y * g.astype(F32)).astype(x.dtype)


def adaln(cond, w, b):
    m = (jax.nn.silu(cond) @ w + b)[:, None, :]
    return jnp.split(m, 3, axis=-1)


def axial_rope(rows):
    row = jnp.repeat(jnp.arange(rows, dtype=F32), GRID_W)
    col = jnp.tile(jnp.arange(GRID_W, dtype=F32), rows)
    n_freq = MLA_ROPE // 4
    inv_freq = ROPE_BASE ** (-jnp.arange(n_freq, dtype=F32) / n_freq)
    ang = jnp.concatenate([row[:, None] * inv_freq, col[:, None] * inv_freq], axis=-1)
    return jnp.cos(ang), jnp.sin(ang)


def apply_rope(x, cos, sin):
    x1, x2 = jnp.split(x, 2, axis=-1)
    return jnp.concatenate([x1 * cos - x2 * sin, x1 * sin + x2 * cos], axis=-1).astype(x.dtype)


def mla_queries(cq, g_q, w_q_up):
    B, L, _ = cq.shape
    q = (rms_norm(cq, g_q) @ w_q_up).reshape(B, L, MLA_HEADS, MLA_NOPE + MLA_ROPE)
    return q[..., :MLA_NOPE], q[..., MLA_NOPE:]


def mla_keys_values(ckv, g_kv, w_kv_up):
    B, L, _ = ckv.shape
    kv = (rms_norm(ckv, g_kv) @ w_kv_up).reshape(B, L, MLA_HEADS, MLA_NOPE + MLA_V)
    return kv[..., :MLA_NOPE], kv[..., MLA_NOPE:]


def mla_attend(qn, qr, kn, kr, v):
    s = (jnp.einsum('bqhd,bkhd->bhqk', qn, kn).astype(F32)
         + jnp.einsum('bqhr,bkr->bhqk', qr, kr).astype(F32))
    p = jax.nn.softmax(s * MLA_SCALE, axis=-1)
    return jnp.einsum('bhqk,bkhd->bqhd', p.astype(v.dtype), v)


def mla_mixer(h_lat, h_ctx, cos, sin, w_in, g_q, w_q_up, g_kv, w_kv_up, w_out, need_ctx_out):
    B, S, _ = h_lat.shape
    C = h_ctx.shape[1]
    cq_l, ckv_l, kr_l, z_l = jnp.split(h_lat @ w_in, MLA_SPLITS, axis=-1)
    qn_l, qr_l = mla_queries(cq_l, g_q, w_q_up)
    qr_l = apply_rope(qr_l, cos[None, :, None, :], sin[None, :, None, :])
    kr_l = apply_rope(kr_l, cos[None], sin[None])
    kn_l, v_l = mla_keys_values(ckv_l, g_kv, w_kv_up)
    if need_ctx_out:
        cq_c, ckv_c, kr_c, z_c = jnp.split(h_ctx @ w_in, MLA_SPLITS, axis=-1)
    else:
        ckv_c, kr_c = jnp.split(h_ctx @ w_in[:, MLA_Q_RANK:MLA_SPLITS[2]], [MLA_KV_RANK], axis=-1)
    kn_c, v_c = mla_keys_values(ckv_c, g_kv, w_kv_up)
    kn_all = jnp.concatenate([kn_c, kn_l], axis=1)
    kr_all = jnp.concatenate([kr_c, kr_l], axis=1)
    v_all = jnp.concatenate([v_c, v_l], axis=1)
    nb = S // Q_BLOCK

    def to_blocks(t):
        return jnp.moveaxis(t.reshape(B, nb, Q_BLOCK, *t.shape[2:]), 1, 0)

    o_l = lax.map(lambda qb: mla_attend(qb[0], qb[1], kn_all, kr_all, v_all),
                  (to_blocks(qn_l), to_blocks(qr_l)))
    o_l = jnp.moveaxis(o_l, 0, 1).reshape(B, S, MLA_WIDTH)
    y_lat = (o_l * jax.nn.silu(z_l)) @ w_out
    y_ctx = None
    if need_ctx_out:
        qn_c, qr_c = mla_queries(cq_c, g_q, w_q_up)
        o_c = mla_attend(qn_c, qr_c, kn_c, kr_c, v_c).reshape(B, C, MLA_WIDTH)
        y_ctx = (o_c * jax.nn.silu(z_c)) @ w_out
    return y_lat, y_ctx


def l2_normalize(t):
    t = t.astype(F32)
    return t * lax.rsqrt(jnp.sum(t * t, axis=-1, keepdims=True) + EPS)


def centred_depthwise_conv(u, w):
    L = u.shape[1]
    r = GDN_CONV // 2
    up = jnp.pad(u, ((0, 0), (r, r), (0, 0)))
    return sum(up[:, j:j + L] * w[j] for j in range(GDN_CONV))


def gdn_project(h, w_in, conv_w):
    B, L, _ = h.shape
    qkv, a, b, z = jnp.split(h @ w_in, [GDN_QKV, GDN_QKV + 2 * GDN_V_HEADS, GDN_QKV + 4 * GDN_V_HEADS], axis=-1)
    qkv = jax.nn.silu(centred_depthwise_conv(qkv, conv_w))
    q, k, v = jnp.split(qkv, [GDN_KW, 2 * GDN_KW], axis=-1)
    rep = GDN_V_HEADS // GDN_K_HEADS
    q = jnp.repeat(l2_normalize(q.reshape(B, L, GDN_K_HEADS, GDN_DK)), rep, axis=2) * (GDN_DK ** -0.5)
    k = jnp.repeat(l2_normalize(k.reshape(B, L, GDN_K_HEADS, GDN_DK)), rep, axis=2)
    v = v.reshape(B, L, GDN_V_HEADS, GDN_DV).astype(F32)
    a = a.reshape(B, L, 2, GDN_V_HEADS).astype(F32)
    b = b.reshape(B, L, 2, GDN_V_HEADS).astype(F32)
    z = z.reshape(B, L, GDN_V_HEADS, GDN_DV)
    return q, k, v, a, b, z


def gdn_gates(a, b, a_log, dt_bias, d):
    g = -jnp.exp(a_log[d].astype(F32)) * jax.nn.softplus(a[:, :, d] + dt_bias[d].astype(F32))
    beta = jax.nn.sigmoid(b[:, :, d])
    return g, beta


def gated_delta_chunked(q, k, v, g, beta, state0):
    B, L, H, _ = k.shape
    n, C = L // GDN_CHUNK, GDN_CHUNK

    def chunks(t):
        t = t.astype(F32).reshape(B, n, C, H, *t.shape[3:])
        return jnp.moveaxis(t, 3, 1)

    q, k, v, g, beta = (chunks(t) for t in (q, k, v, g, beta))
    g = jnp.cumsum(g, axis=-1)
    lower = jnp.tril(jnp.ones((C, C), dtype=bool))
    strict = jnp.tril(jnp.ones((C, C), dtype=bool), -1)
    decay = jnp.exp(jnp.where(lower, g[..., :, None] - g[..., None, :], -jnp.inf))
    k_beta = k * beta[..., None]
    a = jnp.where(strict, jnp.einsum('bhncd,bhnsd->bhncs', k_beta, k) * decay, 0.0)
    eye = jnp.eye(C, dtype=F32)
    t_inv = lax.linalg.triangular_solve(a + eye, jnp.broadcast_to(eye, a.shape),
                                        left_side=True, lower=True, unit_diagonal=True)
    u = t_inv @ (v * beta[..., None])
    w = t_inv @ (k_beta * jnp.exp(g)[..., None])
    qk = jnp.where(lower, jnp.einsum('bhncd,bhnsd->bhncs', q, k) * decay, 0.0)
    q_dec = q * jnp.exp(g)[..., None]
    k_dec = k * jnp.exp(g[..., -1:] - g)[..., None]
    g_last = jnp.exp(g[..., -1])

    def step(state, xs):
        qk_i, q_i, k_i, u_i, w_i, gl_i = xs
        v_new = u_i - jnp.einsum('bhcd,bhde->bhce', w_i, state)
        o_i = jnp.einsum('bhcd,bhde->bhce', q_i, state) + jnp.einsum('bhcs,bhse->bhce', qk_i, v_new)
        state = state * gl_i[..., None, None] + jnp.einsum('bhcd,bhce->bhde', k_i, v_new)
        return state, o_i

    xs = tuple(jnp.moveaxis(t, 2, 0) for t in (qk, q_dec, k_dec, u, w, g_last))
    state, o = lax.scan(step, state0, xs)
    o = o.transpose(1, 0, 3, 2, 4).reshape(B, L, H, -1)
    return o, state


def gdn_output(o, z, g_norm, w_out):
    B, L = o.shape[:2]
    y = rms_norm(o, g_norm) * jax.nn.silu(z.astype(F32))
    return y.reshape(B, L, GDN_VW).astype(z.dtype) @ w_out


def gdn_mixer(h_lat, h_ctx, w_in, conv_w, a_log, dt_bias, g_norm, w_out, need_ctx_out):
    q_l, k_l, v_l, a_l, b_l, z_l = gdn_project(h_lat, w_in, conv_w)
    q_c, k_c, v_c, a_c, b_c, z_c = gdn_project(h_ctx, w_in, conv_w)
    state0 = jnp.zeros((h_ctx.shape[0], GDN_V_HEADS, GDN_DK, GDN_DV), F32)
    o_lat, o_ctx = [], []
    for d in range(2):
        def dirn(t, flip=(d == 1)):
            return jnp.flip(t, axis=1) if flip else t
        g_c, beta_c = gdn_gates(a_c, b_c, a_log, dt_bias, d)
        oc, s_ctx = gated_delta_chunked(dirn(q_c), dirn(k_c), dirn(v_c), dirn(g_c), dirn(beta_c), state0)
        g_l, beta_l = gdn_gates(a_l, b_l, a_log, dt_bias, d)
        ol, _ = gated_delta_chunked(dirn(q_l), dirn(k_l), dirn(v_l), dirn(g_l), dirn(beta_l), s_ctx)
        o_lat.append(dirn(ol))
        o_ctx.append(dirn(oc))
    y_lat = gdn_output(o_lat[0] + o_lat[1], z_l, g_norm, w_out)
    y_ctx = gdn_output(o_ctx[0] + o_ctx[1], z_c, g_norm, w_out) if need_ctx_out else None
    return y_lat, y_ctx


def window_mean_minus_self(u, radius):
    B, L, G = u.shape
    csum = jnp.concatenate([jnp.zeros((B, 1, G), u.dtype), jnp.cumsum(u, axis=1)], axis=1)
    t = jnp.arange(L)
    lo = jnp.maximum(t - radius, 0)
    hi = jnp.minimum(t + radius + 1, L)
    mean = (csum[:, hi] - csum[:, lo]) / (hi - lo).astype(u.dtype)[None, :, None]
    return mean - u


def pool_branch(h, w_in, w_grp, scale, w_out):
    B, L, _ = h.shape
    u, z = jnp.split(h @ w_in, 2, axis=-1)
    groups = jnp.split(u.astype(F32), len(POOL_WINDOWS), axis=-1)
    m = jnp.stack([window_mean_minus_self(gu, w // 2) for gu, w in zip(groups, POOL_WINDOWS)],
                  axis=2).astype(h.dtype)
    y = jnp.einsum('blgc,gcd->blgd', m, w_grp).reshape(B, L, POOL_WIDTH) * scale
    return (y * jax.nn.silu(z)) @ w_out


def pool_mixer(h_lat, h_ctx, w_in, w_grp, scale, w_out, need_ctx_out):
    y_lat = pool_branch(h_lat, w_in, w_grp, scale, w_out)
    y_ctx = pool_branch(h_ctx, w_in, w_grp, scale, w_out) if need_ctx_out else None
    return y_lat, y_ctx


def setup_inputs(seed: int = 0) -> dict:
    key = jax.random.key(seed)
    ks = iter(jax.random.split(key, 40))

    def normal(shape, scale):
        return jax.random.normal(next(ks), shape, F32) * scale

    def gain(shape):
        return 1.0 + 0.1 * jax.random.normal(next(ks), shape, F32)

    n_mla = len([i for i in range(DEPTH) if i % N_MIXERS == 0])
    n_gdn = len([i for i in range(DEPTH) if i % N_MIXERS == 1])
    n_pool = len([i for i in range(DEPTH) if i % N_MIXERS == 2])
    a_decay = jax.random.uniform(next(ks), (n_gdn, 2, GDN_V_HEADS), F32, 1.0, 16.0)
    dt = jnp.exp(jax.random.uniform(next(ks), (n_gdn, 2, GDN_V_HEADS), F32,
                                    math.log(1e-3), math.log(1e-1)))
    return {
        'x': normal((BATCH, SEQ, D_MODEL), 1.0),
        'c': normal((BATCH, D_MODEL), 1.0),
        'ctx': normal((BATCH, CTX_LEN, D_MODEL), 1.0),
        'c_ctx': normal((D_MODEL,), 1.0),
        'ada_w': normal((DEPTH, D_MODEL, 3 * D_MODEL), D_MODEL ** -0.5),
        'ada_b': normal((DEPTH, 3 * D_MODEL), 0.02),
        'norm_pre': gain((DEPTH, D_MODEL)),
        'norm_post': gain((DEPTH, D_MODEL)),
        'mla_w_in': normal((n_mla, D_MODEL, MLA_IN), D_MODEL ** -0.5),
        'mla_g_q': gain((n_mla, MLA_Q_RANK)),
        'mla_w_q_up': normal((n_mla, MLA_Q_RANK, MLA_HEADS * (MLA_NOPE + MLA_ROPE)), MLA_Q_RANK ** -0.5),
        'mla_g_kv': gain((n_mla, MLA_KV_RANK)),
        'mla_w_kv_up': normal((n_mla, MLA_KV_RANK, MLA_HEADS * (MLA_NOPE + MLA_V)), MLA_KV_RANK ** -0.5),
        'mla_w_out': normal((n_mla, MLA_WIDTH, D_MODEL), MLA_WIDTH ** -0.5),
        'gdn_w_in': normal((n_gdn, D_MODEL, GDN_IN), D_MODEL ** -0.5),
        'gdn_conv_w': normal((n_gdn, GDN_CONV, GDN_QKV), GDN_CONV ** -0.5),
        'gdn_a_log': jnp.log(a_decay),
        'gdn_dt_bias': dt + jnp.log(-jnp.expm1(-dt)),
        'gdn_g_norm': gain((n_gdn, GDN_DV)),
        'gdn_w_out': normal((n_gdn, GDN_VW, D_MODEL), GDN_VW ** -0.5),
        'pool_w_in': normal((n_pool, D_MODEL, 2 * POOL_WIDTH), D_MODEL ** -0.5),
        'pool_w_grp': normal((n_pool, len(POOL_WINDOWS), POOL_GROUP, POOL_GROUP), POOL_GROUP ** -0.5),
        'pool_scale': gain((n_pool, POOL_WIDTH)),
        'pool_w_out': normal((n_pool, POOL_WIDTH, D_MODEL), POOL_WIDTH ** -0.5),
    }


def reference(x, c, ctx, c_ctx, ada_w, ada_b, norm_pre, norm_post,
              mla_w_in, mla_g_q, mla_w_q_up, mla_g_kv, mla_w_kv_up, mla_w_out,
              gdn_w_in, gdn_conv_w, gdn_a_log, gdn_dt_bias, gdn_g_norm, gdn_w_out,
              pool_w_in, pool_w_grp, pool_scale, pool_w_out):
    ROWS = x.shape[1] // GRID_W
    cos, sin = axial_rope(ROWS)
    xc = ctx
    counts = [0, 0, 0]
    for i in range(DEPTH):
        kind = i % N_MIXERS
        j = counts[kind]
        counts[kind] += 1
        need_ctx = i < DEPTH - 1
        sh_l, sc_l, gt_l = adaln(c, ada_w[i], ada_b[i])
        sh_c, sc_c, gt_c = adaln(c_ctx[None], ada_w[i], ada_b[i])
        h_l = rms_norm(x, norm_pre[i]) * (1 + sc_l) + sh_l
        h_c = rms_norm(xc, norm_pre[i]) * (1 + sc_c) + sh_c
        if kind == 0:
            y_l, y_c = mla_mixer(h_l, h_c, cos, sin, mla_w_in[j], mla_g_q[j], mla_w_q_up[j],
                                 mla_g_kv[j], mla_w_kv_up[j], mla_w_out[j], need_ctx)
        elif kind == 1:
            y_l, y_c = gdn_mixer(h_l, h_c, gdn_w_in[j], gdn_conv_w[j], gdn_a_log[j], gdn_dt_bias[j],
                                 gdn_g_norm[j], gdn_w_out[j], need_ctx)
        else:
            y_l, y_c = pool_mixer(h_l, h_c, pool_w_in[j], pool_w_grp[j], pool_scale[j], pool_w_out[j], need_ctx)
        x = x + gt_l * rms_norm(y_l, norm_post[i])
        if need_ctx:
            xc = xc + gt_c * rms_norm(y_c, norm_post[i])
    return x
```

```python
import functools
import math

import jax
import jax.numpy as jnp
from jax import lax
from jax.experimental import pallas as pl
from jax.experimental.pallas import tpu as pltpu

F32 = jnp.float32
BF16 = jnp.bfloat16
EPS = 1e-6

HEAD = 128
MLA_ROPE = 64
MLA_KV_RANK = 512
MLA_QK_PAD = 256
ROPE_BASE = 10000.0
GRID_W = 64
GDN_CONV = 5
GDN_CHUNK = 64
POOL_WINDOWS = (2, 4, 8, 16)
SEG_TILE = 256
HALO = 16
VMEM_LIMIT = 56 * 1024 * 1024


def _pick(n, prefs):
    for p in prefs:
        if n % p == 0:
            return p
    return n


def _cparams(sem):
    return pltpu.CompilerParams(dimension_semantics=sem, vmem_limit_bytes=VMEM_LIMIT)


def _dot(a, b):
    return jnp.dot(a, b, preferred_element_type=F32)


def _dot_nt(a, b):
    return lax.dot_general(a, b, (((1,), (1,)), ((), ())), preferred_element_type=F32)


def _silu(x):
    return x * jax.nn.sigmoid(x)


def _adaln_kernel(c_ref, w_ref, b_ref, o_ref):
    a = _silu(c_ref[...]).astype(BF16)
    o_ref[...] = _dot(a, w_ref[...].astype(BF16)) + b_ref[...]


def _adaln(cond8, ada_w, ada_b):
    depth, d, n = ada_w.shape
    tn = _pick(n, (512, 256, 128))
    return pl.pallas_call(
        _adaln_kernel,
        grid=(depth, n // tn),
        in_specs=[pl.BlockSpec((8, d), lambda l, j: (0, 0)),
                  pl.BlockSpec((None, d, tn), lambda l, j: (l, 0, j)),
                  pl.BlockSpec((None, 1, tn), lambda l, j: (l, 0, j))],
        out_specs=pl.BlockSpec((None, 8, tn), lambda l, j: (l, 0, j)),
        out_shape=jax.ShapeDtypeStruct((depth, 8, n), F32),
        compiler_params=_cparams(("arbitrary", "arbitrary")),
        name="adaln",
    )(cond8, ada_w, ada_b.reshape(depth, 1, n))


def _rms(xf):
    return xf * lax.rsqrt(jnp.mean(xf * xf, axis=-1, keepdims=True) + EPS)


def _prenorm_kernel(x_ref, g_ref, mod_ref, o_ref, *, n_lat_blocks, d):
    is_ctx = pl.program_id(0) >= n_lat_blocks
    mod = jnp.where(is_ctx, mod_ref[1:2, :], mod_ref[0:1, :])
    sh, sc = mod[:, :d], mod[:, d:2 * d]
    y = _rms(x_ref[...]) * g_ref[...]
    o_ref[...] = (y * (1.0 + sc) + sh).astype(o_ref.dtype)


def _prenorm(x, g, mod, n_lat):
    m, d = x.shape
    tm = SEG_TILE
    return pl.pallas_call(
        functools.partial(_prenorm_kernel, n_lat_blocks=n_lat // tm, d=d),
        grid=(m // tm,),
        in_specs=[pl.BlockSpec((tm, d), lambda i: (i, 0)),
                  pl.BlockSpec((1, d), lambda i: (0, 0)),
                  pl.BlockSpec((8, 3 * d), lambda i: (0, 0))],
        out_specs=pl.BlockSpec((tm, d), lambda i: (i, 0)),
        out_shape=jax.ShapeDtypeStruct((m, d), BF16),
        compiler_params=_cparams(("arbitrary",)),
        name="prenorm",
    )(x, g.reshape(1, d), mod)


def _residual_kernel(x_ref, y_ref, g_ref, mod_ref, o_ref, *, n_lat_blocks, d):
    is_ctx = pl.program_id(0) >= n_lat_blocks
    mod = jnp.where(is_ctx, mod_ref[1:2, :], mod_ref[0:1, :])
    gt = mod[:, 2 * d:]
    y = _rms(y_ref[...].astype(F32)) * g_ref[...]
    o_ref[...] = x_ref[...] + gt * y


def _residual(x, y, g, mod, n_lat, rows):
    d = x.shape[1]
    tm = SEG_TILE
    return pl.pallas_call(
        functools.partial(_residual_kernel, n_lat_blocks=n_lat // tm, d=d),
        grid=(rows // tm,),
        in_specs=[pl.BlockSpec((tm, d), lambda i: (i, 0)),
                  pl.BlockSpec((tm, d), lambda i: (i, 0)),
                  pl.BlockSpec((1, d), lambda i: (0, 0)),
                  pl.BlockSpec((8, 3 * d), lambda i: (0, 0))],
        out_specs=pl.BlockSpec((tm, d), lambda i: (i, 0)),
        out_shape=jax.ShapeDtypeStruct((rows, d), F32),
        compiler_params=_cparams(("arbitrary",)),
        name="residual",
    )(x, y, g.reshape(1, d), mod)


def _mm_kernel(a_ref, w_ref, *rest, nk, has_gain, epilogue, n_extra):
    pos = 0
    g_ref = rest[pos] if has_gain else None
    pos += int(has_gain)
    extra = rest[pos:pos + n_extra]
    pos += n_extra
    o_ref = rest[pos]
    acc_ref = rest[pos + 1] if nk > 1 else None

    a = a_ref[...]
    if has_gain:
        a = (_rms(a.astype(F32)) * g_ref[...]).astype(BF16)
    r = _dot(a, w_ref[...])

    def finish(res):
        if epilogue is not None:
            res = epilogue(res, *[e[...] for e in extra])
        o_ref[...] = res.astype(o_ref.dtype)

    if nk == 1:
        finish(r)
    else:
        k = pl.program_id(2)

        @pl.when(k == 0)
        def _():
            acc_ref[...] = r

        @pl.when(k > 0)
        def _():
            acc_ref[...] += r

        @pl.when(k == nk - 1)
        def _():
            finish(acc_ref[...])


def _matmul(a, w, *, rows=None, a_col=0, k=None, out_dtype=BF16, gain=None,
            epilogue=None, extras=(), tn=None, tm=None):
    k = w.shape[0] if k is None else k
    n = w.shape[1]
    rows = a.shape[0] if rows is None else rows
    tm = tm or _pick(rows, (768, 1024, 512, 256, 128, 64, 32, 16, 8))
    tn = tn or _pick(n, (512, 256, 128))
    tk = k if k <= 4096 else _pick(k, (4096, 2048, 1024, 512))
    nk = k // tk
    assert a_col % tk == 0 and (gain is None or nk == 1)
    a_cb = a_col // tk
    in_specs = [pl.BlockSpec((tm, tk), lambda i, j, kk: (i, a_cb + kk)),
                pl.BlockSpec((tk, tn), lambda i, j, kk: (kk, j))]
    args = [a, w]
    if gain is not None:
        in_specs.append(pl.BlockSpec((1, tk), lambda i, j, kk: (0, 0)))
        args.append(gain.reshape(1, tk).astype(F32))
    for arr, cols, cb in extras:
        in_specs.append(pl.BlockSpec((tm, cols), lambda i, j, kk, cb=cb: (i, cb(j))))
        args.append(arr)
    scratch = [pltpu.VMEM((tm, tn), F32)] if nk > 1 else []
    return pl.pallas_call(
        functools.partial(_mm_kernel, nk=nk, has_gain=gain is not None,
                          epilogue=epilogue, n_extra=len(extras)),
        grid=(rows // tm, n // tn, nk),
        in_specs=in_specs,
        out_specs=pl.BlockSpec((tm, tn), lambda i, j, kk: (i, j)),
        out_shape=jax.ShapeDtypeStruct((rows, n), out_dtype),
        scratch_shapes=scratch,
        compiler_params=_cparams(("arbitrary", "arbitrary", "arbitrary")),
        name="matmul",
    )(*args)


def _rope_tables(n_lat, n_ctx):
    rows = n_lat // GRID_W
    row = jnp.repeat(jnp.arange(rows, dtype=F32), GRID_W)
    col = jnp.tile(jnp.arange(GRID_W, dtype=F32), rows)
    n_freq = MLA_ROPE // 4
    inv_freq = ROPE_BASE ** (-jnp.arange(n_freq, dtype=F32) / n_freq)
    ang = jnp.concatenate([row[:, None] * inv_freq, col[:, None] * inv_freq], axis=-1)
    ang = jnp.concatenate([ang, jnp.zeros((n_ctx, ang.shape[1]), F32)], axis=0)
    cos, sin = jnp.cos(ang), jnp.sin(ang)
    m = ang.shape[0]
    half = MLA_ROPE // 2
    ones = jnp.ones((m, HEAD), F32)
    z_h = jnp.zeros((m, HEAD), F32)
    z_p = jnp.zeros((m, MLA_QK_PAD - HEAD - MLA_ROPE), F32)
    z_r = jnp.zeros((m, half), F32)
    c = jnp.concatenate([ones, cos, cos, z_p], axis=1)
    s1 = jnp.concatenate([z_h, -sin, z_r, z_p], axis=1)
    s2 = jnp.concatenate([z_h, z_r, sin, z_p], axis=1)
    return c, s1, s2


def _rope_heads(t, c, s1, s2):
    n = t.shape[1]
    reps = n // MLA_QK_PAD
    half = MLA_ROPE // 2
    if reps > 1:
        c, s1, s2 = (jnp.concatenate([u] * reps, axis=1) for u in (c, s1, s2))
    return t * c + pltpu.roll(t, n - half, 1) * s1 + pltpu.roll(t, half, 1) * s2


def _k_epilogue(t, krp):
    reps = t.shape[1] // MLA_QK_PAD
    kr = krp.astype(F32)
    if reps > 1:
        kr = jnp.concatenate([kr] * reps, axis=1)
    return t + kr


def _ckv_epilogue(t, c, s1, s2):
    r = MLA_KV_RANK
    return jnp.concatenate([t[:, :r], _rope_heads(t[:, r:], c, s1, s2)], axis=1)


def _flash_kernel(q_ref, k_ref, v_ref, z_ref, o_ref, *, tk, nk):
    q = q_ref[...]
    tq = q.shape[0]

    def body(c, carry):
        m, l, acc = carry
        start = pl.multiple_of(c * tk, tk)
        kb = k_ref[pl.ds(start, tk), :]
        vb = v_ref[pl.ds(start, tk), :]
        s = _dot_nt(q, kb)
        m_new = jnp.maximum(m, jnp.max(s, axis=-1, keepdims=True))
        p = jnp.exp(s - m_new)
        alpha = jnp.exp(m - m_new)
        l = alpha * l + jnp.sum(p, axis=-1, keepdims=True)
        acc = alpha * acc + _dot(p.astype(BF16), vb)
        return m_new, l, acc

    init = (jnp.full((tq, 1), -jnp.inf, F32), jnp.zeros((tq, 1), F32),
            jnp.zeros((tq, HEAD), F32))
    _, l, acc = lax.fori_loop(0, nk, body, init)
    o = acc / l
    o_ref[...] = (o * _silu(z_ref[...].astype(F32))).astype(o_ref.dtype)


def _flash(q, k, v, zsrc, z_col, og_prev, *, q_row0, n_q, k_row0, n_k, total_rows):
    heads = v.shape[1] // HEAD
    tq = _pick(n_q, (512, 256, 128))
    tk = _pick(n_k, (768, 512, 256, 128))
    assert q_row0 % tq == 0 and k_row0 % n_k == 0
    qb0, kb0, zcb = q_row0 // tq, k_row0 // n_k, z_col // HEAD
    in_specs = [pl.BlockSpec((tq, MLA_QK_PAD), lambda h, i: (qb0 + i, h)),
                pl.BlockSpec((n_k, MLA_QK_PAD), lambda h, i: (kb0, h)),
                pl.BlockSpec((n_k, HEAD), lambda h, i: (kb0, h)),
                pl.BlockSpec((tq, HEAD), lambda h, i: (qb0 + i, zcb + h))]
    args = [q, k, v, zsrc]
    aliases = {}
    if og_prev is not None:
        in_specs.append(pl.BlockSpec(memory_space=pl.ANY))
        args.append(og_prev)
        aliases = {4: 0}

    def kern(q_ref, k_ref, v_ref, z_ref, *rest):
        _flash_kernel(q_ref, k_ref, v_ref, z_ref, rest[-1], tk=tk, nk=n_k // tk)

    return pl.pallas_call(
        kern,
        grid=(heads, n_q // tq),
        in_specs=in_specs,
        out_specs=pl.BlockSpec((tq, HEAD), lambda h, i: (qb0 + i, h)),
        out_shape=jax.ShapeDtypeStruct((total_rows, heads * HEAD), BF16),
        input_output_aliases=aliases,
        compiler_params=_cparams(("arbitrary", "arbitrary")),
        name="mla_flash",
    )(*args)


def _mla_weights(w_in, w_q_up, w_kv_up):
    d = w_in.shape[0]
    q_rank = d // 4
    heads = d // HEAD
    r0, r1, r2 = q_rank, q_rank + MLA_KV_RANK, q_rank + MLA_KV_RANK + MLA_ROPE
    pad = MLA_QK_PAD - HEAD - MLA_ROPE
    scale = (HEAD + MLA_ROPE) ** -0.5
    w_cqz = jnp.concatenate([w_in[:, :r0], w_in[:, r2:]], axis=1).astype(BF16)
    w_ckv = jnp.concatenate([w_in[:, r0:r1], jnp.zeros((d, HEAD), F32), w_in[:, r1:r2],
                             jnp.zeros((d, pad), F32)], axis=1).astype(BF16)
    wq = (w_q_up * scale).reshape(q_rank, heads, HEAD + MLA_ROPE)
    wq = jnp.concatenate([wq, jnp.zeros((q_rank, heads, pad), F32)], axis=2)
    wq = wq.reshape(q_rank, heads * MLA_QK_PAD).astype(BF16)
    wkv = w_kv_up.reshape(MLA_KV_RANK, heads, 2 * HEAD)
    wk = jnp.concatenate([wkv[:, :, :HEAD], jnp.zeros((MLA_KV_RANK, heads, HEAD), F32)], axis=2)
    wk = wk.reshape(MLA_KV_RANK, heads * MLA_QK_PAD).astype(BF16)
    wv = wkv[:, :, HEAD:].reshape(MLA_KV_RANK, heads * HEAD).astype(BF16)
    return w_cqz, w_ckv, wq, wk, wv


def _mla_mixer(h, tables, w_in, g_q, w_q_up, g_kv, w_kv_up, w_out, n_lat, need_ctx):
    m, d = h.shape
    n_ctx = m - n_lat
    q_rank = d // 4
    c, s1, s2 = tables
    w_cqz, w_ckv, wq, wk, wv = _mla_weights(w_in, w_q_up, w_kv_up)
    first = lambda j: 0

    cqz = _matmul(h, w_cqz)
    ckvkr = _matmul(h, w_ckv, epilogue=_ckv_epilogue, tn=w_ckv.shape[1],
                    extras=[(c, MLA_QK_PAD, first), (s1, MLA_QK_PAD, first),
                            (s2, MLA_QK_PAD, first)])
    q = _matmul(cqz, wq, k=q_rank, gain=g_q, epilogue=_rope_heads,
                extras=[(c, MLA_QK_PAD, first), (s1, MLA_QK_PAD, first),
                        (s2, MLA_QK_PAD, first)])
    kr_cb = MLA_KV_RANK // MLA_QK_PAD
    kk = _matmul(ckvkr, wk, k=MLA_KV_RANK, gain=g_kv, epilogue=_k_epilogue,
                 extras=[(ckvkr, MLA_QK_PAD, lambda j: kr_cb)])
    v = _matmul(ckvkr, wv, k=MLA_KV_RANK, gain=g_kv)

    og = _flash(q, kk, v, cqz, q_rank, None, q_row0=0, n_q=n_lat, k_row0=0, n_k=m,
                total_rows=m)
    if need_ctx:
        og = _flash(q, kk, v, cqz, q_rank, og, q_row0=n_lat, n_q=n_ctx, k_row0=n_lat,
                    n_k=n_ctx, total_rows=m)
    rows = m if need_ctx else n_lat
    return _matmul(og, w_out.astype(BF16), rows=rows)


def _fill_window(scr, prev_ref, cur_ref, next_ref):
    tm = cur_ref.shape[0]
    scr[0:HALO, :] = prev_ref[...].astype(F32)
    scr[HALO:HALO + tm, :] = cur_ref[...].astype(F32)
    scr[HALO + tm:, :] = next_ref[...].astype(F32)


def _seg_position(tm, n_lat_blocks, n_lat, n_ctx):
    i = pl.program_id(0)
    is_ctx = i >= n_lat_blocks
    local0 = jnp.where(is_ctx, i - n_lat_blocks, i) * tm
    t = local0 + lax.broadcasted_iota(jnp.int32, (tm, 1), 0)
    return t, jnp.where(is_ctx, n_ctx, n_lat)


def _window_specs(tm, cols, n_rows, col_fn):
    hb = tm // HALO
    last = n_rows // HALO - 1
    return [pl.BlockSpec((HALO, cols), lambda i, j: (jnp.maximum(i * hb - 1, 0), col_fn(j))),
            pl.BlockSpec((tm, cols), lambda i, j: (i, col_fn(j))),
            pl.BlockSpec((HALO, cols), lambda i, j: (jnp.minimum((i + 1) * hb, last), col_fn(j)))]


def _gdn_conv_kernel(prev_ref, cur_ref, next_ref, w_ref, o_ref, scr, *, normalize,
                     n_scaled_blocks, n_lat_blocks, n_lat, n_ctx):
    tm, tc = cur_ref.shape
    _fill_window(scr, prev_ref, cur_ref, next_ref)
    t, seg_len = _seg_position(tm, n_lat_blocks, n_lat, n_ctx)
    r = GDN_CONV // 2
    acc = jnp.zeros((tm, tc), F32)
    for j in range(GDN_CONV):
        dlt = j - r
        term = scr[HALO + dlt:HALO + dlt + tm, :] * w_ref[j:j + 1, :]
        if dlt != 0:
            ok = jnp.logical_and(t + dlt >= 0, t + dlt < seg_len)
            term = jnp.where(ok, term, 0.0)
        acc = acc + term
    y = _silu(acc)
    if normalize:
        scale = jnp.where(pl.program_id(1) < n_scaled_blocks, HEAD ** -0.5, 1.0)
        parts = []
        for hh in range(tc // HEAD):
            u = y[:, hh * HEAD:(hh + 1) * HEAD]
            parts.append(u * (lax.rsqrt(jnp.sum(u * u, axis=-1, keepdims=True) + EPS) * scale))
        y = jnp.concatenate(parts, axis=1) if len(parts) > 1 else parts[0]
    o_ref[...] = y.astype(o_ref.dtype)


def _gdn_conv(src, conv_w, col0, ncols, normalize, n_scaled_cols, n_lat):
    m = src.shape[0]
    tm = SEG_TILE
    tc = _pick(ncols, (512, 256, 128))
    cb0 = col0 // tc
    specs = _window_specs(tm, tc, m, lambda j: cb0 + j)
    specs.append(pl.BlockSpec((8, tc), lambda i, j: (0, cb0 + j)))
    w8 = jnp.concatenate([conv_w, jnp.zeros((8 - GDN_CONV, conv_w.shape[1]), F32)], axis=0)
    return pl.pallas_call(
        functools.partial(_gdn_conv_kernel, normalize=normalize,
                          n_scaled_blocks=n_scaled_cols // tc, n_lat_blocks=n_lat // tm,
                          n_lat=n_lat, n_ctx=m - n_lat),
        grid=(m // tm, ncols // tc),
        in_specs=specs,
        out_specs=pl.BlockSpec((tm, tc), lambda i, j: (i, j)),
        out_shape=jax.ShapeDtypeStruct((m, ncols), BF16),
        scratch_shapes=[pltpu.VMEM((tm + 2 * HALO, tc), F32)],
        compiler_params=_cparams(("arbitrary", "arbitrary")),
        name="gdn_conv",
    )(src, src, src, w8)


def _split3(x):
    x1 = x.astype(BF16)
    r1 = x - x1.astype(F32)
    x2 = r1.astype(BF16)
    x3 = (r1 - x2.astype(F32)).astype(BF16)
    return x1, x2, x3


def _gdn_gate_kernel(ab_ref, alog_ref, dtb_ref, g_ref, beta_ref):
    tm, w = g_ref.shape
    ab = ab_ref[...]
    x = ab[:, :w] + dtb_ref[...]
    softplus = jnp.maximum(x, 0.0) + jnp.log1p(jnp.exp(-jnp.abs(x)))
    g = -jnp.exp(alog_ref[...]) * softplus
    beta_ref[...] = jax.nn.sigmoid(ab[:, w:])
    ch = GDN_CHUNK
    ri = lax.broadcasted_iota(jnp.int32, (ch, ch), 0)
    ci = lax.broadcasted_iota(jnp.int32, (ch, ch), 1)
    lower = (ci <= ri).astype(BF16)
    upper = (ci >= ri).astype(BF16)
    fwd_lane = lax.broadcasted_iota(jnp.int32, (ch, w), 1) < w // 2
    for cidx in range(tm // ch):
        parts = _split3(g[cidx * ch:(cidx + 1) * ch, :])
        pre = _dot(lower, parts[0]) + (_dot(lower, parts[1]) + _dot(lower, parts[2]))
        suf = _dot(upper, parts[0]) + (_dot(upper, parts[1]) + _dot(upper, parts[2]))
        g_ref[cidx * ch:(cidx + 1) * ch, :] = jnp.where(fwd_lane, pre, suf)


def _gdn_gates(ab, a_log, dt_bias):
    m, w4 = ab.shape
    w = w4 // 2
    tm = SEG_TILE
    return pl.pallas_call(
        _gdn_gate_kernel,
        grid=(m // tm,),
        in_specs=[pl.BlockSpec((tm, w4), lambda i: (i, 0)),
                  pl.BlockSpec((1, w), lambda i: (0, 0)),
                  pl.BlockSpec((1, w), lambda i: (0, 0))],
        out_specs=[pl.BlockSpec((tm, w), lambda i: (i, 0)),
                   pl.BlockSpec((tm, w), lambda i: (i, 0))],
        out_shape=[jax.ShapeDtypeStruct((m, w), F32), jax.ShapeDtypeStruct((m, w), F32)],
        compiler_params=_cparams(("arbitrary",)),
        name="gdn_gates",
    )(ab, a_log.reshape(1, w).astype(F32), dt_bias.reshape(1, w).astype(F32))


def _mm3(a, b):
    ah = a.astype(BF16)
    al = (a - ah.astype(F32)).astype(BF16)
    bh = b.astype(BF16)
    bl = (b - bh.astype(F32)).astype(BF16)
    return _dot(ah, bh) + (_dot(ah, bl) + _dot(al, bh))


def _unit_tri_inverse(a, ri, ci):
    n = a.shape[0]
    eye = (ri == ci).astype(F32)
    same16 = (ri // 16) == (ci // 16)
    same32 = (ri // 32) == (ci // 32)
    d = jnp.where(same16, a, 0.0)
    l1 = jnp.where(jnp.logical_and(same32, jnp.logical_not(same16)), a, 0.0)
    l2 = jnp.where(same32, 0.0, a)
    d2 = _mm3(d, d)
    d4 = _mm3(d2, d2)
    d8 = _mm3(d4, d4)
    t = eye - d
    t = t + _mm3(t, d2)
    t = t + _mm3(t, d4)
    t = t + _mm3(t, d8)
    t = t - _mm3(_mm3(t, l1), t)
    t = t - _mm3(_mm3(t, l2), t)
    return t


def _gdn_kernel(q_ref, k_ref, v_ref, col_ref, row_ref, o_ref, state, *, kg):
    d = pl.program_id(0)
    n = pl.program_id(2)
    ch = GDN_CHUNK
    p2 = 2 * ch

    @pl.when(n == 0)
    def _():
        state[...] = jnp.zeros_like(state)

    ri = lax.broadcasted_iota(jnp.int32, (p2, p2), 0)
    ci = lax.broadcasted_iota(jnp.int32, (p2, p2), 1)
    same_head = (ri // ch) == (ci // ch)
    fwd = d == 0
    ahead = jnp.where(fwd, ci - ri, ri - ci)
    incl = jnp.logical_and(same_head, ahead <= 0)
    strict = jnp.logical_and(incl, ci != ri)
    first_rows = lax.broadcasted_iota(jnp.int32, (p2, 1), 0) < ch
    first_lanes = lax.broadcasted_iota(jnp.int32, (1, p2), 1) < ch
    last_tok = jnp.where(fwd, ch - 1, 0)

    for kh in range(kg):
        kf = k_ref[:, kh * HEAD:(kh + 1) * HEAD].astype(F32)
        qf = q_ref[:, kh * HEAD:(kh + 1) * HEAD].astype(F32)
        k2 = jnp.concatenate([kf, kf], axis=0)
        q2 = jnp.concatenate([qf, qf], axis=0)
        kt = k2.T
        ktb = kt.astype(BF16)
        kkqk = _dot(jnp.concatenate([k2, q2], axis=0).astype(BF16), ktb)
        kk, qk = kkqk[:p2], kkqk[p2:]

        a0, a1 = 2 * kh, 2 * kh + 1
        nv = 2 * kg
        gcol = jnp.concatenate([col_ref[:, a0:a0 + 1], col_ref[:, a1:a1 + 1]], axis=0)
        bcol = jnp.concatenate([col_ref[:, nv + a0:nv + a0 + 1],
                                col_ref[:, nv + a1:nv + a1 + 1]], axis=0)
        grow = jnp.concatenate([row_ref[a0:a0 + 1, :], row_ref[a1:a1 + 1, :]], axis=1)
        tot_a = jnp.sum(jnp.where(lax.broadcasted_iota(jnp.int32, (1, ch), 1) == last_tok,
                                  row_ref[a0:a0 + 1, :], 0.0), axis=1, keepdims=True)
        tot_b = jnp.sum(jnp.where(lax.broadcasted_iota(jnp.int32, (1, ch), 1) == last_tok,
                                  row_ref[a1:a1 + 1, :], 0.0), axis=1, keepdims=True)
        tot_col = jnp.where(first_rows, tot_a, tot_b)
        tot_row = jnp.where(first_lanes, tot_a, tot_b)

        decay = jnp.exp(jnp.where(incl, gcol - grow, -jnp.inf))
        amat = jnp.where(strict, bcol * kk * decay, 0.0)
        tinv = _unit_tri_inverse(amat, ri, ci)

        vf = jnp.concatenate([v_ref[:, a0 * HEAD:(a0 + 1) * HEAD],
                              v_ref[:, a1 * HEAD:(a1 + 1) * HEAD]], axis=0).astype(F32)
        eg = jnp.exp(gcol)
        rhs = jnp.concatenate([vf * bcol, k2 * (bcol * eg)], axis=1).astype(BF16)
        uw = _dot(tinv.astype(BF16), rhs)
        u, w = uw[:, :HEAD], uw[:, HEAD:]
        qk_d = jnp.where(incl, qk * decay, 0.0).astype(BF16)
        q_dec = (q2 * eg).astype(BF16)
        w_b = w.astype(BF16)
        kt_dec = kt * jnp.exp(tot_row - grow)
        g_last = jnp.exp(tot_col)

        s_a = state[a0]
        s_b = state[a1]
        ws_a = _dot(jnp.concatenate([w_b[:ch], q_dec[:ch]], axis=0), s_a.astype(BF16))
        ws_b = _dot(jnp.concatenate([w_b[ch:], q_dec[ch:]], axis=0), s_b.astype(BF16))
        v_new = u - jnp.concatenate([ws_a[:ch], ws_b[:ch]], axis=0)
        v_new_b = v_new.astype(BF16)
        o = jnp.concatenate([ws_a[ch:], ws_b[ch:]], axis=0) + _dot(qk_d, v_new_b)
        kd_a = jnp.where(first_lanes, kt_dec, 0.0).astype(BF16)
        kd_b = jnp.where(first_lanes, 0.0, kt_dec).astype(BF16)
        state[a0] = s_a * g_last[0:1, :] + _dot(kd_a, v_new_b)
        state[a1] = s_b * g_last[ch:ch + 1, :] + _dot(kd_b, v_new_b)
        o_ref[:, a0 * HEAD:(a0 + 1) * HEAD] = o[:ch].astype(o_ref.dtype)
        o_ref[:, a1 * HEAD:(a1 + 1) * HEAD] = o[ch:].astype(o_ref.dtype)


def _gdn_scan(qk, v, gcs, beta, n_lat, kg):
    m = v.shape[0]
    vh = v.shape[1] // HEAD
    khs = vh // 2
    ch = GDN_CHUNK
    nch = m // ch
    lat_ch = n_lat // ch
    ngrp = khs // kg
    nv = 2 * kg
    g4 = gcs.reshape(m, 2, ngrp, nv)
    b4 = beta.reshape(m, 2, ngrp, nv)
    cols = jnp.concatenate([g4, b4], axis=3).transpose(1, 2, 0, 3)
    rows = g4.reshape(nch, ch, 2, ngrp, nv).transpose(2, 3, 0, 4, 1)

    def chunk(d, n):
        return jnp.where(d == 0, (n + lat_ch) % nch, nch - 1 - n)

    return pl.pallas_call(
        functools.partial(_gdn_kernel, kg=kg),
        grid=(2, ngrp, nch),
        in_specs=[pl.BlockSpec((ch, kg * HEAD), lambda d, g, n: (chunk(d, n), g)),
                  pl.BlockSpec((ch, kg * HEAD), lambda d, g, n: (chunk(d, n), ngrp + g)),
                  pl.BlockSpec((ch, nv * HEAD), lambda d, g, n: (chunk(d, n), g)),
                  pl.BlockSpec((None, None, ch, 2 * nv), lambda d, g, n: (d, g, chunk(d, n), 0)),
                  pl.BlockSpec((None, None, None, nv, ch), lambda d, g, n: (d, g, chunk(d, n), 0, 0))],
        out_specs=pl.BlockSpec((None, ch, nv * HEAD), lambda d, g, n: (d, chunk(d, n), g)),
        out_shape=jax.ShapeDtypeStruct((2, m, vh * HEAD), BF16),
        scratch_shapes=[pltpu.VMEM((nv, HEAD, HEAD), F32)],
        compiler_params=_cparams(("arbitrary", "arbitrary", "arbitrary")),
        name="gdn_scan",
    )(qk, qk, v, cols, rows)


def _gdn_out_kernel(o_ref, z_ref, g_ref, y_ref):
    o = o_ref[0].astype(F32) + o_ref[1].astype(F32)
    z = z_ref[...].astype(F32)
    g = g_ref[...]
    parts = []
    for hh in range(o.shape[1] // HEAD):
        u = o[:, hh * HEAD:(hh + 1) * HEAD]
        parts.append(_rms(u) * g)
    y = jnp.concatenate(parts, axis=1) if len(parts) > 1 else parts[0]
    y_ref[...] = (y * _silu(z)).astype(y_ref.dtype)


def _gdn_out(o2, zsrc, z_col, g_norm):
    _, m, w = o2.shape
    tm = SEG_TILE
    tc = _pick(w, (1024, 512, 256, 128))
    zcb = z_col // tc
    return pl.pallas_call(
        _gdn_out_kernel,
        grid=(m // tm, w // tc),
        in_specs=[pl.BlockSpec((2, tm, tc), lambda i, j: (0, i, j)),
                  pl.BlockSpec((tm, tc), lambda i, j: (i, zcb + j)),
                  pl.BlockSpec((1, HEAD), lambda i, j: (0, 0))],
        out_specs=pl.BlockSpec((tm, tc), lambda i, j: (i, j)),
        out_shape=jax.ShapeDtypeStruct((m, w), BF16),
        compiler_params=_cparams(("arbitrary", "arbitrary")),
        name="gdn_out",
    )(o2, zsrc, g_norm.reshape(1, HEAD).astype(F32))


def _gdn_mixer(h, w_in, conv_w, a_log, dt_bias, g_norm, w_out, n_lat):
    m, d = h.shape
    khs = d // HEAD
    vhs = 2 * khs
    kw, vw = khs * HEAD, vhs * HEAD
    qkv_w = 2 * kw + vw
    w_big = jnp.concatenate([w_in[:, :qkv_w], w_in[:, qkv_w + 4 * vhs:]], axis=1).astype(BF16)
    w_ab = w_in[:, qkv_w:qkv_w + 4 * vhs].astype(BF16)
    qkvz = _matmul(h, w_big)
    ab = _matmul(h, w_ab, out_dtype=F32, tn=4 * vhs)
    qk = _gdn_conv(qkvz, conv_w, 0, 2 * kw, True, kw, n_lat)
    v = _gdn_conv(qkvz, conv_w, 2 * kw, vw, False, 0, n_lat)
    gcs, beta = _gdn_gates(ab, a_log, dt_bias)
    kg = _pick(khs, (2, 1))
    o2 = _gdn_scan(qk, v, gcs, beta, n_lat, kg)
    y = _gdn_out(o2, qkvz, qkv_w, g_norm)
    return _matmul(y, w_out.astype(BF16))


def _pool_kernel(prev_ref, cur_ref, next_ref, w_ref, sc_ref, z_ref, o_ref, scr, *,
                 n_lat_blocks, n_lat, n_ctx):
    tm, tc = cur_ref.shape
    _fill_window(scr, prev_ref, cur_ref, next_ref)
    t, seg_len = _seg_position(tm, n_lat_blocks, n_lat, n_ctx)
    grp = pl.program_id(1)
    u = scr[HALO:HALO + tm, :]
    r_max = max(POOL_WINDOWS) // 2
    acc = u
    for dlt in range(1, r_max + 1):
        radius_ok = dlt <= jnp.left_shift(1, grp)
        for sgn in (-1, 1):
            s = sgn * dlt
            ok = jnp.logical_and(jnp.logical_and(t + s >= 0, t + s < seg_len), radius_ok)
            acc = acc + jnp.where(ok, scr[HALO + s:HALO + s + tm, :], 0.0)
    radius = jnp.left_shift(1, grp)
    cnt = jnp.minimum(t + radius + 1, seg_len) - jnp.maximum(t - radius, 0)
    mean_minus = acc / cnt.astype(F32) - u
    y = _dot(mean_minus.astype(BF16), w_ref[...]) * sc_ref[...]
    o_ref[...] = (y * _silu(z_ref[...].astype(F32))).astype(o_ref.dtype)


def _pool_mixer(h, w_in, w_grp, scale, w_out, n_lat):
    m, d = h.shape
    ng, gw, _ = w_grp.shape
    assert tuple(w // 2 for w in POOL_WINDOWS) == tuple(2 ** g for g in range(ng))
    uz = _matmul(h, w_in.astype(BF16))
    tm = SEG_TILE
    specs = _window_specs(tm, gw, m, lambda g: g)
    specs += [pl.BlockSpec((None, gw, gw), lambda i, g: (g, 0, 0)),
              pl.BlockSpec((1, gw), lambda i, g: (0, g)),
              pl.BlockSpec((tm, gw), lambda i, g: (i, ng + g))]
    yg = pl.pallas_call(
        functools.partial(_pool_kernel, n_lat_blocks=n_lat // tm, n_lat=n_lat, n_ctx=m - n_lat),
        grid=(m // tm, ng),
        in_specs=specs,
        out_specs=pl.BlockSpec((tm, gw), lambda i, g: (i, g)),
        out_shape=jax.ShapeDtypeStruct((m, d), BF16),
        scratch_shapes=[pltpu.VMEM((tm + 2 * HALO, gw), F32)],
        compiler_params=_cparams(("arbitrary", "arbitrary")),
        name="pool_group",
    )(uz, uz, uz, w_grp.astype(BF16), scale.reshape(1, d).astype(F32), uz)
    return _matmul(yg, w_out.astype(BF16))


def kernel(x, c, ctx, c_ctx, ada_w, ada_b, norm_pre, norm_post, mla_w_in, mla_g_q, mla_w_q_up,
           mla_g_kv, mla_w_kv_up, mla_w_out, gdn_w_in, gdn_conv_w, gdn_a_log, gdn_dt_bias,
           gdn_g_norm, gdn_w_out, pool_w_in, pool_w_grp, pool_scale, pool_w_out):
    assert x.shape[0] == 1 and ctx.shape[0] == 1
    n_lat, d = x.shape[1], x.shape[2]
    n_ctx = ctx.shape[1]
    depth = ada_w.shape[0]
    n_mixers = 3
    assert n_lat % SEG_TILE == 0 and n_ctx % SEG_TILE == 0

    cond8 = jnp.concatenate([c.reshape(1, d), c_ctx.reshape(1, d), jnp.zeros((6, d), F32)], axis=0)
    mods = _adaln(cond8, ada_w, ada_b)
    tables = _rope_tables(n_lat, n_ctx)
    xs = jnp.concatenate([x[0], ctx[0]], axis=0)

    counts = [0, 0, 0]
    for i in range(depth):
        kind = i % n_mixers
        j = counts[kind]
        counts[kind] += 1
        need_ctx = i < depth - 1
        h = _prenorm(xs, norm_pre[i], mods[i], n_lat)
        if kind == 0:
            y = _mla_mixer(h, tables, mla_w_in[j], mla_g_q[j], mla_w_q_up[j], mla_g_kv[j],
                           mla_w_kv_up[j], mla_w_out[j], n_lat, need_ctx)
        elif kind == 1:
            y = _gdn_mixer(h, gdn_w_in[j], gdn_conv_w[j], gdn_a_log[j], gdn_dt_bias[j],
                           gdn_g_norm[j], gdn_w_out[j], n_lat)
        else:
            y = _pool_mixer(h, pool_w_in[j], pool_w_grp[j], pool_scale[j], pool_w_out[j], n_lat)
        rows = xs.shape[0] if need_ctx else n_lat
        xs = _residual(xs, y, norm_post[i], mods[i], n_lat, rows)
    return xs[None]
```

```python
import functools
import math

import jax
import jax.numpy as jnp
from jax import lax
from jax.experimental import pallas as pl
from jax.experimental.pallas import tpu as pltpu

F32 = jnp.float32
BF16 = jnp.bfloat16
EPS = 1e-6

HEAD = 128
MLA_ROPE = 64
MLA_KV_RANK = 512
MLA_QK_PAD = 256
ROPE_BASE = 10000.0
GRID_W = 64
GDN_CONV = 5
GDN_CHUNK = 64
GDN_PAIRS = 8
POOL_WINDOWS = (2, 4, 8, 16)
SEG_TILE = 256
HALO = 16
VMEM_LIMIT = 56 * 1024 * 1024


def _pick(n, prefs):
    for p in prefs:
        if n % p == 0:
            return p
    return n


def _cparams(sem):
    return pltpu.CompilerParams(dimension_semantics=sem, vmem_limit_bytes=VMEM_LIMIT)


def _dot(a, b):
    return jnp.dot(a, b, preferred_element_type=F32)


def _dot_nt(a, b):
    return lax.dot_general(a, b, (((1,), (1,)), ((), ())), preferred_element_type=F32)


def _silu(x):
    return x * jax.nn.sigmoid(x)


def _adaln_kernel(c_ref, w_ref, b_ref, o_ref):
    a = _silu(c_ref[...]).astype(BF16)
    o_ref[...] = _dot(a, w_ref[...].astype(BF16)) + b_ref[...]


def _adaln(cond8, ada_w, ada_b):
    depth, d, n = ada_w.shape
    tn = _pick(n, (512, 256, 128))
    return pl.pallas_call(
        _adaln_kernel,
        grid=(depth, n // tn),
        in_specs=[pl.BlockSpec((8, d), lambda l, j: (0, 0)),
                  pl.BlockSpec((None, d, tn), lambda l, j: (l, 0, j)),
                  pl.BlockSpec((None, 1, tn), lambda l, j: (l, 0, j))],
        out_specs=pl.BlockSpec((None, 8, tn), lambda l, j: (l, 0, j)),
        out_shape=jax.ShapeDtypeStruct((depth, 8, n), F32),
        compiler_params=_cparams(("arbitrary", "arbitrary")),
        name="adaln",
    )(cond8, ada_w, ada_b.reshape(depth, 1, n))


def _rms(xf):
    return xf * lax.rsqrt(jnp.mean(xf * xf, axis=-1, keepdims=True) + EPS)


def _prenorm_kernel(x_ref, g_ref, mod_ref, o_ref, *, n_lat_blocks, d):
    is_ctx = pl.program_id(0) >= n_lat_blocks
    mod = jnp.where(is_ctx, mod_ref[1:2, :], mod_ref[0:1, :])
    sh, sc = mod[:, :d], mod[:, d:2 * d]
    y = _rms(x_ref[...]) * g_ref[...]
    o_ref[...] = (y * (1.0 + sc) + sh).astype(o_ref.dtype)


def _prenorm(x, g, mod, n_lat):
    m, d = x.shape
    tm = SEG_TILE
    return pl.pallas_call(
        functools.partial(_prenorm_kernel, n_lat_blocks=n_lat // tm, d=d),
        grid=(m // tm,),
        in_specs=[pl.BlockSpec((tm, d), lambda i: (i, 0)),
                  pl.BlockSpec((1, d), lambda i: (0, 0)),
                  pl.BlockSpec((8, 3 * d), lambda i: (0, 0))],
        out_specs=pl.BlockSpec((tm, d), lambda i: (i, 0)),
        out_shape=jax.ShapeDtypeStruct((m, d), BF16),
        compiler_params=_cparams(("arbitrary",)),
        name="prenorm",
    )(x, g.reshape(1, d), mod)


def _residual_kernel(x_ref, y_ref, g_ref, mod_ref, o_ref, *, n_lat_blocks, d):
    is_ctx = pl.program_id(0) >= n_lat_blocks
    mod = jnp.where(is_ctx, mod_ref[1:2, :], mod_ref[0:1, :])
    gt = mod[:, 2 * d:]
    y = _rms(y_ref[...].astype(F32)) * g_ref[...]
    o_ref[...] = x_ref[...] + gt * y


def _residual(x, y, g, mod, n_lat, rows):
    d = x.shape[1]
    tm = SEG_TILE
    return pl.pallas_call(
        functools.partial(_residual_kernel, n_lat_blocks=n_lat // tm, d=d),
        grid=(rows // tm,),
        in_specs=[pl.BlockSpec((tm, d), lambda i: (i, 0)),
                  pl.BlockSpec((tm, d), lambda i: (i, 0)),
                  pl.BlockSpec((1, d), lambda i: (0, 0)),
                  pl.BlockSpec((8, 3 * d), lambda i: (0, 0))],
        out_specs=pl.BlockSpec((tm, d), lambda i: (i, 0)),
        out_shape=jax.ShapeDtypeStruct((rows, d), F32),
        compiler_params=_cparams(("arbitrary",)),
        name="residual",
    )(x, y, g.reshape(1, d), mod)


def _mm_kernel(a_ref, w_ref, *rest, nk, has_gain, epilogue, n_extra):
    pos = 0
    g_ref = rest[pos] if has_gain else None
    pos += int(has_gain)
    extra = rest[pos:pos + n_extra]
    pos += n_extra
    o_ref = rest[pos]
    acc_ref = rest[pos + 1] if nk > 1 else None

    a = a_ref[...]
    if has_gain:
        a = (_rms(a.astype(F32)) * g_ref[...]).astype(BF16)
    r = _dot(a, w_ref[...])

    def finish(res):
        if epilogue is not None:
            res = epilogue(res, *[e[...] for e in extra])
        o_ref[...] = res.astype(o_ref.dtype)

    if nk == 1:
        finish(r)
    else:
        k = pl.program_id(2)

        @pl.when(k == 0)
        def _():
            acc_ref[...] = r

        @pl.when(k > 0)
        def _():
            acc_ref[...] += r

        @pl.when(k == nk - 1)
        def _():
            finish(acc_ref[...])


def _matmul(a, w, *, rows=None, a_col=0, k=None, out_dtype=BF16, gain=None,
            epilogue=None, extras=(), tn=None, tm=None):
    k = w.shape[0] if k is None else k
    n = w.shape[1]
    rows = a.shape[0] if rows is None else rows
    tm = tm or _pick(rows, (768, 1024, 512, 256, 128, 64, 32, 16, 8))
    tn = tn or _pick(n, (512, 256, 128))
    tk = k if k <= 4096 else _pick(k, (4096, 2048, 1024, 512))
    nk = k // tk
    assert a_col % tk == 0 and (gain is None or nk == 1)
    a_cb = a_col // tk
    in_specs = [pl.BlockSpec((tm, tk), lambda i, j, kk: (i, a_cb + kk)),
                pl.BlockSpec((tk, tn), lambda i, j, kk: (kk, j))]
    args = [a, w]
    if gain is not None:
        in_specs.append(pl.BlockSpec((1, tk), lambda i, j, kk: (0, 0)))
        args.append(gain.reshape(1, tk).astype(F32))
    for arr, cols, cb in extras:
        in_specs.append(pl.BlockSpec((tm, cols), lambda i, j, kk, cb=cb: (i, cb(j))))
        args.append(arr)
    scratch = [pltpu.VMEM((tm, tn), F32)] if nk > 1 else []
    return pl.pallas_call(
        functools.partial(_mm_kernel, nk=nk, has_gain=gain is not None,
                          epilogue=epilogue, n_extra=len(extras)),
        grid=(rows // tm, n // tn, nk),
        in_specs=in_specs,
        out_specs=pl.BlockSpec((tm, tn), lambda i, j, kk: (i, j)),
        out_shape=jax.ShapeDtypeStruct((rows, n), out_dtype),
        scratch_shapes=scratch,
        compiler_params=_cparams(("arbitrary", "arbitrary", "arbitrary")),
        name="matmul",
    )(*args)


def _rope_tables(n_lat, n_ctx):
    rows = n_lat // GRID_W
    row = jnp.repeat(jnp.arange(rows, dtype=F32), GRID_W)
    col = jnp.tile(jnp.arange(GRID_W, dtype=F32), rows)
    n_freq = MLA_ROPE // 4
    inv_freq = ROPE_BASE ** (-jnp.arange(n_freq, dtype=F32) / n_freq)
    ang = jnp.concatenate([row[:, None] * inv_freq, col[:, None] * inv_freq], axis=-1)
    ang = jnp.concatenate([ang, jnp.zeros((n_ctx, ang.shape[1]), F32)], axis=0)
    cos, sin = jnp.cos(ang), jnp.sin(ang)
    m = ang.shape[0]
    half = MLA_ROPE // 2
    ones = jnp.ones((m, HEAD), F32)
    z_h = jnp.zeros((m, HEAD), F32)
    z_p = jnp.zeros((m, MLA_QK_PAD - HEAD - MLA_ROPE), F32)
    z_r = jnp.zeros((m, half), F32)
    c = jnp.concatenate([ones, cos, cos, z_p], axis=1)
    s1 = jnp.concatenate([z_h, -sin, z_r, z_p], axis=1)
    s2 = jnp.concatenate([z_h, z_r, sin, z_p], axis=1)
    return c, s1, s2


def _rope_heads(t, c, s1, s2):
    n = t.shape[1]
    reps = n // MLA_QK_PAD
    half = MLA_ROPE // 2
    if reps > 1:
        c, s1, s2 = (jnp.concatenate([u] * reps, axis=1) for u in (c, s1, s2))
    return t * c + pltpu.roll(t, n - half, 1) * s1 + pltpu.roll(t, half, 1) * s2


def _k_epilogue(t, krp):
    reps = t.shape[1] // MLA_QK_PAD
    kr = krp.astype(F32)
    if reps > 1:
        kr = jnp.concatenate([kr] * reps, axis=1)
    return t + kr


def _ckv_epilogue(t, c, s1, s2):
    r = MLA_KV_RANK
    return jnp.concatenate([t[:, :r], _rope_heads(t[:, r:], c, s1, s2)], axis=1)


def _flash_kernel(q_ref, k_ref, v_ref, z_ref, o_ref, *, tk, nk):
    q = q_ref[...]
    tq = q.shape[0]
    ones_col = (lax.broadcasted_iota(jnp.int32, (tk, HEAD), 1) == 0).astype(BF16)
    m = jnp.full((tq, 1), -jnp.inf, F32)
    acc = jnp.zeros((tq, 2 * HEAD), F32)
    s = _dot_nt(q, k_ref[0:tk, :])
    for c in range(nk):
        s_next = _dot_nt(q, k_ref[(c + 1) * tk:(c + 2) * tk, :]) if c + 1 < nk else None
        m_new = jnp.maximum(m, jnp.max(s, axis=-1, keepdims=True))
        p = jnp.exp2(s - m_new)
        alpha = jnp.exp2(m - m_new)
        v_ext = jnp.concatenate([v_ref[c * tk:(c + 1) * tk, :], ones_col], axis=1)
        acc = alpha * acc + _dot(p.astype(BF16), v_ext)
        m, s = m_new, s_next
    o = acc[:, :HEAD] / acc[:, HEAD:HEAD + 1]
    o_ref[...] = (o * _silu(z_ref[...].astype(F32))).astype(o_ref.dtype)


def _flash(q, k, v, zsrc, z_col, og_prev, *, q_row0, n_q, k_row0, n_k, total_rows):
    heads = v.shape[1] // HEAD
    tq = _pick(n_q, (512, 256, 128))
    tk = _pick(n_k, (768, 512, 256, 128))
    assert q_row0 % tq == 0 and k_row0 % n_k == 0
    qb0, kb0, zcb = q_row0 // tq, k_row0 // n_k, z_col // HEAD
    in_specs = [pl.BlockSpec((tq, MLA_QK_PAD), lambda h, i: (qb0 + i, h)),
                pl.BlockSpec((n_k, MLA_QK_PAD), lambda h, i: (kb0, h)),
                pl.BlockSpec((n_k, HEAD), lambda h, i: (kb0, h)),
                pl.BlockSpec((tq, HEAD), lambda h, i: (qb0 + i, zcb + h))]
    args = [q, k, v, zsrc]
    aliases = {}
    if og_prev is not None:
        in_specs.append(pl.BlockSpec(memory_space=pl.ANY))
        args.append(og_prev)
        aliases = {4: 0}

    def kern(q_ref, k_ref, v_ref, z_ref, *rest):
        _flash_kernel(q_ref, k_ref, v_ref, z_ref, rest[-1], tk=tk, nk=n_k // tk)

    return pl.pallas_call(
        kern,
        grid=(heads, n_q // tq),
        in_specs=in_specs,
        out_specs=pl.BlockSpec((tq, HEAD), lambda h, i: (qb0 + i, h)),
        out_shape=jax.ShapeDtypeStruct((total_rows, heads * HEAD), BF16),
        input_output_aliases=aliases,
        compiler_params=_cparams(("arbitrary", "arbitrary")),
        name="mla_flash",
    )(*args)


def _mla_weights(w_in, w_q_up, w_kv_up):
    d = w_in.shape[0]
    q_rank = d // 4
    heads = d // HEAD
    r0, r1, r2 = q_rank, q_rank + MLA_KV_RANK, q_rank + MLA_KV_RANK + MLA_ROPE
    pad = MLA_QK_PAD - HEAD - MLA_ROPE
    scale = (HEAD + MLA_ROPE) ** -0.5 * math.log2(math.e)
    w_cqz = jnp.concatenate([w_in[:, :r0], w_in[:, r2:]], axis=1).astype(BF16)
    w_ckv = jnp.concatenate([w_in[:, r0:r1], jnp.zeros((d, HEAD), F32), w_in[:, r1:r2],
                             jnp.zeros((d, pad), F32)], axis=1).astype(BF16)
    wq = (w_q_up * scale).reshape(q_rank, heads, HEAD + MLA_ROPE)
    wq = jnp.concatenate([wq, jnp.zeros((q_rank, heads, pad), F32)], axis=2)
    wq = wq.reshape(q_rank, heads * MLA_QK_PAD).astype(BF16)
    wkv = w_kv_up.reshape(MLA_KV_RANK, heads, 2 * HEAD)
    wk = jnp.concatenate([wkv[:, :, :HEAD], jnp.zeros((MLA_KV_RANK, heads, HEAD), F32)], axis=2)
    wk = wk.reshape(MLA_KV_RANK, heads * MLA_QK_PAD).astype(BF16)
    wv = wkv[:, :, HEAD:].reshape(MLA_KV_RANK, heads * HEAD).astype(BF16)
    return w_cqz, w_ckv, wq, wk, wv


def _mla_mixer(h, tables, w_in, g_q, w_q_up, g_kv, w_kv_up, w_out, n_lat, need_ctx):
    m, d = h.shape
    n_ctx = m - n_lat
    q_rank = d // 4
    c, s1, s2 = tables
    w_cqz, w_ckv, wq, wk, wv = _mla_weights(w_in, w_q_up, w_kv_up)
    first = lambda j: 0

    cqz = _matmul(h, w_cqz)
    ckvkr = _matmul(h, w_ckv, epilogue=_ckv_epilogue, tn=w_ckv.shape[1],
                    extras=[(c, MLA_QK_PAD, first), (s1, MLA_QK_PAD, first),
                            (s2, MLA_QK_PAD, first)])
    q = _matmul(cqz, wq, k=q_rank, gain=g_q, epilogue=_rope_heads,
                extras=[(c, MLA_QK_PAD, first), (s1, MLA_QK_PAD, first),
                        (s2, MLA_QK_PAD, first)])
    kr_cb = MLA_KV_RANK // MLA_QK_PAD
    kk = _matmul(ckvkr, wk, k=MLA_KV_RANK, gain=g_kv, epilogue=_k_epilogue,
                 extras=[(ckvkr, MLA_QK_PAD, lambda j: kr_cb)])
    v = _matmul(ckvkr, wv, k=MLA_KV_RANK, gain=g_kv)

    og = _flash(q, kk, v, cqz, q_rank, None, q_row0=0, n_q=n_lat, k_row0=0, n_k=m,
                total_rows=m)
    if need_ctx:
        og = _flash(q, kk, v, cqz, q_rank, og, q_row0=n_lat, n_q=n_ctx, k_row0=n_lat,
                    n_k=n_ctx, total_rows=m)
    rows = m if need_ctx else n_lat
    return _matmul(og, w_out.astype(BF16), rows=rows)


def _fill_window(scr, prev_ref, cur_ref, next_ref):
    tm = cur_ref.shape[0]
    scr[0:HALO, :] = prev_ref[...].astype(F32)
    scr[HALO:HALO + tm, :] = cur_ref[...].astype(F32)
    scr[HALO + tm:, :] = next_ref[...].astype(F32)


def _seg_position(tm, n_lat_blocks, n_lat, n_ctx):
    i = pl.program_id(0)
    is_ctx = i >= n_lat_blocks
    local0 = jnp.where(is_ctx, i - n_lat_blocks, i) * tm
    t = local0 + lax.broadcasted_iota(jnp.int32, (tm, 1), 0)
    return t, jnp.where(is_ctx, n_ctx, n_lat)


def _window_specs(tm, cols, n_rows, col_fn):
    hb = tm // HALO
    last = n_rows // HALO - 1
    return [pl.BlockSpec((HALO, cols), lambda i, j: (jnp.maximum(i * hb - 1, 0), col_fn(j))),
            pl.BlockSpec((tm, cols), lambda i, j: (i, col_fn(j))),
            pl.BlockSpec((HALO, cols), lambda i, j: (jnp.minimum((i + 1) * hb, last), col_fn(j)))]


def _gdn_conv_kernel(prev_ref, cur_ref, next_ref, w_ref, o_ref, scr, *, normalize,
                     n_scaled_blocks, n_lat_blocks, n_lat, n_ctx):
    tm, tc = cur_ref.shape
    _fill_window(scr, prev_ref, cur_ref, next_ref)
    t, seg_len = _seg_position(tm, n_lat_blocks, n_lat, n_ctx)
    r = GDN_CONV // 2
    acc = jnp.zeros((tm, tc), F32)
    for j in range(GDN_CONV):
        dlt = j - r
        term = scr[HALO + dlt:HALO + dlt + tm, :] * w_ref[j:j + 1, :]
        if dlt != 0:
            ok = jnp.logical_and(t + dlt >= 0, t + dlt < seg_len)
            term = jnp.where(ok, term, 0.0)
        acc = acc + term
    y = _silu(acc)
    if normalize:
        scale = jnp.where(pl.program_id(1) < n_scaled_blocks, HEAD ** -0.5, 1.0)
        parts = []
        for hh in range(tc // HEAD):
            u = y[:, hh * HEAD:(hh + 1) * HEAD]
            parts.append(u * (lax.rsqrt(jnp.sum(u * u, axis=-1, keepdims=True) + EPS) * scale))
        y = jnp.concatenate(parts, axis=1) if len(parts) > 1 else parts[0]
    o_ref[...] = y.astype(o_ref.dtype)


def _gdn_conv(src, conv_w, col0, ncols, normalize, n_scaled_cols, n_lat):
    m = src.shape[0]
    tm = SEG_TILE
    tc = _pick(ncols, (512, 256, 128))
    cb0 = col0 // tc
    specs = _window_specs(tm, tc, m, lambda j: cb0 + j)
    specs.append(pl.BlockSpec((8, tc), lambda i, j: (0, cb0 + j)))
    w8 = jnp.concatenate([conv_w, jnp.zeros((8 - GDN_CONV, conv_w.shape[1]), F32)], axis=0)
    return pl.pallas_call(
        functools.partial(_gdn_conv_kernel, normalize=normalize,
                          n_scaled_blocks=n_scaled_cols // tc, n_lat_blocks=n_lat // tm,
                          n_lat=n_lat, n_ctx=m - n_lat),
        grid=(m // tm, ncols // tc),
        in_specs=specs,
        out_specs=pl.BlockSpec((tm, tc), lambda i, j: (i, j)),
        out_shape=jax.ShapeDtypeStruct((m, ncols), BF16),
        scratch_shapes=[pltpu.VMEM((tm + 2 * HALO, tc), F32)],
        compiler_params=_cparams(("arbitrary", "arbitrary")),
        name="gdn_conv",
    )(src, src, src, w8)


def _split3(x):
    x1 = x.astype(BF16)
    r1 = x - x1.astype(F32)
    x2 = r1.astype(BF16)
    x3 = (r1 - x2.astype(F32)).astype(BF16)
    return x1, x2, x3


def _gdn_gate_kernel(ab_ref, alog_ref, dtb_ref, g_ref, beta_ref):
    tm, w = g_ref.shape
    ab = ab_ref[...]
    x = ab[:, :w] + dtb_ref[...]
    softplus = jnp.maximum(x, 0.0) + jnp.log1p(jnp.exp(-jnp.abs(x)))
    g = -jnp.exp(alog_ref[...]) * softplus
    beta_ref[...] = jax.nn.sigmoid(ab[:, w:])
    ch = GDN_CHUNK
    ri = lax.broadcasted_iota(jnp.int32, (ch, ch), 0)
    ci = lax.broadcasted_iota(jnp.int32, (ch, ch), 1)
    lower = (ci <= ri).astype(BF16)
    upper = (ci >= ri).astype(BF16)
    fwd_lane = lax.broadcasted_iota(jnp.int32, (ch, w), 1) < w // 2
    for cidx in range(tm // ch):
        parts = _split3(g[cidx * ch:(cidx + 1) * ch, :])
        pre = _dot(lower, parts[0]) + (_dot(lower, parts[1]) + _dot(lower, parts[2]))
        suf = _dot(upper, parts[0]) + (_dot(upper, parts[1]) + _dot(upper, parts[2]))
        g_ref[cidx * ch:(cidx + 1) * ch, :] = jnp.where(fwd_lane, pre, suf)


def _gdn_gates(ab, a_log, dt_bias):
    m, w4 = ab.shape
    w = w4 // 2
    tm = SEG_TILE
    return pl.pallas_call(
        _gdn_gate_kernel,
        grid=(m // tm,),
        in_specs=[pl.BlockSpec((tm, w4), lambda i: (i, 0)),
                  pl.BlockSpec((1, w), lambda i: (0, 0)),
                  pl.BlockSpec((1, w), lambda i: (0, 0))],
        out_specs=[pl.BlockSpec((tm, w), lambda i: (i, 0)),
                   pl.BlockSpec((tm, w), lambda i: (i, 0))],
        out_shape=[jax.ShapeDtypeStruct((m, w), F32), jax.ShapeDtypeStruct((m, w), F32)],
        compiler_params=_cparams(("arbitrary",)),
        name="gdn_gates",
    )(ab, a_log.reshape(1, w).astype(F32), dt_bias.reshape(1, w).astype(F32))


def _bmm(a, b):
    return lax.dot_general(a, b, (((2,), (1,)), ((0,), (0,))), preferred_element_type=F32)


def _unit_tri_inverse(a, ri, ci):
    def same(s):
        return (ri // s) == (ci // s)

    t = (ri == ci).astype(F32) - jnp.where(same(2), a, 0.0)
    s = 2
    while s < GDN_CHUNK:
        coupling = jnp.logical_and(same(2 * s), jnp.logical_not(same(s)))
        tb = t.astype(BF16)
        tl = _bmm(tb, jnp.where(coupling, a, 0.0).astype(BF16))
        t = t - _bmm(tl.astype(BF16), tb)
        s *= 2
    return t


def _gdn_kernel(q_ref, k_ref, v_ref, col_ref, row_ref, o_ref, state, *, kg):
    d = pl.program_id(0)
    n = pl.program_id(2)
    ch = GDN_CHUNK
    p2 = 2 * ch
    nv = 2 * kg

    @pl.when(n == 0)
    def _():
        state[...] = jnp.zeros_like(state)

    ri = lax.broadcasted_iota(jnp.int32, (1, p2, p2), 1)
    ci = lax.broadcasted_iota(jnp.int32, (1, p2, p2), 2)
    same_head = (ri // ch) == (ci // ch)
    fwd = d == 0
    ahead = jnp.where(fwd, ci - ri, ri - ci)
    incl = jnp.logical_and(same_head, ahead <= 0)
    strict = jnp.logical_and(incl, ci != ri)
    first_lanes = lax.broadcasted_iota(jnp.int32, (1, 1, p2), 2) < ch

    def pair_rows(ref):
        x = jnp.stack([ref[:, p * HEAD:(p + 1) * HEAD] for p in range(kg)]).astype(F32)
        return jnp.concatenate([x, x], axis=1)

    k2 = pair_rows(k_ref)
    q2 = pair_rows(q_ref)
    vf = jnp.stack([jnp.concatenate([v_ref[:, (2 * p) * HEAD:(2 * p + 1) * HEAD],
                                     v_ref[:, (2 * p + 1) * HEAD:(2 * p + 2) * HEAD]], axis=0)
                    for p in range(kg)]).astype(F32)
    kt = jnp.stack([k2[p].T for p in range(kg)])
    kkqk = _bmm(jnp.concatenate([k2, q2], axis=1).astype(BF16), kt.astype(BF16))
    kk, qk = kkqk[:, :p2], kkqk[:, p2:]

    def col(j0):
        return jnp.stack([jnp.concatenate([col_ref[:, j0 + 2 * p:j0 + 2 * p + 1],
                                           col_ref[:, j0 + 2 * p + 1:j0 + 2 * p + 2]], axis=0)
                          for p in range(kg)])

    gcol, bcol = col(0), col(nv)
    grow = jnp.stack([jnp.concatenate([row_ref[2 * p:2 * p + 1, :], row_ref[2 * p + 1:2 * p + 2, :]],
                                      axis=1) for p in range(kg)])
    tot_a = jnp.where(fwd, grow[:, :, ch - 1:ch], grow[:, :, 0:1])
    tot_b = jnp.where(fwd, grow[:, :, p2 - 1:p2], grow[:, :, ch:ch + 1])
    tot_row = jnp.where(first_lanes, tot_a, tot_b)

    decay = jnp.exp(jnp.where(incl, gcol - grow, -jnp.inf))
    tinv = _unit_tri_inverse(jnp.where(strict, bcol * kk * decay, 0.0), ri, ci)

    eg = jnp.exp(gcol)
    rhs = jnp.concatenate([vf * bcol, k2 * (bcol * eg)], axis=2).astype(BF16)
    uw = _bmm(tinv.astype(BF16), rhs)
    u, w_b = uw[:, :, :HEAD], uw[:, :, HEAD:].astype(BF16)
    qk_d = jnp.where(incl, qk * decay, 0.0).astype(BF16)
    q_dec = (q2 * eg).astype(BF16)
    kt_dec = kt * jnp.exp(tot_row - grow)

    s_a = state[:, 0]
    s_b = state[:, 1]
    ws_a = _bmm(jnp.concatenate([w_b[:, :ch], q_dec[:, :ch]], axis=1), s_a.astype(BF16))
    ws_b = _bmm(jnp.concatenate([w_b[:, ch:], q_dec[:, ch:]], axis=1), s_b.astype(BF16))
    v_new = (u - jnp.concatenate([ws_a[:, :ch], ws_b[:, :ch]], axis=1)).astype(BF16)
    o = jnp.concatenate([ws_a[:, ch:], ws_b[:, ch:]], axis=1) + _bmm(qk_d, v_new)
    kd_a = jnp.where(first_lanes, kt_dec, 0.0).astype(BF16)
    kd_b = jnp.where(first_lanes, 0.0, kt_dec).astype(BF16)
    state[:, 0] = s_a * jnp.exp(tot_a) + _bmm(kd_a, v_new)
    state[:, 1] = s_b * jnp.exp(tot_b) + _bmm(kd_b, v_new)
    for p in range(kg):
        o_ref[:, (2 * p) * HEAD:(2 * p + 1) * HEAD] = o[p, :ch].astype(o_ref.dtype)
        o_ref[:, (2 * p + 1) * HEAD:(2 * p + 2) * HEAD] = o[p, ch:].astype(o_ref.dtype)


def _gdn_scan(qk, v, gcs, beta, n_lat, kg):
    m = v.shape[0]
    vh = v.shape[1] // HEAD
    khs = vh // 2
    ch = GDN_CHUNK
    nch = m // ch
    lat_ch = n_lat // ch
    ngrp = khs // kg
    nv = 2 * kg
    g4 = gcs.reshape(m, 2, ngrp, nv)
    b4 = beta.reshape(m, 2, ngrp, nv)
    cols = jnp.concatenate([g4, b4], axis=3).transpose(1, 2, 0, 3)
    rows = g4.reshape(nch, ch, 2, ngrp, nv).transpose(2, 3, 0, 4, 1)

    def chunk(d, n):
        return jnp.where(d == 0, (n + lat_ch) % nch, nch - 1 - n)

    return pl.pallas_call(
        functools.partial(_gdn_kernel, kg=kg),
        grid=(2, ngrp, nch),
        in_specs=[pl.BlockSpec((ch, kg * HEAD), lambda d, g, n: (chunk(d, n), g)),
                  pl.BlockSpec((ch, kg * HEAD), lambda d, g, n: (chunk(d, n), ngrp + g)),
                  pl.BlockSpec((ch, nv * HEAD), lambda d, g, n: (chunk(d, n), g)),
                  pl.BlockSpec((None, None, ch, 2 * nv), lambda d, g, n: (d, g, chunk(d, n), 0)),
                  pl.BlockSpec((None, None, None, nv, ch), lambda d, g, n: (d, g, chunk(d, n), 0, 0))],
        out_specs=pl.BlockSpec((None, ch, nv * HEAD), lambda d, g, n: (d, chunk(d, n), g)),
        out_shape=jax.ShapeDtypeStruct((2, m, vh * HEAD), BF16),
        scratch_shapes=[pltpu.VMEM((kg, 2, HEAD, HEAD), F32)],
        compiler_params=_cparams(("arbitrary", "arbitrary", "arbitrary")),
        name="gdn_scan",
    )(qk, qk, v, cols, rows)


def _gdn_out_kernel(o_ref, z_ref, g_ref, y_ref):
    o = o_ref[0].astype(F32) + o_ref[1].astype(F32)
    z = z_ref[...].astype(F32)
    g = g_ref[...]
    parts = []
    for hh in range(o.shape[1] // HEAD):
        u = o[:, hh * HEAD:(hh + 1) * HEAD]
        parts.append(_rms(u) * g)
    y = jnp.concatenate(parts, axis=1) if len(parts) > 1 else parts[0]
    y_ref[...] = (y * _silu(z)).astype(y_ref.dtype)


def _gdn_out(o2, zsrc, z_col, g_norm):
    _, m, w = o2.shape
    tm = SEG_TILE
    tc = _pick(w, (1024, 512, 256, 128))
    zcb = z_col // tc
    return pl.pallas_call(
        _gdn_out_kernel,
        grid=(m // tm, w // tc),
        in_specs=[pl.BlockSpec((2, tm, tc), lambda i, j: (0, i, j)),
                  pl.BlockSpec((tm, tc), lambda i, j: (i, zcb + j)),
                  pl.BlockSpec((1, HEAD), lambda i, j: (0, 0))],
        out_specs=pl.BlockSpec((tm, tc), lambda i, j: (i, j)),
        out_shape=jax.ShapeDtypeStruct((m, w), BF16),
        compiler_params=_cparams(("arbitrary", "arbitrary")),
        name="gdn_out",
    )(o2, zsrc, g_norm.reshape(1, HEAD).astype(F32))


def _gdn_mixer(h, w_in, conv_w, a_log, dt_bias, g_norm, w_out, n_lat):
    m, d = h.shape
    khs = d // HEAD
    vhs = 2 * khs
    kw, vw = khs * HEAD, vhs * HEAD
    qkv_w = 2 * kw + vw
    w_big = jnp.concatenate([w_in[:, :qkv_w], w_in[:, qkv_w + 4 * vhs:]], axis=1).astype(BF16)
    w_ab = w_in[:, qkv_w:qkv_w + 4 * vhs].astype(BF16)
    qkvz = _matmul(h, w_big)
    ab = _matmul(h, w_ab, out_dtype=F32, tn=4 * vhs)
    qk = _gdn_conv(qkvz, conv_w, 0, 2 * kw, True, kw, n_lat)
    v = _gdn_conv(qkvz, conv_w, 2 * kw, vw, False, 0, n_lat)
    gcs, beta = _gdn_gates(ab, a_log, dt_bias)
    kg = _pick(khs, (GDN_PAIRS, 4, 2, 1))
    o2 = _gdn_scan(qk, v, gcs, beta, n_lat, kg)
    y = _gdn_out(o2, qkvz, qkv_w, g_norm)
    return _matmul(y, w_out.astype(BF16))


def _pool_kernel(prev_ref, cur_ref, next_ref, w_ref, sc_ref, z_ref, o_ref, scr, *,
                 n_lat_blocks, n_lat, n_ctx):
    tm, tc = cur_ref.shape
    _fill_window(scr, prev_ref, cur_ref, next_ref)
    t, seg_len = _seg_position(tm, n_lat_blocks, n_lat, n_ctx)
    grp = pl.program_id(1)
    u = scr[HALO:HALO + tm, :]
    r_max = max(POOL_WINDOWS) // 2
    acc = u
    for dlt in range(1, r_max + 1):
        radius_ok = dlt <= jnp.left_shift(1, grp)
        for sgn in (-1, 1):
            s = sgn * dlt
            ok = jnp.logical_and(jnp.logical_and(t + s >= 0, t + s < seg_len), radius_ok)
            acc = acc + jnp.where(ok, scr[HALO + s:HALO + s + tm, :], 0.0)
    radius = jnp.left_shift(1, grp)
    cnt = jnp.minimum(t + radius + 1, seg_len) - jnp.maximum(t - radius, 0)
    mean_minus = acc / cnt.astype(F32) - u
    y = _dot(mean_minus.astype(BF16), w_ref[...]) * sc_ref[...]
    o_ref[...] = (y * _silu(z_ref[...].astype(F32))).astype(o_ref.dtype)


def _pool_mixer(h, w_in, w_grp, scale, w_out, n_lat):
    m, d = h.shape
    ng, gw, _ = w_grp.shape
    assert tuple(w // 2 for w in POOL_WINDOWS) == tuple(2 ** g for g in range(ng))
    uz = _matmul(h, w_in.astype(BF16))
    tm = SEG_TILE
    specs = _window_specs(tm, gw, m, lambda g: g)
    specs += [pl.BlockSpec((None, gw, gw), lambda i, g: (g, 0, 0)),
              pl.BlockSpec((1, gw), lambda i, g: (0, g)),
              pl.BlockSpec((tm, gw), lambda i, g: (i, ng + g))]
    yg = pl.pallas_call(
        functools.partial(_pool_kernel, n_lat_blocks=n_lat // tm, n_lat=n_lat, n_ctx=m - n_lat),
        grid=(m // tm, ng),
        in_specs=specs,
        out_specs=pl.BlockSpec((tm, gw), lambda i, g: (i, g)),
        out_shape=jax.ShapeDtypeStruct((m, d), BF16),
        scratch_shapes=[pltpu.VMEM((tm + 2 * HALO, gw), F32)],
        compiler_params=_cparams(("arbitrary", "arbitrary")),
        name="pool_group",
    )(uz, uz, uz, w_grp.astype(BF16), scale.reshape(1, d).astype(F32), uz)
    return _matmul(yg, w_out.astype(BF16))


def kernel(x, c, ctx, c_ctx, ada_w, ada_b, norm_pre, norm_post, mla_w_in, mla_g_q, mla_w_q_up,
           mla_g_kv, mla_w_kv_up, mla_w_out, gdn_w_in, gdn_conv_w, gdn_a_log, gdn_dt_bias,
           gdn_g_norm, gdn_w_out, pool_w_in, pool_w_grp, pool_scale, pool_w_out):
    assert x.shape[0] == 1 and ctx.shape[0] == 1
    n_lat, d = x.shape[1], x.shape[2]
    n_ctx = ctx.shape[1]
    depth = ada_w.shape[0]
    n_mixers = 3
    assert n_lat % SEG_TILE == 0 and n_ctx % SEG_TILE == 0

    cond8 = jnp.concatenate([c.reshape(1, d), c_ctx.reshape(1, d), jnp.zeros((6, d), F32)], axis=0)
    mods = _adaln(cond8, ada_w, ada_b)
    tables = _rope_tables(n_lat, n_ctx)
    xs = jnp.concatenate([x[0], ctx[0]], axis=0)

    counts = [0, 0, 0]
    for i in range(depth):
        kind = i % n_mixers
        j = counts[kind]
        counts[kind] += 1
        need_ctx = i < depth - 1
        h = _prenorm(xs, norm_pre[i], mods[i], n_lat)
        if kind == 0:
            y = _mla_mixer(h, tables, mla_w_in[j], mla_g_q[j], mla_w_q_up[j], mla_g_kv[j],
                           mla_w_kv_up[j], mla_w_out[j], n_lat, need_ctx)
        elif kind == 1:
            y = _gdn_mixer(h, gdn_w_in[j], gdn_conv_w[j], gdn_a_log[j], gdn_dt_bias[j],
                           gdn_g_norm[j], gdn_w_out[j], n_lat)
        else:
            y = _pool_mixer(h, pool_w_in[j], pool_w_grp[j], pool_scale[j], pool_w_out[j], n_lat)
        rows = xs.shape[0] if need_ctx else n_lat
        xs = _residual(xs, y, norm_post[i], mods[i], n_lat, rows)
    return xs[None]
```

```python
import functools
import math

import jax
import jax.numpy as jnp
from jax import lax
from jax.experimental import pallas as pl
from jax.experimental.pallas import tpu as pltpu

F32 = jnp.float32
BF16 = jnp.bfloat16
EPS = 1e-6

HEAD = 128
MLA_ROPE = 64
MLA_KV_RANK = 512
MLA_QK_PAD = 256
ROPE_BASE = 10000.0
GRID_W = 64
GDN_CONV = 5
GDN_CHUNK = 64
GDN_PAIRS = 16
POOL_WINDOWS = (2, 4, 8, 16)
SEG_TILE = 256
HALO = 16
VMEM_LIMIT = 56 * 1024 * 1024
MM_MAX_TK = 8192
MM_W_TILE_ELEMS = 2 * 1024 * 1024


def _pick(n, prefs):
    for p in prefs:
        if n % p == 0:
            return p
    return n


def _cparams(sem):
    return pltpu.CompilerParams(dimension_semantics=sem, vmem_limit_bytes=VMEM_LIMIT)


def _dot(a, b):
    return jnp.dot(a, b, preferred_element_type=F32)


def _dot_nt(a, b):
    return lax.dot_general(a, b, (((1,), (1,)), ((), ())), preferred_element_type=F32)


def _silu(x):
    return x * jax.nn.sigmoid(x)


def _adaln_kernel(c_ref, w_ref, b_ref, o_ref):
    a = _silu(c_ref[...]).astype(BF16)
    o_ref[...] = _dot(a, w_ref[...].astype(BF16)) + b_ref[...]


def _adaln(cond8, ada_w, ada_b):
    depth, d, n = ada_w.shape
    tn = _pick(n, (512, 256, 128))
    return pl.pallas_call(
        _adaln_kernel,
        grid=(depth, n // tn),
        in_specs=[pl.BlockSpec((8, d), lambda l, j: (0, 0)),
                  pl.BlockSpec((None, d, tn), lambda l, j: (l, 0, j)),
                  pl.BlockSpec((None, 1, tn), lambda l, j: (l, 0, j))],
        out_specs=pl.BlockSpec((None, 8, tn), lambda l, j: (l, 0, j)),
        out_shape=jax.ShapeDtypeStruct((depth, 8, n), F32),
        compiler_params=_cparams(("arbitrary", "arbitrary")),
        name="adaln",
    )(cond8, ada_w, ada_b.reshape(depth, 1, n))


def _rms(xf):
    return xf * lax.rsqrt(jnp.mean(xf * xf, axis=-1, keepdims=True) + EPS)


def _prenorm_kernel(x_ref, g_ref, mod_ref, o_ref, *, n_lat_blocks, d):
    is_ctx = pl.program_id(0) >= n_lat_blocks
    mod = jnp.where(is_ctx, mod_ref[1:2, :], mod_ref[0:1, :])
    sh, sc = mod[:, :d], mod[:, d:2 * d]
    y = _rms(x_ref[...]) * g_ref[...]
    o_ref[...] = (y * (1.0 + sc) + sh).astype(o_ref.dtype)


def _prenorm(x, g, mod, n_lat):
    m, d = x.shape
    tm = SEG_TILE
    return pl.pallas_call(
        functools.partial(_prenorm_kernel, n_lat_blocks=n_lat // tm, d=d),
        grid=(m // tm,),
        in_specs=[pl.BlockSpec((tm, d), lambda i: (i, 0)),
                  pl.BlockSpec((1, d), lambda i: (0, 0)),
                  pl.BlockSpec((8, 3 * d), lambda i: (0, 0))],
        out_specs=pl.BlockSpec((tm, d), lambda i: (i, 0)),
        out_shape=jax.ShapeDtypeStruct((m, d), BF16),
        compiler_params=_cparams(("arbitrary",)),
        name="prenorm",
    )(x, g.reshape(1, d), mod)


def _residual_kernel(x_ref, y_ref, g_ref, mod_ref, o_ref, *, n_lat_blocks, d):
    is_ctx = pl.program_id(0) >= n_lat_blocks
    mod = jnp.where(is_ctx, mod_ref[1:2, :], mod_ref[0:1, :])
    gt = mod[:, 2 * d:]
    y = _rms(y_ref[...].astype(F32)) * g_ref[...]
    o_ref[...] = x_ref[...] + gt * y


def _residual(x, y, g, mod, n_lat, rows):
    d = x.shape[1]
    tm = SEG_TILE
    return pl.pallas_call(
        functools.partial(_residual_kernel, n_lat_blocks=n_lat // tm, d=d),
        grid=(rows // tm,),
        in_specs=[pl.BlockSpec((tm, d), lambda i: (i, 0)),
                  pl.BlockSpec((tm, d), lambda i: (i, 0)),
                  pl.BlockSpec((1, d), lambda i: (0, 0)),
                  pl.BlockSpec((8, 3 * d), lambda i: (0, 0))],
        out_specs=pl.BlockSpec((tm, d), lambda i: (i, 0)),
        out_shape=jax.ShapeDtypeStruct((rows, d), F32),
        compiler_params=_cparams(("arbitrary",)),
        name="residual",
    )(x, y, g.reshape(1, d), mod)


def _mm_kernel(a_ref, w_ref, *rest, nk, has_gain, epilogue, n_extra):
    pos = 0
    g_ref = rest[pos] if has_gain else None
    pos += int(has_gain)
    extra = rest[pos:pos + n_extra]
    pos += n_extra
    o_ref = rest[pos]
    acc_ref = rest[pos + 1] if nk > 1 else None

    a = a_ref[...]
    if has_gain:
        a = (_rms(a.astype(F32)) * g_ref[...]).astype(BF16)
    r = _dot(a, w_ref[...])

    def finish(res):
        if epilogue is not None:
            res = epilogue(res, *[e[...] for e in extra])
        o_ref[...] = res.astype(o_ref.dtype)

    if nk == 1:
        finish(r)
    else:
        k = pl.program_id(2)

        @pl.when(k == 0)
        def _():
            acc_ref[...] = r

        @pl.when(k > 0)
        def _():
            acc_ref[...] += r

        @pl.when(k == nk - 1)
        def _():
            finish(acc_ref[...])


def _matmul(a, w, *, rows=None, a_col=0, k=None, out_dtype=BF16, gain=None,
            epilogue=None, extras=(), tn=None, tm=None):
    k = w.shape[0] if k is None else k
    n = w.shape[1]
    rows = a.shape[0] if rows is None else rows
    tm = tm or _pick(rows, (768, 1024, 512, 256, 128, 64, 32, 16, 8))
    tk = k if k <= MM_MAX_TK else _pick(k, (4096, 2048, 1024, 512))
    nk = k // tk
    tn = tn or _pick(n, [t for t in (1024, 512, 256, 128) if tk * t <= MM_W_TILE_ELEMS] or [128])
    assert a_col % tk == 0 and (gain is None or nk == 1)
    a_cb = a_col // tk
    in_specs = [pl.BlockSpec((tm, tk), lambda i, j, kk: (i, a_cb + kk)),
                pl.BlockSpec((tk, tn), lambda i, j, kk: (kk, j))]
    args = [a, w]
    if gain is not None:
        in_specs.append(pl.BlockSpec((1, tk), lambda i, j, kk: (0, 0)))
        args.append(gain.reshape(1, tk).astype(F32))
    for arr, cols, cb in extras:
        in_specs.append(pl.BlockSpec((tm, cols), lambda i, j, kk, cb=cb: (i, cb(j))))
        args.append(arr)
    scratch = [pltpu.VMEM((tm, tn), F32)] if nk > 1 else []
    return pl.pallas_call(
        functools.partial(_mm_kernel, nk=nk, has_gain=gain is not None,
                          epilogue=epilogue, n_extra=len(extras)),
        grid=(rows // tm, n // tn, nk),
        in_specs=in_specs,
        out_specs=pl.BlockSpec((tm, tn), lambda i, j, kk: (i, j)),
        out_shape=jax.ShapeDtypeStruct((rows, n), out_dtype),
        scratch_shapes=scratch,
        compiler_params=_cparams(("arbitrary", "arbitrary", "arbitrary")),
        name="matmul",
    )(*args)


def _rope_tables(n_lat, n_ctx):
    rows = n_lat // GRID_W
    row = jnp.repeat(jnp.arange(rows, dtype=F32), GRID_W)
    col = jnp.tile(jnp.arange(GRID_W, dtype=F32), rows)
    n_freq = MLA_ROPE // 4
    inv_freq = ROPE_BASE ** (-jnp.arange(n_freq, dtype=F32) / n_freq)
    ang = jnp.concatenate([row[:, None] * inv_freq, col[:, None] * inv_freq], axis=-1)
    ang = jnp.concatenate([ang, jnp.zeros((n_ctx, ang.shape[1]), F32)], axis=0)
    cos, sin = jnp.cos(ang), jnp.sin(ang)
    m = ang.shape[0]
    half = MLA_ROPE // 2
    ones = jnp.ones((m, HEAD), F32)
    z_h = jnp.zeros((m, HEAD), F32)
    z_p = jnp.zeros((m, MLA_QK_PAD - HEAD - MLA_ROPE), F32)
    z_r = jnp.zeros((m, half), F32)
    c = jnp.concatenate([ones, cos, cos, z_p], axis=1)
    s1 = jnp.concatenate([z_h, -sin, z_r, z_p], axis=1)
    s2 = jnp.concatenate([z_h, z_r, sin, z_p], axis=1)
    return c, s1, s2


def _rope_heads(t, c, s1, s2):
    n = t.shape[1]
    reps = n // MLA_QK_PAD
    half = MLA_ROPE // 2
    if reps > 1:
        c, s1, s2 = (jnp.concatenate([u] * reps, axis=1) for u in (c, s1, s2))
    return t * c + pltpu.roll(t, n - half, 1) * s1 + pltpu.roll(t, half, 1) * s2


def _k_epilogue(t, krp):
    reps = t.shape[1] // MLA_QK_PAD
    kr = krp.astype(F32)
    if reps > 1:
        kr = jnp.concatenate([kr] * reps, axis=1)
    return t + kr


def _ckv_epilogue(t, c, s1, s2):
    r = MLA_KV_RANK
    return jnp.concatenate([t[:, :r], _rope_heads(t[:, r:], c, s1, s2)], axis=1)


def _flash_kernel(q_ref, k_ref, v_ref, z_ref, o_ref, *, tk, nk):
    q = q_ref[...]
    tq = q.shape[0]
    ones_col = (lax.broadcasted_iota(jnp.int32, (tk, HEAD), 1) == 0).astype(BF16)
    m = jnp.full((tq, 1), -jnp.inf, F32)
    acc = jnp.zeros((tq, 2 * HEAD), F32)
    s = _dot_nt(q, k_ref[0:tk, :])
    for c in range(nk):
        s_next = _dot_nt(q, k_ref[(c + 1) * tk:(c + 2) * tk, :]) if c + 1 < nk else None
        m_new = jnp.maximum(m, jnp.max(s, axis=-1, keepdims=True))
        p = jnp.exp2(s - m_new)
        alpha = jnp.exp2(m - m_new)
        v_ext = jnp.concatenate([v_ref[c * tk:(c + 1) * tk, :], ones_col], axis=1)
        acc = alpha * acc + _dot(p.astype(BF16), v_ext)
        m, s = m_new, s_next
    o = acc[:, :HEAD] / acc[:, HEAD:HEAD + 1]
    o_ref[...] = (o * _silu(z_ref[...].astype(F32))).astype(o_ref.dtype)


def _flash(q, k, v, zsrc, z_col, og_prev, *, q_row0, n_q, k_row0, n_k, total_rows):
    heads = v.shape[1] // HEAD
    tq = _pick(n_q, (512, 256, 128))
    tk = _pick(n_k, (768, 512, 256, 128))
    assert q_row0 % tq == 0 and k_row0 % n_k == 0
    qb0, kb0, zcb = q_row0 // tq, k_row0 // n_k, z_col // HEAD
    in_specs = [pl.BlockSpec((tq, MLA_QK_PAD), lambda h, i: (qb0 + i, h)),
                pl.BlockSpec((n_k, MLA_QK_PAD), lambda h, i: (kb0, h)),
                pl.BlockSpec((n_k, HEAD), lambda h, i: (kb0, h)),
                pl.BlockSpec((tq, HEAD), lambda h, i: (qb0 + i, zcb + h))]
    args = [q, k, v, zsrc]
    aliases = {}
    if og_prev is not None:
        in_specs.append(pl.BlockSpec(memory_space=pl.ANY))
        args.append(og_prev)
        aliases = {4: 0}

    def kern(q_ref, k_ref, v_ref, z_ref, *rest):
        _flash_kernel(q_ref, k_ref, v_ref, z_ref, rest[-1], tk=tk, nk=n_k // tk)

    return pl.pallas_call(
        kern,
        grid=(heads, n_q // tq),
        in_specs=in_specs,
        out_specs=pl.BlockSpec((tq, HEAD), lambda h, i: (qb0 + i, h)),
        out_shape=jax.ShapeDtypeStruct((total_rows, heads * HEAD), BF16),
        input_output_aliases=aliases,
        compiler_params=_cparams(("arbitrary", "arbitrary")),
        name="mla_flash",
    )(*args)


def _mla_weights(w_in, w_q_up, w_kv_up):
    d = w_in.shape[0]
    q_rank = d // 4
    heads = d // HEAD
    r0, r1, r2 = q_rank, q_rank + MLA_KV_RANK, q_rank + MLA_KV_RANK + MLA_ROPE
    pad = MLA_QK_PAD - HEAD - MLA_ROPE
    scale = (HEAD + MLA_ROPE) ** -0.5 * math.log2(math.e)
    w_cqz = jnp.concatenate([w_in[:, :r0], w_in[:, r2:]], axis=1).astype(BF16)
    w_ckv = jnp.concatenate([w_in[:, r0:r1], jnp.zeros((d, HEAD), F32), w_in[:, r1:r2],
                             jnp.zeros((d, pad), F32)], axis=1).astype(BF16)
    wq = (w_q_up * scale).reshape(q_rank, heads, HEAD + MLA_ROPE)
    wq = jnp.concatenate([wq, jnp.zeros((q_rank, heads, pad), F32)], axis=2)
    wq = wq.reshape(q_rank, heads * MLA_QK_PAD).astype(BF16)
    wkv = w_kv_up.reshape(MLA_KV_RANK, heads, 2 * HEAD)
    wk = jnp.concatenate([wkv[:, :, :HEAD], jnp.zeros((MLA_KV_RANK, heads, HEAD), F32)], axis=2)
    wk = wk.reshape(MLA_KV_RANK, heads * MLA_QK_PAD).astype(BF16)
    wv = wkv[:, :, HEAD:].reshape(MLA_KV_RANK, heads * HEAD).astype(BF16)
    return w_cqz, w_ckv, wq, wk, wv


def _mla_mixer(h, tables, w_in, g_q, w_q_up, g_kv, w_kv_up, w_out, n_lat, need_ctx):
    m, d = h.shape
    n_ctx = m - n_lat
    q_rank = d // 4
    c, s1, s2 = tables
    w_cqz, w_ckv, wq, wk, wv = _mla_weights(w_in, w_q_up, w_kv_up)
    first = lambda j: 0

    cqz = _matmul(h, w_cqz)
    ckvkr = _matmul(h, w_ckv, epilogue=_ckv_epilogue, tn=w_ckv.shape[1],
                    extras=[(c, MLA_QK_PAD, first), (s1, MLA_QK_PAD, first),
                            (s2, MLA_QK_PAD, first)])
    q = _matmul(cqz, wq, k=q_rank, gain=g_q, epilogue=_rope_heads,
                extras=[(c, MLA_QK_PAD, first), (s1, MLA_QK_PAD, first),
                        (s2, MLA_QK_PAD, first)])
    kr_cb = MLA_KV_RANK // MLA_QK_PAD
    kk = _matmul(ckvkr, wk, k=MLA_KV_RANK, gain=g_kv, epilogue=_k_epilogue,
                 extras=[(ckvkr, MLA_QK_PAD, lambda j: kr_cb)])
    v = _matmul(ckvkr, wv, k=MLA_KV_RANK, gain=g_kv)

    og = _flash(q, kk, v, cqz, q_rank, None, q_row0=0, n_q=n_lat, k_row0=0, n_k=m,
                total_rows=m)
    if need_ctx:
        og = _flash(q, kk, v, cqz, q_rank, og, q_row0=n_lat, n_q=n_ctx, k_row0=n_lat,
                    n_k=n_ctx, total_rows=m)
    rows = m if need_ctx else n_lat
    return _matmul(og, w_out.astype(BF16), rows=rows)


def _fill_window(scr, prev_ref, cur_ref, next_ref, n_lat_blocks):
    tm = cur_ref.shape[0]
    i = pl.program_id(0)
    first = jnp.logical_or(i == 0, i == n_lat_blocks)
    last = jnp.logical_or(i == n_lat_blocks - 1, i == pl.num_programs(0) - 1)
    scr[0:HALO, :] = jnp.where(first, 0.0, prev_ref[...].astype(F32))
    scr[HALO:HALO + tm, :] = cur_ref[...].astype(F32)
    scr[HALO + tm:, :] = jnp.where(last, 0.0, next_ref[...].astype(F32))


def _seg_position(tm, n_lat_blocks, n_lat, n_ctx):
    i = pl.program_id(0)
    is_ctx = i >= n_lat_blocks
    local0 = jnp.where(is_ctx, i - n_lat_blocks, i) * tm
    t = local0 + lax.broadcasted_iota(jnp.int32, (tm, 1), 0)
    return t, jnp.where(is_ctx, n_ctx, n_lat)


def _window_specs(tm, cols, n_rows, col_fn):
    hb = tm // HALO
    last = n_rows // HALO - 1
    return [pl.BlockSpec((HALO, cols), lambda i, j: (jnp.maximum(i * hb - 1, 0), col_fn(j))),
            pl.BlockSpec((tm, cols), lambda i, j: (i, col_fn(j))),
            pl.BlockSpec((HALO, cols), lambda i, j: (jnp.minimum((i + 1) * hb, last), col_fn(j)))]


def _gdn_conv_kernel(prev_ref, cur_ref, next_ref, w_ref, o_ref, scr, *, normalize,
                     n_scaled_blocks, n_lat_blocks):
    tm, tc = cur_ref.shape
    _fill_window(scr, prev_ref, cur_ref, next_ref, n_lat_blocks)
    r = GDN_CONV // 2
    acc = scr[HALO - r:HALO - r + tm, :] * w_ref[0:1, :]
    for j in range(1, GDN_CONV):
        acc = acc + scr[HALO + j - r:HALO + j - r + tm, :] * w_ref[j:j + 1, :]
    y = _silu(acc)
    if normalize:
        scale = jnp.where(pl.program_id(1) < n_scaled_blocks, HEAD ** -0.5, 1.0)
        parts = []
        for hh in range(tc // HEAD):
            u = y[:, hh * HEAD:(hh + 1) * HEAD]
            parts.append(u * (lax.rsqrt(jnp.sum(u * u, axis=-1, keepdims=True) + EPS) * scale))
        y = jnp.concatenate(parts, axis=1) if len(parts) > 1 else parts[0]
    o_ref[...] = y.astype(o_ref.dtype)


def _gdn_conv(src, conv_w, col0, ncols, normalize, n_scaled_cols, n_lat):
    m = src.shape[0]
    tm = SEG_TILE
    tc = _pick(ncols, (512, 256, 128))
    cb0 = col0 // tc
    specs = _window_specs(tm, tc, m, lambda j: cb0 + j)
    specs.append(pl.BlockSpec((8, tc), lambda i, j: (0, cb0 + j)))
    w8 = jnp.concatenate([conv_w, jnp.zeros((8 - GDN_CONV, conv_w.shape[1]), F32)], axis=0)
    return pl.pallas_call(
        functools.partial(_gdn_conv_kernel, normalize=normalize,
                          n_scaled_blocks=n_scaled_cols // tc, n_lat_blocks=n_lat // tm),
        grid=(m // tm, ncols // tc),
        in_specs=specs,
        out_specs=pl.BlockSpec((tm, tc), lambda i, j: (i, j)),
        out_shape=jax.ShapeDtypeStruct((m, ncols), BF16),
        scratch_shapes=[pltpu.VMEM((tm + 2 * HALO, tc), F32)],
        compiler_params=_cparams(("arbitrary", "arbitrary")),
        name="gdn_conv",
    )(src, src, src, w8)


def _split3(x):
    x1 = x.astype(BF16)
    r1 = x - x1.astype(F32)
    x2 = r1.astype(BF16)
    x3 = (r1 - x2.astype(F32)).astype(BF16)
    return x1, x2, x3


def _gdn_gate_kernel(ab_ref, alog_ref, dtb_ref, g_ref, beta_ref):
    tm, w = g_ref.shape
    ab = ab_ref[...]
    x = ab[:, :w] + dtb_ref[...]
    softplus = jnp.maximum(x, 0.0) + jnp.log1p(jnp.exp(-jnp.abs(x)))
    g = -jnp.exp(alog_ref[...]) * softplus
    beta_ref[...] = jax.nn.sigmoid(ab[:, w:])
    ch = GDN_CHUNK
    ri = lax.broadcasted_iota(jnp.int32, (ch, ch), 0)
    ci = lax.broadcasted_iota(jnp.int32, (ch, ch), 1)
    lower = (ci <= ri).astype(BF16)
    upper = (ci >= ri).astype(BF16)
    fwd_lane = lax.broadcasted_iota(jnp.int32, (ch, w), 1) < w // 2
    for cidx in range(tm // ch):
        parts = _split3(g[cidx * ch:(cidx + 1) * ch, :])
        pre = _dot(lower, parts[0]) + (_dot(lower, parts[1]) + _dot(lower, parts[2]))
        suf = _dot(upper, parts[0]) + (_dot(upper, parts[1]) + _dot(upper, parts[2]))
        g_ref[cidx * ch:(cidx + 1) * ch, :] = jnp.where(fwd_lane, pre, suf)


def _gdn_gates(ab, a_log, dt_bias):
    m, w4 = ab.shape
    w = w4 // 2
    tm = SEG_TILE
    return pl.pallas_call(
        _gdn_gate_kernel,
        grid=(m // tm,),
        in_specs=[pl.BlockSpec((tm, w4), lambda i: (i, 0)),
                  pl.BlockSpec((1, w), lambda i: (0, 0)),
                  pl.BlockSpec((1, w), lambda i: (0, 0))],
        out_specs=[pl.BlockSpec((tm, w), lambda i: (i, 0)),
                   pl.BlockSpec((tm, w), lambda i: (i, 0))],
        out_shape=[jax.ShapeDtypeStruct((m, w), F32), jax.ShapeDtypeStruct((m, w), F32)],
        compiler_params=_cparams(("arbitrary",)),
        name="gdn_gates",
    )(ab, a_log.reshape(1, w).astype(F32), dt_bias.reshape(1, w).astype(F32))


def _bmm(a, b):
    return lax.dot_general(a, b, (((2,), (1,)), ((0,), (0,))), preferred_element_type=F32)


def _unit_tri_inverse(a, ri, ci):
    def same(s):
        return (ri // s) == (ci // s)

    t = (ri == ci).astype(F32) - jnp.where(same(2), a, 0.0)
    s = 2
    while s < GDN_CHUNK:
        coupling = jnp.logical_and(same(2 * s), jnp.logical_not(same(s)))
        tb = t.astype(BF16)
        tl = _bmm(tb, jnp.where(coupling, a, 0.0).astype(BF16))
        t = t - _bmm(tl.astype(BF16), tb)
        s *= 2
    return t


def _gdn_kernel(q_ref, k_ref, v_ref, col_ref, row_ref, o_ref, state, *, kg):
    d = pl.program_id(0)
    n = pl.program_id(2)
    ch = GDN_CHUNK
    p2 = 2 * ch
    nv = 2 * kg

    @pl.when(n == 0)
    def _():
        state[...] = jnp.zeros_like(state)

    ri = lax.broadcasted_iota(jnp.int32, (1, p2, p2), 1)
    ci = lax.broadcasted_iota(jnp.int32, (1, p2, p2), 2)
    same_head = (ri // ch) == (ci // ch)
    fwd = d == 0
    ahead = jnp.where(fwd, ci - ri, ri - ci)
    incl = jnp.logical_and(same_head, ahead <= 0)
    strict = jnp.logical_and(incl, ci != ri)
    first_lanes = lax.broadcasted_iota(jnp.int32, (1, 1, p2), 2) < ch

    def pair_rows(ref):
        x = jnp.stack([ref[:, p * HEAD:(p + 1) * HEAD] for p in range(kg)]).astype(F32)
        return jnp.concatenate([x, x], axis=1)

    k2 = pair_rows(k_ref)
    q2 = pair_rows(q_ref)
    vf = jnp.stack([jnp.concatenate([v_ref[:, (2 * p) * HEAD:(2 * p + 1) * HEAD],
                                     v_ref[:, (2 * p + 1) * HEAD:(2 * p + 2) * HEAD]], axis=0)
                    for p in range(kg)]).astype(F32)
    kt = jnp.stack([k2[p].T for p in range(kg)])
    kkqk = _bmm(jnp.concatenate([k2, q2], axis=1).astype(BF16), kt.astype(BF16))
    kk, qk = kkqk[:, :p2], kkqk[:, p2:]

    def col(j0):
        c = jnp.stack([col_ref[:, j0 + p:j0 + p + 1] for p in range(kg)])
        return jnp.broadcast_to(c, (kg, p2, p2))

    gcol, bcol = col(0), col(kg)
    grow = jnp.stack([row_ref[p:p + 1, :] for p in range(kg)])
    tot_a = jnp.where(fwd, grow[:, :, ch - 1:ch], grow[:, :, 0:1])
    tot_b = jnp.where(fwd, grow[:, :, p2 - 1:p2], grow[:, :, ch:ch + 1])
    tot_row = jnp.where(first_lanes, tot_a, tot_b)

    decay = jnp.exp(jnp.where(incl, gcol - grow, -jnp.inf))
    tinv = _unit_tri_inverse(jnp.where(strict, bcol * kk * decay, 0.0), ri, ci)

    eg = jnp.exp(gcol)
    rhs = jnp.concatenate([vf * bcol, k2 * (bcol * eg)], axis=2).astype(BF16)
    uw = _bmm(tinv.astype(BF16), rhs)
    u, w_b = uw[:, :, :HEAD], uw[:, :, HEAD:].astype(BF16)
    qk_d = jnp.where(incl, qk * decay, 0.0).astype(BF16)
    q_dec = (q2 * eg).astype(BF16)
    kt_dec = kt * jnp.exp(tot_row - grow)

    s_a = state[:, 0]
    s_b = state[:, 1]
    ws_a = _bmm(jnp.concatenate([w_b[:, :ch], q_dec[:, :ch]], axis=1), s_a.astype(BF16))
    ws_b = _bmm(jnp.concatenate([w_b[:, ch:], q_dec[:, ch:]], axis=1), s_b.astype(BF16))
    v_new = (u - jnp.concatenate([ws_a[:, :ch], ws_b[:, :ch]], axis=1)).astype(BF16)
    o = jnp.concatenate([ws_a[:, ch:], ws_b[:, ch:]], axis=1) + _bmm(qk_d, v_new)
    kd_a = jnp.where(first_lanes, kt_dec, 0.0).astype(BF16)
    kd_b = jnp.where(first_lanes, 0.0, kt_dec).astype(BF16)
    state[:, 0] = s_a * jnp.exp(tot_a) + _bmm(kd_a, v_new)
    state[:, 1] = s_b * jnp.exp(tot_b) + _bmm(kd_b, v_new)
    for p in range(kg):
        o_ref[:, (2 * p) * HEAD:(2 * p + 1) * HEAD] = o[p, :ch].astype(o_ref.dtype)
        o_ref[:, (2 * p + 1) * HEAD:(2 * p + 2) * HEAD] = o[p, ch:].astype(o_ref.dtype)


def _gdn_scan(qk, v, gcs, beta, n_lat, kg):
    m = v.shape[0]
    vh = v.shape[1] // HEAD
    khs = vh // 2
    ch = GDN_CHUNK
    nch = m // ch
    lat_ch = n_lat // ch
    ngrp = khs // kg
    nv = 2 * kg
    assert HEAD == 2 * ch
    g6 = gcs.reshape(nch, ch, 2, ngrp, kg, 2)
    b6 = beta.reshape(nch, ch, 2, ngrp, kg, 2)
    to_cols = lambda t: t.transpose(2, 3, 0, 5, 1, 4).reshape(2, ngrp, nch, 2 * ch, kg)
    cols = jnp.concatenate([to_cols(g6), to_cols(b6)], axis=4)
    rows = g6.transpose(2, 3, 0, 4, 5, 1).reshape(2, ngrp, nch, kg, 2 * ch)

    def chunk(d, n):
        return jnp.where(d == 0, (n + lat_ch) % nch, nch - 1 - n)

    return pl.pallas_call(
        functools.partial(_gdn_kernel, kg=kg),
        grid=(2, ngrp, nch),
        in_specs=[pl.BlockSpec((ch, kg * HEAD), lambda d, g, n: (chunk(d, n), g)),
                  pl.BlockSpec((ch, kg * HEAD), lambda d, g, n: (chunk(d, n), ngrp + g)),
                  pl.BlockSpec((ch, nv * HEAD), lambda d, g, n: (chunk(d, n), g)),
                  pl.BlockSpec((None, None, None, 2 * ch, 2 * kg),
                               lambda d, g, n: (d, g, chunk(d, n), 0, 0)),
                  pl.BlockSpec((None, None, None, kg, 2 * ch),
                               lambda d, g, n: (d, g, chunk(d, n), 0, 0))],
        out_specs=pl.BlockSpec((None, ch, nv * HEAD), lambda d, g, n: (d, chunk(d, n), g)),
        out_shape=jax.ShapeDtypeStruct((2, m, vh * HEAD), BF16),
        scratch_shapes=[pltpu.VMEM((kg, 2, HEAD, HEAD), F32)],
        compiler_params=_cparams(("arbitrary", "arbitrary", "arbitrary")),
        name="gdn_scan",
    )(qk, qk, v, cols, rows)


def _gdn_out_kernel(o_ref, z_ref, g_ref, y_ref):
    o = o_ref[0].astype(F32) + o_ref[1].astype(F32)
    z = z_ref[...].astype(F32)
    g = g_ref[...]
    parts = []
    for hh in range(o.shape[1] // HEAD):
        u = o[:, hh * HEAD:(hh + 1) * HEAD]
        parts.append(_rms(u) * g)
    y = jnp.concatenate(parts, axis=1) if len(parts) > 1 else parts[0]
    y_ref[...] = (y * _silu(z)).astype(y_ref.dtype)


def _gdn_out(o2, zsrc, z_col, g_norm):
    _, m, w = o2.shape
    tm = SEG_TILE
    tc = _pick(w, (1024, 512, 256, 128))
    zcb = z_col // tc
    return pl.pallas_call(
        _gdn_out_kernel,
        grid=(m // tm, w // tc),
        in_specs=[pl.BlockSpec((2, tm, tc), lambda i, j: (0, i, j)),
                  pl.BlockSpec((tm, tc), lambda i, j: (i, zcb + j)),
                  pl.BlockSpec((1, HEAD), lambda i, j: (0, 0))],
        out_specs=pl.BlockSpec((tm, tc), lambda i, j: (i, j)),
        out_shape=jax.ShapeDtypeStruct((m, w), BF16),
        compiler_params=_cparams(("arbitrary", "arbitrary")),
        name="gdn_out",
    )(o2, zsrc, g_norm.reshape(1, HEAD).astype(F32))


def _gdn_mixer(h, w_in, conv_w, a_log, dt_bias, g_norm, w_out, n_lat):
    m, d = h.shape
    khs = d // HEAD
    vhs = 2 * khs
    kw, vw = khs * HEAD, vhs * HEAD
    qkv_w = 2 * kw + vw
    w_big = jnp.concatenate([w_in[:, :qkv_w], w_in[:, qkv_w + 4 * vhs:]], axis=1).astype(BF16)
    w_ab = w_in[:, qkv_w:qkv_w + 4 * vhs].astype(BF16)
    qkvz = _matmul(h, w_big)
    ab = _matmul(h, w_ab, out_dtype=F32, tn=4 * vhs)
    qk = _gdn_conv(qkvz, conv_w, 0, 2 * kw, True, kw, n_lat)
    v = _gdn_conv(qkvz, conv_w, 2 * kw, vw, False, 0, n_lat)
    gcs, beta = _gdn_gates(ab, a_log, dt_bias)
    kg = _pick(khs, (GDN_PAIRS, 4, 2, 1))
    o2 = _gdn_scan(qk, v, gcs, beta, n_lat, kg)
    y = _gdn_out(o2, qkvz, qkv_w, g_norm)
    return _matmul(y, w_out.astype(BF16))


def _pool_kernel(prev_ref, cur_ref, next_ref, w_ref, sc_ref, z_ref, o_ref, scr, *,
                 n_lat_blocks, n_lat, n_ctx):
    tm, tc = cur_ref.shape
    _fill_window(scr, prev_ref, cur_ref, next_ref, n_lat_blocks)
    t, seg_len = _seg_position(tm, n_lat_blocks, n_lat, n_ctx)
    for g, window in enumerate(POOL_WINDOWS):
        @pl.when(pl.program_id(1) == g)
        def _(radius=window // 2):
            u = scr[HALO:HALO + tm, :]
            acc = u
            for dlt in range(1, radius + 1):
                acc = acc + (scr[HALO - dlt:HALO - dlt + tm, :] + scr[HALO + dlt:HALO + dlt + tm, :])
            cnt = jnp.minimum(t + radius + 1, seg_len) - jnp.maximum(t - radius, 0)
            mean_minus = acc / cnt.astype(F32) - u
            y = _dot(mean_minus.astype(BF16), w_ref[...]) * sc_ref[...]
            o_ref[...] = (y * _silu(z_ref[...].astype(F32))).astype(o_ref.dtype)


def _pool_mixer(h, w_in, w_grp, scale, w_out, n_lat):
    m, d = h.shape
    ng, gw, _ = w_grp.shape
    assert tuple(w // 2 for w in POOL_WINDOWS) == tuple(2 ** g for g in range(ng))
    uz = _matmul(h, w_in.astype(BF16))
    tm = SEG_TILE
    specs = _window_specs(tm, gw, m, lambda g: g)
    specs += [pl.BlockSpec((None, gw, gw), lambda i, g: (g, 0, 0)),
              pl.BlockSpec((1, gw), lambda i, g: (0, g)),
              pl.BlockSpec((tm, gw), lambda i, g: (i, ng + g))]
    yg = pl.pallas_call(
        functools.partial(_pool_kernel, n_lat_blocks=n_lat // tm, n_lat=n_lat, n_ctx=m - n_lat),
        grid=(m // tm, ng),
        in_specs=specs,
        out_specs=pl.BlockSpec((tm, gw), lambda i, g: (i, g)),
        out_shape=jax.ShapeDtypeStruct((m, d), BF16),
        scratch_shapes=[pltpu.VMEM((tm + 2 * HALO, gw), F32)],
        compiler_params=_cparams(("arbitrary", "arbitrary")),
        name="pool_group",
    )(uz, uz, uz, w_grp.astype(BF16), scale.reshape(1, d).astype(F32), uz)
    return _matmul(yg, w_out.astype(BF16))


def kernel(x, c, ctx, c_ctx, ada_w, ada_b, norm_pre, norm_post, mla_w_in, mla_g_q, mla_w_q_up,
           mla_g_kv, mla_w_kv_up, mla_w_out, gdn_w_in, gdn_conv_w, gdn_a_log, gdn_dt_bias,
           gdn_g_norm, gdn_w_out, pool_w_in, pool_w_grp, pool_scale, pool_w_out):
    assert x.shape[0] == 1 and ctx.shape[0] == 1
    n_lat, d = x.shape[1], x.shape[2]
    n_ctx = ctx.shape[1]
    depth = ada_w.shape[0]
    n_mixers = 3
    assert n_lat % SEG_TILE == 0 and n_ctx % SEG_TILE == 0

    cond8 = jnp.concatenate([c.reshape(1, d), c_ctx.reshape(1, d), jnp.zeros((6, d), F32)], axis=0)
    mods = _adaln(cond8, ada_w, ada_b)
    tables = _rope_tables(n_lat, n_ctx)
    xs = jnp.concatenate([x[0], ctx[0]], axis=0)

    counts = [0, 0, 0]
    for i in range(depth):
        kind = i % n_mixers
        j = counts[kind]
        counts[kind] += 1
        need_ctx = i < depth - 1
        h = _prenorm(xs, norm_pre[i], mods[i], n_lat)
        if kind == 0:
            y = _mla_mixer(h, tables, mla_w_in[j], mla_g_q[j], mla_w_q_up[j], mla_g_kv[j],
                           mla_w_kv_up[j], mla_w_out[j], n_lat, need_ctx)
        elif kind == 1:
            y = _gdn_mixer(h, gdn_w_in[j], gdn_conv_w[j], gdn_a_log[j], gdn_dt_bias[j],
                           gdn_g_norm[j], gdn_w_out[j], n_lat)
        else:
            y = _pool_mixer(h, pool_w_in[j], pool_w_grp[j], pool_scale[j], pool_w_out[j], n_lat)
        rows = xs.shape[0] if need_ctx else n_lat
        xs = _residual(xs, y, norm_post[i], mods[i], n_lat, rows)
    return xs[None]
```

```python
import functools
import math

import jax
import jax.numpy as jnp
from jax import lax
from jax.experimental import pallas as pl
from jax.experimental.pallas import tpu as pltpu

F32 = jnp.float32
BF16 = jnp.bfloat16
EPS = 1e-6

HEAD = 128
MLA_ROPE = 64
MLA_KV_RANK = 512
MLA_QK_PAD = 256
ROPE_BASE = 10000.0
GRID_W = 64
GDN_CONV = 5
GDN_CHUNK = 64
GDN_PAIRS = 16
POOL_WINDOWS = (2, 4, 8, 16)
SEG_TILE = 256
HALO = 16
VMEM_LIMIT = 56 * 1024 * 1024
MM_MAX_TK = 8192
MM_W_TILE_ELEMS = 2 * 1024 * 1024


def _pick(n, prefs):
    for p in prefs:
        if n % p == 0:
            return p
    return n


def _cparams(sem):
    return pltpu.CompilerParams(dimension_semantics=sem, vmem_limit_bytes=VMEM_LIMIT)


def _dot(a, b):
    return jnp.dot(a, b, preferred_element_type=F32)


def _dot_nt(a, b):
    return lax.dot_general(a, b, (((1,), (1,)), ((), ())), preferred_element_type=F32)


def _silu(x):
    return x * jax.nn.sigmoid(x)


def _adaln_kernel(c_ref, w_ref, b_ref, o_ref):
    a = _silu(c_ref[...]).astype(BF16)
    o_ref[...] = _dot(a, w_ref[...].astype(BF16)) + b_ref[...]


def _adaln(cond8, ada_w, ada_b):
    depth, d, n = ada_w.shape
    tn = _pick(n, (512, 256, 128))
    return pl.pallas_call(
        _adaln_kernel,
        grid=(depth, n // tn),
        in_specs=[pl.BlockSpec((8, d), lambda l, j: (0, 0)),
                  pl.BlockSpec((None, d, tn), lambda l, j: (l, 0, j)),
                  pl.BlockSpec((None, 1, tn), lambda l, j: (l, 0, j))],
        out_specs=pl.BlockSpec((None, 8, tn), lambda l, j: (l, 0, j)),
        out_shape=jax.ShapeDtypeStruct((depth, 8, n), F32),
        compiler_params=_cparams(("arbitrary", "arbitrary")),
        name="adaln",
    )(cond8, ada_w, ada_b.reshape(depth, 1, n))


def _rms(xf):
    return xf * lax.rsqrt(jnp.mean(xf * xf, axis=-1, keepdims=True) + EPS)


def _prenorm_kernel(x_ref, g_ref, mod_ref, o_ref, *, n_lat_blocks, d):
    is_ctx = pl.program_id(0) >= n_lat_blocks
    mod = jnp.where(is_ctx, mod_ref[1:2, :], mod_ref[0:1, :])
    sh, sc = mod[:, :d], mod[:, d:2 * d]
    y = _rms(x_ref[...]) * g_ref[...]
    o_ref[...] = (y * (1.0 + sc) + sh).astype(o_ref.dtype)


def _prenorm(x, g, mod, n_lat):
    m, d = x.shape
    tm = SEG_TILE
    return pl.pallas_call(
        functools.partial(_prenorm_kernel, n_lat_blocks=n_lat // tm, d=d),
        grid=(m // tm,),
        in_specs=[pl.BlockSpec((tm, d), lambda i: (i, 0)),
                  pl.BlockSpec((1, d), lambda i: (0, 0)),
                  pl.BlockSpec((8, 3 * d), lambda i: (0, 0))],
        out_specs=pl.BlockSpec((tm, d), lambda i: (i, 0)),
        out_shape=jax.ShapeDtypeStruct((m, d), BF16),
        compiler_params=_cparams(("arbitrary",)),
        name="prenorm",
    )(x, g.reshape(1, d), mod)


def _residual_kernel(x_ref, y_ref, g_ref, mod_ref, *rest, n_lat_blocks, d, with_next):
    is_ctx = pl.program_id(0) >= n_lat_blocks
    mod = jnp.where(is_ctx, mod_ref[1:2, :], mod_ref[0:1, :])
    gt = mod[:, 2 * d:]
    y = _rms(y_ref[...].astype(F32)) * g_ref[...]
    x_new = x_ref[...] + gt * y
    if not with_next:
        rest[0][...] = x_new
        return
    gn_ref, modn_ref, o_ref, h_ref = rest
    o_ref[...] = x_new
    modn = jnp.where(is_ctx, modn_ref[1:2, :], modn_ref[0:1, :])
    h = _rms(x_new) * gn_ref[...]
    h_ref[...] = (h * (1.0 + modn[:, d:2 * d]) + modn[:, :d]).astype(h_ref.dtype)


def _residual(x, y, g, mod, n_lat, rows, g_next=None, mod_next=None):
    d = x.shape[1]
    tm = SEG_TILE
    with_next = g_next is not None
    row_spec = pl.BlockSpec((tm, d), lambda i: (i, 0))
    vec_spec = pl.BlockSpec((1, d), lambda i: (0, 0))
    mod_spec = pl.BlockSpec((8, 3 * d), lambda i: (0, 0))
    in_specs = [row_spec, row_spec, vec_spec, mod_spec]
    args = [x, y, g.reshape(1, d), mod]
    out_specs, out_shape = row_spec, jax.ShapeDtypeStruct((rows, d), F32)
    if with_next:
        in_specs += [vec_spec, mod_spec]
        args += [g_next.reshape(1, d), mod_next]
        out_specs = [row_spec, row_spec]
        out_shape = [out_shape, jax.ShapeDtypeStruct((rows, d), BF16)]
    return pl.pallas_call(
        functools.partial(_residual_kernel, n_lat_blocks=n_lat // tm, d=d, with_next=with_next),
        grid=(rows // tm,),
        in_specs=in_specs,
        out_specs=out_specs,
        out_shape=out_shape,
        compiler_params=_cparams(("arbitrary",)),
        name="residual",
    )(*args)


def _mm_kernel(a_ref, w_ref, *rest, nk, has_gain, epilogue, n_extra):
    pos = 0
    g_ref = rest[pos] if has_gain else None
    pos += int(has_gain)
    extra = rest[pos:pos + n_extra]
    pos += n_extra
    o_ref = rest[pos]
    acc_ref = rest[pos + 1] if nk > 1 else None

    a = a_ref[...]
    if has_gain:
        a = (_rms(a.astype(F32)) * g_ref[...]).astype(BF16)
    r = _dot(a, w_ref[...])

    def finish(res):
        if epilogue is not None:
            res = epilogue(res, *[e[...] for e in extra])
        o_ref[...] = res.astype(o_ref.dtype)

    if nk == 1:
        finish(r)
    else:
        k = pl.program_id(2)

        @pl.when(k == 0)
        def _():
            acc_ref[...] = r

        @pl.when(k > 0)
        def _():
            acc_ref[...] += r

        @pl.when(k == nk - 1)
        def _():
            finish(acc_ref[...])


def _matmul(a, w, *, rows=None, a_col=0, k=None, out_dtype=BF16, gain=None,
            epilogue=None, extras=(), tn=None, tm=None, out_widen=1):
    k = w.shape[0] if k is None else k
    n = w.shape[1]
    rows = a.shape[0] if rows is None else rows
    tm = tm or _pick(rows, (768, 1024, 512, 256, 128, 64, 32, 16, 8))
    tk = k if k <= MM_MAX_TK else _pick(k, (4096, 2048, 1024, 512))
    nk = k // tk
    tn = tn or _pick(n, [t for t in (1024, 512, 256, 128) if tk * t <= MM_W_TILE_ELEMS] or [128])
    assert a_col % tk == 0 and (gain is None or nk == 1)
    a_cb = a_col // tk
    in_specs = [pl.BlockSpec((tm, tk), lambda i, j, kk: (i, a_cb + kk)),
                pl.BlockSpec((tk, tn), lambda i, j, kk: (kk, j))]
    args = [a, w]
    if gain is not None:
        in_specs.append(pl.BlockSpec((1, tk), lambda i, j, kk: (0, 0)))
        args.append(gain.reshape(1, tk).astype(F32))
    for arr, cols, cb in extras:
        in_specs.append(pl.BlockSpec((tm, cols), lambda i, j, kk, cb=cb: (i, cb(j))))
        args.append(arr)
    scratch = [pltpu.VMEM((tm, tn), F32)] if nk > 1 else []
    return pl.pallas_call(
        functools.partial(_mm_kernel, nk=nk, has_gain=gain is not None,
                          epilogue=epilogue, n_extra=len(extras)),
        grid=(rows // tm, n // tn, nk),
        in_specs=in_specs,
        out_specs=pl.BlockSpec((tm, tn * out_widen), lambda i, j, kk: (i, j)),
        out_shape=jax.ShapeDtypeStruct((rows, n * out_widen), out_dtype),
        scratch_shapes=scratch,
        compiler_params=_cparams(("arbitrary", "arbitrary", "arbitrary")),
        name="matmul",
    )(*args)


def _rope_tables(n_lat, n_ctx):
    rows = n_lat // GRID_W
    row = jnp.repeat(jnp.arange(rows, dtype=F32), GRID_W)
    col = jnp.tile(jnp.arange(GRID_W, dtype=F32), rows)
    n_freq = MLA_ROPE // 4
    inv_freq = ROPE_BASE ** (-jnp.arange(n_freq, dtype=F32) / n_freq)
    ang = jnp.concatenate([row[:, None] * inv_freq, col[:, None] * inv_freq], axis=-1)
    ang = jnp.concatenate([ang, jnp.zeros((n_ctx, ang.shape[1]), F32)], axis=0)
    cos, sin = jnp.cos(ang), jnp.sin(ang)
    m = ang.shape[0]
    half = MLA_ROPE // 2
    ones = jnp.ones((m, HEAD), F32)
    z_h = jnp.zeros((m, HEAD), F32)
    z_p = jnp.zeros((m, MLA_QK_PAD - HEAD - MLA_ROPE), F32)
    z_r = jnp.zeros((m, half), F32)
    c = jnp.concatenate([ones, cos, cos, z_p], axis=1)
    s1 = jnp.concatenate([z_h, -sin, z_r, z_p], axis=1)
    s2 = jnp.concatenate([z_h, z_r, sin, z_p], axis=1)
    return c, s1, s2


def _rope_heads(t, c, s1, s2):
    n = t.shape[1]
    reps = n // MLA_QK_PAD
    half = MLA_ROPE // 2
    if reps > 1:
        c, s1, s2 = (jnp.concatenate([u] * reps, axis=1) for u in (c, s1, s2))
    return t * c + pltpu.roll(t, n - half, 1) * s1 + pltpu.roll(t, half, 1) * s2


def _k_epilogue(t, kr):
    krf = kr.astype(F32)
    parts = []
    for hh in range(t.shape[1] // HEAD):
        parts += [t[:, hh * HEAD:(hh + 1) * HEAD], krf]
    return jnp.concatenate(parts, axis=1)


def _ckv_epilogue(t, c, s1, s2):
    r = MLA_KV_RANK
    return jnp.concatenate([t[:, :r], _rope_heads(t[:, r:], c, s1, s2)], axis=1)


def _flash_kernel(q_ref, k_ref, v_ref, z_ref, o_ref, *, tk, nk):
    q = q_ref[...]
    tq = q.shape[0]
    ones_col = (lax.broadcasted_iota(jnp.int32, (tk, HEAD), 1) == 0).astype(BF16)
    m = jnp.full((tq, 1), -jnp.inf, F32)
    acc = jnp.zeros((tq, 2 * HEAD), F32)
    s = _dot_nt(q, k_ref[0:tk, :])
    for c in range(nk):
        s_next = _dot_nt(q, k_ref[(c + 1) * tk:(c + 2) * tk, :]) if c + 1 < nk else None
        m_new = jnp.maximum(m, jnp.max(s, axis=-1, keepdims=True))
        p = jnp.exp2(s - m_new)
        alpha = jnp.exp2(m - m_new)
        v_ext = jnp.concatenate([v_ref[c * tk:(c + 1) * tk, :], ones_col], axis=1)
        acc = alpha * acc + _dot(p.astype(BF16), v_ext)
        m, s = m_new, s_next
    o = acc[:, :HEAD] / acc[:, HEAD:HEAD + 1]
    o_ref[...] = (o * _silu(z_ref[...].astype(F32))).astype(o_ref.dtype)


def _flash(q, k, v, zsrc, z_col, og_prev, *, q_row0, n_q, k_row0, n_k, total_rows):
    heads = v.shape[1] // HEAD
    tq = _pick(n_q, (512, 256, 128))
    tk = _pick(n_k, (768, 512, 256, 128))
    assert q_row0 % tq == 0 and k_row0 % n_k == 0
    qb0, kb0, zcb = q_row0 // tq, k_row0 // n_k, z_col // HEAD
    in_specs = [pl.BlockSpec((tq, MLA_QK_PAD), lambda h, i: (qb0 + i, h)),
                pl.BlockSpec((n_k, MLA_QK_PAD), lambda h, i: (kb0, h)),
                pl.BlockSpec((n_k, HEAD), lambda h, i: (kb0, h)),
                pl.BlockSpec((tq, HEAD), lambda h, i: (qb0 + i, zcb + h))]
    args = [q, k, v, zsrc]
    aliases = {}
    if og_prev is not None:
        in_specs.append(pl.BlockSpec(memory_space=pl.ANY))
        args.append(og_prev)
        aliases = {4: 0}

    def kern(q_ref, k_ref, v_ref, z_ref, *rest):
        _flash_kernel(q_ref, k_ref, v_ref, z_ref, rest[-1], tk=tk, nk=n_k // tk)

    return pl.pallas_call(
        kern,
        grid=(heads, n_q // tq),
        in_specs=in_specs,
        out_specs=pl.BlockSpec((tq, HEAD), lambda h, i: (qb0 + i, h)),
        out_shape=jax.ShapeDtypeStruct((total_rows, heads * HEAD), BF16),
        input_output_aliases=aliases,
        compiler_params=_cparams(("arbitrary", "arbitrary")),
        name="mla_flash",
    )(*args)


def _mla_weights(w_in, w_q_up, w_kv_up):
    d = w_in.shape[0]
    q_rank = d // 4
    heads = d // HEAD
    r0, r1, r2 = q_rank, q_rank + MLA_KV_RANK, q_rank + MLA_KV_RANK + MLA_ROPE
    pad = MLA_QK_PAD - HEAD - MLA_ROPE
    scale = (HEAD + MLA_ROPE) ** -0.5 * math.log2(math.e)
    w_cqz = jnp.concatenate([w_in[:, :r0], w_in[:, r2:]], axis=1).astype(BF16)
    w_ckv = jnp.concatenate([w_in[:, r0:r1], jnp.zeros((d, HEAD), F32), w_in[:, r1:r2],
                             jnp.zeros((d, pad), F32)], axis=1).astype(BF16)
    wq = (w_q_up * scale).reshape(q_rank, heads, HEAD + MLA_ROPE)
    wq = jnp.concatenate([wq, jnp.zeros((q_rank, heads, pad), F32)], axis=2)
    wq = wq.reshape(q_rank, heads * MLA_QK_PAD).astype(BF16)
    wkv = w_kv_up.reshape(MLA_KV_RANK, heads, 2 * HEAD)
    wk = wkv[:, :, :HEAD].reshape(MLA_KV_RANK, heads * HEAD).astype(BF16)
    wv = wkv[:, :, HEAD:].reshape(MLA_KV_RANK, heads * HEAD).astype(BF16)
    return w_cqz, w_ckv, wq, wk, wv


def _mla_mixer(h, tables, w_in, g_q, w_q_up, g_kv, w_kv_up, w_out, n_lat, need_ctx):
    m, d = h.shape
    n_ctx = m - n_lat
    q_rank = d // 4
    c, s1, s2 = tables
    w_cqz, w_ckv, wq, wk, wv = _mla_weights(w_in, w_q_up, w_kv_up)
    first = lambda j: 0

    cqz = _matmul(h, w_cqz)
    ckvkr = _matmul(h, w_ckv, epilogue=_ckv_epilogue, tn=w_ckv.shape[1],
                    extras=[(c, MLA_QK_PAD, first), (s1, MLA_QK_PAD, first),
                            (s2, MLA_QK_PAD, first)])
    q = _matmul(cqz, wq, k=q_rank, gain=g_q, epilogue=_rope_heads,
                extras=[(c, MLA_QK_PAD, first), (s1, MLA_QK_PAD, first),
                        (s2, MLA_QK_PAD, first)])
    kr_cb = (MLA_KV_RANK + HEAD) // HEAD
    kk = _matmul(ckvkr, wk, k=MLA_KV_RANK, gain=g_kv, epilogue=_k_epilogue,
                 extras=[(ckvkr, HEAD, lambda j: kr_cb)], out_widen=MLA_QK_PAD // HEAD)
    v = _matmul(ckvkr, wv, k=MLA_KV_RANK, gain=g_kv)

    og = _flash(q, kk, v, cqz, q_rank, None, q_row0=0, n_q=n_lat, k_row0=0, n_k=m,
                total_rows=m)
    if need_ctx:
        og = _flash(q, kk, v, cqz, q_rank, og, q_row0=n_lat, n_q=n_ctx, k_row0=n_lat,
                    n_k=n_ctx, total_rows=m)
    rows = m if need_ctx else n_lat
    return _matmul(og, w_out.astype(BF16), rows=rows)


def _fill_window(scr, prev_ref, cur_ref, next_ref, n_lat_blocks):
    tm = cur_ref.shape[0]
    i = pl.program_id(0)
    first = jnp.logical_or(i == 0, i == n_lat_blocks)
    last = jnp.logical_or(i == n_lat_blocks - 1, i == pl.num_programs(0) - 1)
    scr[0:HALO, :] = jnp.where(first, 0.0, prev_ref[...].astype(F32))
    scr[HALO:HALO + tm, :] = cur_ref[...].astype(F32)
    scr[HALO + tm:, :] = jnp.where(last, 0.0, next_ref[...].astype(F32))


def _seg_position(tm, n_lat_blocks, n_lat, n_ctx):
    i = pl.program_id(0)
    is_ctx = i >= n_lat_blocks
    local0 = jnp.where(is_ctx, i - n_lat_blocks, i) * tm
    t = local0 + lax.broadcasted_iota(jnp.int32, (tm, 1), 0)
    return t, jnp.where(is_ctx, n_ctx, n_lat)


def _window_specs(tm, cols, n_rows, col_fn):
    hb = tm // HALO
    last = n_rows // HALO - 1
    return [pl.BlockSpec((HALO, cols), lambda i, j: (jnp.maximum(i * hb - 1, 0), col_fn(j))),
            pl.BlockSpec((tm, cols), lambda i, j: (i, col_fn(j))),
            pl.BlockSpec((HALO, cols), lambda i, j: (jnp.minimum((i + 1) * hb, last), col_fn(j)))]


def _gdn_conv_kernel(prev_ref, cur_ref, next_ref, w_ref, o_ref, scr, *, normalize,
                     n_scaled_blocks, n_lat_blocks):
    tm, tc = cur_ref.shape
    _fill_window(scr, prev_ref, cur_ref, next_ref, n_lat_blocks)
    r = GDN_CONV // 2
    acc = scr[HALO - r:HALO - r + tm, :] * w_ref[0:1, :]
    for j in range(1, GDN_CONV):
        acc = acc + scr[HALO + j - r:HALO + j - r + tm, :] * w_ref[j:j + 1, :]
    y = _silu(acc)
    if normalize:
        scale = jnp.where(pl.program_id(1) < n_scaled_blocks, HEAD ** -0.5, 1.0)
        parts = []
        for hh in range(tc // HEAD):
            u = y[:, hh * HEAD:(hh + 1) * HEAD]
            parts.append(u * (lax.rsqrt(jnp.sum(u * u, axis=-1, keepdims=True) + EPS) * scale))
        y = jnp.concatenate(parts, axis=1) if len(parts) > 1 else parts[0]
    o_ref[...] = y.astype(o_ref.dtype)


def _gdn_conv(src, conv_w, col0, ncols, normalize, n_scaled_cols, n_lat):
    m = src.shape[0]
    tm = SEG_TILE
    tc = _pick(ncols, (512, 256, 128))
    cb0 = col0 // tc
    specs = _window_specs(tm, tc, m, lambda j: cb0 + j)
    specs.append(pl.BlockSpec((8, tc), lambda i, j: (0, cb0 + j)))
    w8 = jnp.concatenate([conv_w, jnp.zeros((8 - GDN_CONV, conv_w.shape[1]), F32)], axis=0)
    return pl.pallas_call(
        functools.partial(_gdn_conv_kernel, normalize=normalize,
                          n_scaled_blocks=n_scaled_cols // tc, n_lat_blocks=n_lat // tm),
        grid=(m // tm, ncols // tc),
        in_specs=specs,
        out_specs=pl.BlockSpec((tm, tc), lambda i, j: (i, j)),
        out_shape=jax.ShapeDtypeStruct((m, ncols), BF16),
        scratch_shapes=[pltpu.VMEM((tm + 2 * HALO, tc), F32)],
        compiler_params=_cparams(("arbitrary", "arbitrary")),
        name="gdn_conv",
    )(src, src, src, w8)


def _split3(x):
    x1 = x.astype(BF16)
    r1 = x - x1.astype(F32)
    x2 = r1.astype(BF16)
    x3 = (r1 - x2.astype(F32)).astype(BF16)
    return x1, x2, x3


def _gdn_gate_kernel(ab_ref, alog_ref, dtb_ref, g_ref, beta_ref):
    tm, w = g_ref.shape
    ab = ab_ref[...]
    x = ab[:, :w] + dtb_ref[...]
    softplus = jnp.maximum(x, 0.0) + jnp.log1p(jnp.exp(-jnp.abs(x)))
    g = -jnp.exp(alog_ref[...]) * softplus
    beta_ref[...] = jax.nn.sigmoid(ab[:, w:])
    ch = GDN_CHUNK
    ri = lax.broadcasted_iota(jnp.int32, (ch, ch), 0)
    ci = lax.broadcasted_iota(jnp.int32, (ch, ch), 1)
    lower = (ci <= ri).astype(BF16)
    upper = (ci >= ri).astype(BF16)
    fwd_lane = lax.broadcasted_iota(jnp.int32, (ch, w), 1) < w // 2
    for cidx in range(tm // ch):
        parts = _split3(g[cidx * ch:(cidx + 1) * ch, :])
        pre = _dot(lower, parts[0]) + (_dot(lower, parts[1]) + _dot(lower, parts[2]))
        suf = _dot(upper, parts[0]) + (_dot(upper, parts[1]) + _dot(upper, parts[2]))
        g_ref[cidx * ch:(cidx + 1) * ch, :] = jnp.where(fwd_lane, pre, suf)


def _gdn_gates(ab, a_log, dt_bias):
    m, w4 = ab.shape
    w = w4 // 2
    tm = SEG_TILE
    return pl.pallas_call(
        _gdn_gate_kernel,
        grid=(m // tm,),
        in_specs=[pl.BlockSpec((tm, w4), lambda i: (i, 0)),
                  pl.BlockSpec((1, w), lambda i: (0, 0)),
                  pl.BlockSpec((1, w), lambda i: (0, 0))],
        out_specs=[pl.BlockSpec((tm, w), lambda i: (i, 0)),
                   pl.BlockSpec((tm, w), lambda i: (i, 0))],
        out_shape=[jax.ShapeDtypeStruct((m, w), F32), jax.ShapeDtypeStruct((m, w), F32)],
        compiler_params=_cparams(("arbitrary",)),
        name="gdn_gates",
    )(ab, a_log.reshape(1, w).astype(F32), dt_bias.reshape(1, w).astype(F32))


def _bmm(a, b):
    return lax.dot_general(a, b, (((2,), (1,)), ((0,), (0,))), preferred_element_type=F32)


def _unit_tri_inverse(a, ri, ci):
    def same(s):
        return (ri // s) == (ci // s)

    t = (ri == ci).astype(F32) - jnp.where(same(2), a, 0.0)
    s = 2
    while s < GDN_CHUNK:
        coupling = jnp.logical_and(same(2 * s), jnp.logical_not(same(s)))
        tb = t.astype(BF16)
        tl = _bmm(tb, jnp.where(coupling, a, 0.0).astype(BF16))
        t = t - _bmm(tl.astype(BF16), tb)
        s *= 2
    return t


def _gdn_kernel(q_ref, k_ref, v_ref, col_ref, row_ref, *rest, kg, fwd):
    o_ref, state = rest[-2:]
    ch = GDN_CHUNK
    p2 = 2 * ch

    @pl.when(pl.program_id(1) == 0)
    def _():
        state[...] = jnp.zeros_like(state)

    ri = lax.broadcasted_iota(jnp.int32, (1, p2, p2), 1)
    ci = lax.broadcasted_iota(jnp.int32, (1, p2, p2), 2)
    same_head = (ri // ch) == (ci // ch)
    ahead = ci - ri if fwd else ri - ci
    incl = jnp.logical_and(same_head, ahead <= 0)
    strict = jnp.logical_and(incl, ci != ri)
    first_lanes = lax.broadcasted_iota(jnp.int32, (1, 1, p2), 2) < ch

    def pair_rows(ref):
        x = jnp.stack([ref[:, p * HEAD:(p + 1) * HEAD] for p in range(kg)]).astype(F32)
        return jnp.concatenate([x, x], axis=1)

    def head_rows(ref):
        return jnp.stack([jnp.concatenate([ref[:, (2 * p) * HEAD:(2 * p + 1) * HEAD],
                                           ref[:, (2 * p + 1) * HEAD:(2 * p + 2) * HEAD]], axis=0)
                          for p in range(kg)]).astype(F32)

    k2 = pair_rows(k_ref)
    q2 = pair_rows(q_ref)
    vf = head_rows(v_ref)
    kt = jnp.stack([k2[p].T for p in range(kg)])
    kkqk = _bmm(jnp.concatenate([k2, q2], axis=1).astype(BF16), kt.astype(BF16))
    kk, qk = kkqk[:, :p2], kkqk[:, p2:]

    def col(j0):
        c = jnp.stack([col_ref[:, j0 + p:j0 + p + 1] for p in range(kg)])
        return jnp.broadcast_to(c, (kg, p2, p2))

    gcol, bcol = col(0), col(kg)
    grow = jnp.stack([row_ref[p:p + 1, :] for p in range(kg)])
    last = ch - 1 if fwd else 0
    tot_a = grow[:, :, last:last + 1]
    tot_b = grow[:, :, ch + last:ch + last + 1]
    tot_row = jnp.where(first_lanes, tot_a, tot_b)

    decay = jnp.exp(jnp.where(incl, gcol - grow, -jnp.inf))
    tinv = _unit_tri_inverse(jnp.where(strict, bcol * kk * decay, 0.0), ri, ci)

    eg = jnp.exp(gcol)
    rhs = jnp.concatenate([vf * bcol, k2 * (bcol * eg)], axis=2).astype(BF16)
    uw = _bmm(tinv.astype(BF16), rhs)
    u, w_b = uw[:, :, :HEAD], uw[:, :, HEAD:].astype(BF16)
    qk_d = jnp.where(incl, qk * decay, 0.0).astype(BF16)
    q_dec = (q2 * eg).astype(BF16)
    kt_dec = kt * jnp.exp(tot_row - grow)

    s_a = state[:, 0]
    s_b = state[:, 1]
    ws_a = _bmm(jnp.concatenate([w_b[:, :ch], q_dec[:, :ch]], axis=1), s_a.astype(BF16))
    ws_b = _bmm(jnp.concatenate([w_b[:, ch:], q_dec[:, ch:]], axis=1), s_b.astype(BF16))
    v_new = (u - jnp.concatenate([ws_a[:, :ch], ws_b[:, :ch]], axis=1)).astype(BF16)
    o = jnp.concatenate([ws_a[:, ch:], ws_b[:, ch:]], axis=1) + _bmm(qk_d, v_new)
    kd_a = jnp.where(first_lanes, kt_dec, 0.0).astype(BF16)
    kd_b = jnp.where(first_lanes, 0.0, kt_dec).astype(BF16)
    state[:, 0] = s_a * jnp.exp(tot_a) + _bmm(kd_a, v_new)
    state[:, 1] = s_b * jnp.exp(tot_b) + _bmm(kd_b, v_new)
    if len(rest) > 2:
        oprev_ref, z_ref, gn_ref = rest[:3]
        o = _rms(o + head_rows(oprev_ref)) * gn_ref[...] * _silu(head_rows(z_ref))
    for p in range(kg):
        o_ref[:, (2 * p) * HEAD:(2 * p + 1) * HEAD] = o[p, :ch].astype(o_ref.dtype)
        o_ref[:, (2 * p + 1) * HEAD:(2 * p + 2) * HEAD] = o[p, ch:].astype(o_ref.dtype)


def _gdn_gate_layouts(gcs, beta, kg):
    m, w = gcs.shape
    ch = GDN_CHUNK
    assert HEAD == 2 * ch
    nch, ngrp = m // ch, w // (4 * kg)
    g6 = gcs.reshape(nch, ch, 2, ngrp, kg, 2)
    b6 = beta.reshape(nch, ch, 2, ngrp, kg, 2)
    to_cols = lambda t: t.transpose(2, 3, 0, 5, 1, 4).reshape(2, ngrp, nch, 2 * ch, kg)
    cols = jnp.concatenate([to_cols(g6), to_cols(b6)], axis=4)
    rows = g6.transpose(2, 3, 0, 4, 5, 1).reshape(2, ngrp, nch, kg, 2 * ch)
    return cols, rows


def _gdn_scan(qk, v, cols, rows, n_lat, kg, fwd, fused=None):
    m = v.shape[0]
    vh = v.shape[1] // HEAD
    ch = GDN_CHUNK
    nch = m // ch
    lat_ch = n_lat // ch
    ngrp = vh // (2 * kg)
    nv = 2 * kg
    d = 0 if fwd else 1

    def chunk(n):
        return (n + lat_ch) % nch if fwd else nch - 1 - n

    head_spec = pl.BlockSpec((ch, nv * HEAD), lambda g, n: (chunk(n), g))
    in_specs = [pl.BlockSpec((ch, kg * HEAD), lambda g, n: (chunk(n), g)),
                pl.BlockSpec((ch, kg * HEAD), lambda g, n: (chunk(n), ngrp + g)),
                head_spec,
                pl.BlockSpec((None, None, None, 2 * ch, 2 * kg), lambda g, n: (d, g, chunk(n), 0, 0)),
                pl.BlockSpec((None, None, None, kg, 2 * ch), lambda g, n: (d, g, chunk(n), 0, 0))]
    args = [qk, qk, v, cols, rows]
    if fused is not None:
        o_other, zsrc, z_col, g_norm = fused
        zcb = z_col // (nv * HEAD)
        in_specs += [head_spec,
                     pl.BlockSpec((ch, nv * HEAD), lambda g, n: (chunk(n), zcb + g)),
                     pl.BlockSpec((1, HEAD), lambda g, n: (0, 0))]
        args += [o_other, zsrc, g_norm.reshape(1, HEAD).astype(F32)]
    return pl.pallas_call(
        functools.partial(_gdn_kernel, kg=kg, fwd=fwd),
        grid=(ngrp, nch),
        in_specs=in_specs,
        out_specs=head_spec,
        out_shape=jax.ShapeDtypeStruct((m, vh * HEAD), BF16),
        scratch_shapes=[pltpu.VMEM((kg, 2, HEAD, HEAD), F32)],
        compiler_params=_cparams(("arbitrary", "arbitrary")),
        name="gdn_scan",
    )(*args)


def _gdn_mixer(h, w_in, conv_w, a_log, dt_bias, g_norm, w_out, n_lat):
    m, d = h.shape
    khs = d // HEAD
    vhs = 2 * khs
    kw, vw = khs * HEAD, vhs * HEAD
    qkv_w = 2 * kw + vw
    w_big = jnp.concatenate([w_in[:, :qkv_w], w_in[:, qkv_w + 4 * vhs:]], axis=1).astype(BF16)
    w_ab = w_in[:, qkv_w:qkv_w + 4 * vhs].astype(BF16)
    qkvz = _matmul(h, w_big)
    ab = _matmul(h, w_ab, out_dtype=F32, tn=4 * vhs)
    qk = _gdn_conv(qkvz, conv_w, 0, 2 * kw, True, kw, n_lat)
    v = _gdn_conv(qkvz, conv_w, 2 * kw, vw, False, 0, n_lat)
    gcs, beta = _gdn_gates(ab, a_log, dt_bias)
    kg = _pick(khs, (GDN_PAIRS, 4, 2, 1))
    cols, rows = _gdn_gate_layouts(gcs, beta, kg)
    o_fwd = _gdn_scan(qk, v, cols, rows, n_lat, kg, True)
    y = _gdn_scan(qk, v, cols, rows, n_lat, kg, False, fused=(o_fwd, qkvz, qkv_w, g_norm))
    return _matmul(y, w_out.astype(BF16))


def _pool_kernel(prev_ref, cur_ref, next_ref, w_ref, sc_ref, z_ref, o_ref, scr, *,
                 n_lat_blocks, n_lat, n_ctx):
    tm, tc = cur_ref.shape
    _fill_window(scr, prev_ref, cur_ref, next_ref, n_lat_blocks)
    t, seg_len = _seg_position(tm, n_lat_blocks, n_lat, n_ctx)
    for g, window in enumerate(POOL_WINDOWS):
        @pl.when(pl.program_id(1) == g)
        def _(radius=window // 2):
            u = scr[HALO:HALO + tm, :]
            acc = u
            for dlt in range(1, radius + 1):
                acc = acc + (scr[HALO - dlt:HALO - dlt + tm, :] + scr[HALO + dlt:HALO + dlt + tm, :])
            cnt = jnp.minimum(t + radius + 1, seg_len) - jnp.maximum(t - radius, 0)
            mean_minus = acc / cnt.astype(F32) - u
            y = _dot(mean_minus.astype(BF16), w_ref[...]) * sc_ref[...]
            o_ref[...] = (y * _silu(z_ref[...].astype(F32))).astype(o_ref.dtype)


def _pool_mixer(h, w_in, w_grp, scale, w_out, n_lat):
    m, d = h.shape
    ng, gw, _ = w_grp.shape
    assert tuple(w // 2 for w in POOL_WINDOWS) == tuple(2 ** g for g in range(ng))
    uz = _matmul(h, w_in.astype(BF16))
    tm = SEG_TILE
    specs = _window_specs(tm, gw, m, lambda g: g)
    specs += [pl.BlockSpec((None, gw, gw), lambda i, g: (g, 0, 0)),
              pl.BlockSpec((1, gw), lambda i, g: (0, g)),
              pl.BlockSpec((tm, gw), lambda i, g: (i, ng + g))]
    yg = pl.pallas_call(
        functools.partial(_pool_kernel, n_lat_blocks=n_lat // tm, n_lat=n_lat, n_ctx=m - n_lat),
        grid=(m // tm, ng),
        in_specs=specs,
        out_specs=pl.BlockSpec((tm, gw), lambda i, g: (i, g)),
        out_shape=jax.ShapeDtypeStruct((m, d), BF16),
        scratch_shapes=[pltpu.VMEM((tm + 2 * HALO, gw), F32)],
        compiler_params=_cparams(("arbitrary", "arbitrary")),
        name="pool_group",
    )(uz, uz, uz, w_grp.astype(BF16), scale.reshape(1, d).astype(F32), uz)
    return _matmul(yg, w_out.astype(BF16))


def kernel(x, c, ctx, c_ctx, ada_w, ada_b, norm_pre, norm_post, mla_w_in, mla_g_q, mla_w_q_up,
           mla_g_kv, mla_w_kv_up, mla_w_out, gdn_w_in, gdn_conv_w, gdn_a_log, gdn_dt_bias,
           gdn_g_norm, gdn_w_out, pool_w_in, pool_w_grp, pool_scale, pool_w_out):
    assert x.shape[0] == 1 and ctx.shape[0] == 1
    n_lat, d = x.shape[1], x.shape[2]
    n_ctx = ctx.shape[1]
    depth = ada_w.shape[0]
    n_mixers = 3
    assert n_lat % SEG_TILE == 0 and n_ctx % SEG_TILE == 0

    cond8 = jnp.concatenate([c.reshape(1, d), c_ctx.reshape(1, d), jnp.zeros((6, d), F32)], axis=0)
    mods = _adaln(cond8, ada_w, ada_b)
    tables = _rope_tables(n_lat, n_ctx)
    xs = jnp.concatenate([x[0], ctx[0]], axis=0)

    counts = [0, 0, 0]
    h = _prenorm(xs, norm_pre[0], mods[0], n_lat)
    for i in range(depth):
        kind = i % n_mixers
        j = counts[kind]
        counts[kind] += 1
        need_ctx = i < depth - 1
        if kind == 0:
            y = _mla_mixer(h, tables, mla_w_in[j], mla_g_q[j], mla_w_q_up[j], mla_g_kv[j],
                           mla_w_kv_up[j], mla_w_out[j], n_lat, need_ctx)
        elif kind == 1:
            y = _gdn_mixer(h, gdn_w_in[j], gdn_conv_w[j], gdn_a_log[j], gdn_dt_bias[j],
                           gdn_g_norm[j], gdn_w_out[j], n_lat)
        else:
            y = _pool_mixer(h, pool_w_in[j], pool_w_grp[j], pool_scale[j], pool_w_out[j], n_lat)
        if need_ctx:
            xs, h = _residual(xs, y, norm_post[i], mods[i], n_lat, xs.shape[0],
                              norm_pre[i + 1], mods[i + 1])
        else:
            xs = _residual(xs, y, norm_post[i], mods[i], n_lat, n_lat)
    return xs[None]
```

```python
import functools
import math

import jax
import jax.numpy as jnp
from jax import lax
from jax.experimental import pallas as pl
from jax.experimental.pallas import tpu as pltpu

F32 = jnp.float32
BF16 = jnp.bfloat16
EPS = 1e-6

HEAD = 128
MLA_ROPE = 64
MLA_KV_RANK = 512
MLA_QK_PAD = 256
ROPE_BASE = 10000.0
GRID_W = 64
GDN_CONV = 5
GDN_CHUNK = 64
GDN_PAIRS = 16
POOL_WINDOWS = (2, 4, 8, 16)
SEG_TILE = 256
HALO = 16
VMEM_LIMIT = 56 * 1024 * 1024
MM_MAX_TK = 8192
MM_W_TILE_ELEMS = 2 * 1024 * 1024
MM_A_TILE_ELEMS = 6 * 1024 * 1024


def _pick(n, prefs):
    for p in prefs:
        if n % p == 0:
            return p
    return n


def _cparams(sem):
    return pltpu.CompilerParams(dimension_semantics=sem, vmem_limit_bytes=VMEM_LIMIT)


def _dot(a, b):
    return jnp.dot(a, b, preferred_element_type=F32)


def _dot_nt(a, b):
    return lax.dot_general(a, b, (((1,), (1,)), ((), ())), preferred_element_type=F32)


def _silu(x):
    return x * jax.nn.sigmoid(x)


def _adaln_kernel(c_ref, w_ref, b_ref, o_ref):
    a = _silu(c_ref[...]).astype(BF16)
    o_ref[...] = _dot(a, w_ref[...].astype(BF16)) + b_ref[...]


def _adaln(cond8, ada_w, ada_b):
    depth, d, n = ada_w.shape
    tn = _pick(n, (512, 256, 128))
    return pl.pallas_call(
        _adaln_kernel,
        grid=(depth, n // tn),
        in_specs=[pl.BlockSpec((8, d), lambda l, j: (0, 0)),
                  pl.BlockSpec((None, d, tn), lambda l, j: (l, 0, j)),
                  pl.BlockSpec((None, 1, tn), lambda l, j: (l, 0, j))],
        out_specs=pl.BlockSpec((None, 8, tn), lambda l, j: (l, 0, j)),
        out_shape=jax.ShapeDtypeStruct((depth, 8, n), F32),
        compiler_params=_cparams(("arbitrary", "arbitrary")),
        name="adaln",
    )(cond8, ada_w, ada_b.reshape(depth, 1, n))


def _rms(xf):
    return xf * lax.rsqrt(jnp.mean(xf * xf, axis=-1, keepdims=True) + EPS)


def _prenorm_kernel(x_ref, g_ref, mod_ref, o_ref, *, n_lat_blocks, d):
    is_ctx = pl.program_id(0) >= n_lat_blocks
    mod = jnp.where(is_ctx, mod_ref[1:2, :], mod_ref[0:1, :])
    sh, sc = mod[:, :d], mod[:, d:2 * d]
    y = _rms(x_ref[...]) * g_ref[...]
    o_ref[...] = (y * (1.0 + sc) + sh).astype(o_ref.dtype)


def _prenorm(x, g, mod, n_lat):
    m, d = x.shape
    tm = SEG_TILE
    return pl.pallas_call(
        functools.partial(_prenorm_kernel, n_lat_blocks=n_lat // tm, d=d),
        grid=(m // tm,),
        in_specs=[pl.BlockSpec((tm, d), lambda i: (i, 0)),
                  pl.BlockSpec((1, d), lambda i: (0, 0)),
                  pl.BlockSpec((8, 3 * d), lambda i: (0, 0))],
        out_specs=pl.BlockSpec((tm, d), lambda i: (i, 0)),
        out_shape=jax.ShapeDtypeStruct((m, d), BF16),
        compiler_params=_cparams(("arbitrary",)),
        name="prenorm",
    )(x, g.reshape(1, d), mod)


def _residual_kernel(x_ref, y_ref, g_ref, mod_ref, *rest, n_lat_blocks, d, with_next):
    is_ctx = pl.program_id(0) >= n_lat_blocks
    mod = jnp.where(is_ctx, mod_ref[1:2, :], mod_ref[0:1, :])
    gt = mod[:, 2 * d:]
    y = _rms(y_ref[...].astype(F32)) * g_ref[...]
    x_new = x_ref[...] + gt * y
    if not with_next:
        rest[0][...] = x_new
        return
    gn_ref, modn_ref, o_ref, h_ref = rest
    o_ref[...] = x_new
    modn = jnp.where(is_ctx, modn_ref[1:2, :], modn_ref[0:1, :])
    h = _rms(x_new) * gn_ref[...]
    h_ref[...] = (h * (1.0 + modn[:, d:2 * d]) + modn[:, :d]).astype(h_ref.dtype)


def _residual(x, y, g, mod, n_lat, rows, g_next=None, mod_next=None):
    d = x.shape[1]
    tm = SEG_TILE
    with_next = g_next is not None
    row_spec = pl.BlockSpec((tm, d), lambda i: (i, 0))
    vec_spec = pl.BlockSpec((1, d), lambda i: (0, 0))
    mod_spec = pl.BlockSpec((8, 3 * d), lambda i: (0, 0))
    in_specs = [row_spec, row_spec, vec_spec, mod_spec]
    args = [x, y, g.reshape(1, d), mod]
    out_specs, out_shape = row_spec, jax.ShapeDtypeStruct((rows, d), F32)
    if with_next:
        in_specs += [vec_spec, mod_spec]
        args += [g_next.reshape(1, d), mod_next]
        out_specs = [row_spec, row_spec]
        out_shape = [out_shape, jax.ShapeDtypeStruct((rows, d), BF16)]
    return pl.pallas_call(
        functools.partial(_residual_kernel, n_lat_blocks=n_lat // tm, d=d, with_next=with_next),
        grid=(rows // tm,),
        in_specs=in_specs,
        out_specs=out_specs,
        out_shape=out_shape,
        compiler_params=_cparams(("arbitrary",)),
        name="residual",
    )(*args)


def _mm_kernel(a_ref, w_ref, *rest, nk, has_gain, epilogue, n_extra):
    pos = 0
    g_ref = rest[pos] if has_gain else None
    pos += int(has_gain)
    extra = rest[pos:pos + n_extra]
    pos += n_extra
    o_ref = rest[pos]
    acc_ref = rest[pos + 1] if nk > 1 else None

    if has_gain:
        an_ref = rest[pos + 1]

        @pl.when(pl.program_id(1) == 0)
        def _():
            an_ref[...] = (_rms(a_ref[...].astype(F32)) * g_ref[...]).astype(BF16)

        a = an_ref[...]
    else:
        a = a_ref[...]
    r = _dot(a, w_ref[...])

    def finish(res):
        if epilogue is not None:
            res = epilogue(res, *[e[...] for e in extra])
        o_ref[...] = res.astype(o_ref.dtype)

    if nk == 1:
        finish(r)
    else:
        k = pl.program_id(2)

        @pl.when(k == 0)
        def _():
            acc_ref[...] = r

        @pl.when(k > 0)
        def _():
            acc_ref[...] += r

        @pl.when(k == nk - 1)
        def _():
            finish(acc_ref[...])


def _matmul(a, w, *, rows=None, a_col=0, k=None, out_dtype=BF16, gain=None,
            epilogue=None, extras=(), tn=None, tm=None, out_widen=1):
    k = w.shape[0] if k is None else k
    n = w.shape[1]
    rows = a.shape[0] if rows is None else rows
    tk = k if k <= MM_MAX_TK else _pick(k, (4096, 2048, 1024, 512))
    nk = k // tk
    tm = tm or _pick(rows, [t for t in (1408, 1024, 768, 512, 256, 128, 64, 32, 16)
                            if t * tk <= MM_A_TILE_ELEMS] or [8])
    tn = tn or _pick(n, [t for t in (1024, 512, 256, 128) if tk * t <= MM_W_TILE_ELEMS] or [128])
    assert a_col % tk == 0 and (gain is None or nk == 1)
    a_cb = a_col // tk
    in_specs = [pl.BlockSpec((tm, tk), lambda i, j, kk: (i, a_cb + kk)),
                pl.BlockSpec((tk, tn), lambda i, j, kk: (kk, j))]
    args = [a, w]
    if gain is not None:
        in_specs.append(pl.BlockSpec((1, tk), lambda i, j, kk: (0, 0)))
        args.append(gain.reshape(1, tk).astype(F32))
    for arr, cols, cb in extras:
        in_specs.append(pl.BlockSpec((tm, cols), lambda i, j, kk, cb=cb: (i, cb(j))))
        args.append(arr)
    scratch = [pltpu.VMEM((tm, tn), F32)] if nk > 1 else []
    if gain is not None:
        scratch = [pltpu.VMEM((tm, tk), BF16)]
    return pl.pallas_call(
        functools.partial(_mm_kernel, nk=nk, has_gain=gain is not None,
                          epilogue=epilogue, n_extra=len(extras)),
        grid=(rows // tm, n // tn, nk),
        in_specs=in_specs,
        out_specs=pl.BlockSpec((tm, tn * out_widen), lambda i, j, kk: (i, j)),
        out_shape=jax.ShapeDtypeStruct((rows, n * out_widen), out_dtype),
        scratch_shapes=scratch,
        compiler_params=_cparams(("arbitrary", "arbitrary", "arbitrary")),
        name="matmul",
    )(*args)


def _rope_tables(n_lat, n_ctx):
    rows = n_lat // GRID_W
    row = jnp.repeat(jnp.arange(rows, dtype=F32), GRID_W)
    col = jnp.tile(jnp.arange(GRID_W, dtype=F32), rows)
    n_freq = MLA_ROPE // 4
    inv_freq = ROPE_BASE ** (-jnp.arange(n_freq, dtype=F32) / n_freq)
    ang = jnp.concatenate([row[:, None] * inv_freq, col[:, None] * inv_freq], axis=-1)
    ang = jnp.concatenate([ang, jnp.zeros((n_ctx, ang.shape[1]), F32)], axis=0)
    cos, sin = jnp.cos(ang), jnp.sin(ang)
    m = ang.shape[0]
    half = MLA_ROPE // 2
    ones = jnp.ones((m, HEAD), F32)
    z_h = jnp.zeros((m, HEAD), F32)
    z_p = jnp.zeros((m, MLA_QK_PAD - HEAD - MLA_ROPE), F32)
    z_r = jnp.zeros((m, half), F32)
    c = jnp.concatenate([ones, cos, cos, z_p], axis=1)
    s1 = jnp.concatenate([z_h, -sin, z_r, z_p], axis=1)
    s2 = jnp.concatenate([z_h, z_r, sin, z_p], axis=1)
    return c, s1, s2


def _rope_heads(t, c, s1, s2):
    n = t.shape[1]
    reps = n // MLA_QK_PAD
    half = MLA_ROPE // 2
    if reps > 1:
        c, s1, s2 = (jnp.concatenate([u] * reps, axis=1) for u in (c, s1, s2))
    return t * c + pltpu.roll(t, n - half, 1) * s1 + pltpu.roll(t, half, 1) * s2


def _k_epilogue(t, kr):
    krf = kr.astype(F32)
    parts = []
    for hh in range(t.shape[1] // HEAD):
        parts += [t[:, hh * HEAD:(hh + 1) * HEAD], krf]
    return jnp.concatenate(parts, axis=1)


def _ckv_epilogue(t, c, s1, s2):
    r = MLA_KV_RANK
    return jnp.concatenate([t[:, :r], _rope_heads(t[:, r:], c, s1, s2)], axis=1)


def _flash_kernel(q_ref, k_ref, v_ref, z_ref, o_ref, *, tk, nk):
    q = q_ref[...]
    tq = q.shape[0]
    ones_col = (lax.broadcasted_iota(jnp.int32, (tk, HEAD), 1) == 0).astype(BF16)
    m = jnp.full((tq, 1), -jnp.inf, F32)
    acc = jnp.zeros((tq, 2 * HEAD), F32)
    s = _dot_nt(q, k_ref[0:tk, :])
    for c in range(nk):
        s_next = _dot_nt(q, k_ref[(c + 1) * tk:(c + 2) * tk, :]) if c + 1 < nk else None
        m_new = jnp.maximum(m, jnp.max(s, axis=-1, keepdims=True))
        p = jnp.exp2(s - m_new)
        alpha = jnp.exp2(m - m_new)
        v_ext = jnp.concatenate([v_ref[c * tk:(c + 1) * tk, :], ones_col], axis=1)
        acc = alpha * acc + _dot(p.astype(BF16), v_ext)
        m, s = m_new, s_next
    o = acc[:, :HEAD] / acc[:, HEAD:HEAD + 1]
    o_ref[...] = (o * _silu(z_ref[...].astype(F32))).astype(o_ref.dtype)


def _flash(q, k, v, zsrc, z_col, og_prev, *, q_row0, n_q, k_row0, n_k, total_rows):
    heads = v.shape[1] // HEAD
    tq = _pick(n_q, (512, 256, 128))
    tk = _pick(n_k, (768, 512, 256, 128))
    assert q_row0 % tq == 0 and k_row0 % n_k == 0
    qb0, kb0, zcb = q_row0 // tq, k_row0 // n_k, z_col // HEAD
    in_specs = [pl.BlockSpec((tq, MLA_QK_PAD), lambda h, i: (qb0 + i, h)),
                pl.BlockSpec((n_k, MLA_QK_PAD), lambda h, i: (kb0, h)),
                pl.BlockSpec((n_k, HEAD), lambda h, i: (kb0, h)),
                pl.BlockSpec((tq, HEAD), lambda h, i: (qb0 + i, zcb + h))]
    args = [q, k, v, zsrc]
    aliases = {}
    if og_prev is not None:
        in_specs.append(pl.BlockSpec(memory_space=pl.ANY))
        args.append(og_prev)
        aliases = {4: 0}

    def kern(q_ref, k_ref, v_ref, z_ref, *rest):
        _flash_kernel(q_ref, k_ref, v_ref, z_ref, rest[-1], tk=tk, nk=n_k // tk)

    return pl.pallas_call(
        kern,
        grid=(heads, n_q // tq),
        in_specs=in_specs,
        out_specs=pl.BlockSpec((tq, HEAD), lambda h, i: (qb0 + i, h)),
        out_shape=jax.ShapeDtypeStruct((total_rows, heads * HEAD), BF16),
        input_output_aliases=aliases,
        compiler_params=_cparams(("arbitrary", "arbitrary")),
        name="mla_flash",
    )(*args)


def _mla_weights(w_in, w_q_up, w_kv_up):
    d = w_in.shape[0]
    q_rank = d // 4
    heads = d // HEAD
    r0, r1, r2 = q_rank, q_rank + MLA_KV_RANK, q_rank + MLA_KV_RANK + MLA_ROPE
    pad = MLA_QK_PAD - HEAD - MLA_ROPE
    scale = (HEAD + MLA_ROPE) ** -0.5 * math.log2(math.e)
    w_cq, w_z = w_in[:, :r0].astype(BF16), w_in[:, r2:].astype(BF16)
    w_ckv = jnp.concatenate([w_in[:, r0:r1], jnp.zeros((d, HEAD), F32), w_in[:, r1:r2],
                             jnp.zeros((d, pad), F32)], axis=1).astype(BF16)
    wq = (w_q_up * scale).reshape(q_rank, heads, HEAD + MLA_ROPE)
    wq = jnp.concatenate([wq, jnp.zeros((q_rank, heads, pad), F32)], axis=2)
    wq = wq.reshape(q_rank, heads * MLA_QK_PAD).astype(BF16)
    wkv = w_kv_up.reshape(MLA_KV_RANK, heads, 2 * HEAD)
    wk = wkv[:, :, :HEAD].reshape(MLA_KV_RANK, heads * HEAD).astype(BF16)
    wv = wkv[:, :, HEAD:].reshape(MLA_KV_RANK, heads * HEAD).astype(BF16)
    return w_cq, w_z, w_ckv, wq, wk, wv


def _mla_mixer(h, tables, w_in, g_q, w_q_up, g_kv, w_kv_up, w_out, n_lat, need_ctx):
    m, d = h.shape
    n_ctx = m - n_lat
    q_rank = d // 4
    c, s1, s2 = tables
    w_cq, w_z, w_ckv, wq, wk, wv = _mla_weights(w_in, w_q_up, w_kv_up)
    first = lambda j: 0

    cq = _matmul(h, w_cq)
    z = _matmul(h, w_z)
    ckvkr = _matmul(h, w_ckv, epilogue=_ckv_epilogue, tn=w_ckv.shape[1],
                    extras=[(c, MLA_QK_PAD, first), (s1, MLA_QK_PAD, first),
                            (s2, MLA_QK_PAD, first)])
    q = _matmul(cq, wq, gain=g_q, epilogue=_rope_heads,
                extras=[(c, MLA_QK_PAD, first), (s1, MLA_QK_PAD, first),
                        (s2, MLA_QK_PAD, first)])
    kr_cb = (MLA_KV_RANK + HEAD) // HEAD
    kk = _matmul(ckvkr, wk, k=MLA_KV_RANK, gain=g_kv, epilogue=_k_epilogue,
                 extras=[(ckvkr, HEAD, lambda j: kr_cb)], out_widen=MLA_QK_PAD // HEAD)
    v = _matmul(ckvkr, wv, k=MLA_KV_RANK, gain=g_kv)

    og = _flash(q, kk, v, z, 0, None, q_row0=0, n_q=n_lat, k_row0=0, n_k=m, total_rows=m)
    if need_ctx:
        og = _flash(q, kk, v, z, 0, og, q_row0=n_lat, n_q=n_ctx, k_row0=n_lat, n_k=n_ctx,
                    total_rows=m)
    rows = m if need_ctx else n_lat
    return _matmul(og, w_out.astype(BF16), rows=rows)


def _fill_window(scr, prev_ref, cur_ref, next_ref, n_lat_blocks):
    tm = cur_ref.shape[0]
    i = pl.program_id(0)
    first = jnp.logical_or(i == 0, i == n_lat_blocks)
    last = jnp.logical_or(i == n_lat_blocks - 1, i == pl.num_programs(0) - 1)
    scr[0:HALO, :] = jnp.where(first, 0.0, prev_ref[...].astype(F32))
    scr[HALO:HALO + tm, :] = cur_ref[...].astype(F32)
    scr[HALO + tm:, :] = jnp.where(last, 0.0, next_ref[...].astype(F32))


def _seg_position(tm, n_lat_blocks, n_lat, n_ctx):
    i = pl.program_id(0)
    is_ctx = i >= n_lat_blocks
    local0 = jnp.where(is_ctx, i - n_lat_blocks, i) * tm
    t = local0 + lax.broadcasted_iota(jnp.int32, (tm, 1), 0)
    return t, jnp.where(is_ctx, n_ctx, n_lat)


def _window_specs(tm, cols, n_rows, col_fn):
    hb = tm // HALO
    last = n_rows // HALO - 1
    return [pl.BlockSpec((HALO, cols), lambda i, j: (jnp.maximum(i * hb - 1, 0), col_fn(j))),
            pl.BlockSpec((tm, cols), lambda i, j: (i, col_fn(j))),
            pl.BlockSpec((HALO, cols), lambda i, j: (jnp.minimum((i + 1) * hb, last), col_fn(j)))]


def _gdn_conv_kernel(prev_ref, cur_ref, next_ref, w_ref, o_ref, scr, *, normalize,
                     n_scaled_blocks, n_lat_blocks):
    tm, tc = cur_ref.shape
    _fill_window(scr, prev_ref, cur_ref, next_ref, n_lat_blocks)
    r = GDN_CONV // 2
    acc = scr[HALO - r:HALO - r + tm, :] * w_ref[0:1, :]
    for j in range(1, GDN_CONV):
        acc = acc + scr[HALO + j - r:HALO + j - r + tm, :] * w_ref[j:j + 1, :]
    y = _silu(acc)
    if normalize:
        scale = jnp.where(pl.program_id(1) < n_scaled_blocks, HEAD ** -0.5, 1.0)
        parts = []
        for hh in range(tc // HEAD):
            u = y[:, hh * HEAD:(hh + 1) * HEAD]
            parts.append(u * (lax.rsqrt(jnp.sum(u * u, axis=-1, keepdims=True) + EPS) * scale))
        y = jnp.concatenate(parts, axis=1) if len(parts) > 1 else parts[0]
    o_ref[...] = y.astype(o_ref.dtype)


def _gdn_conv(src, conv_w, col0, ncols, normalize, n_scaled_cols, n_lat):
    m = src.shape[0]
    tm = SEG_TILE
    tc = _pick(ncols, (512, 256, 128))
    cb0 = col0 // tc
    specs = _window_specs(tm, tc, m, lambda j: cb0 + j)
    specs.append(pl.BlockSpec((8, tc), lambda i, j: (0, cb0 + j)))
    w8 = jnp.concatenate([conv_w, jnp.zeros((8 - GDN_CONV, conv_w.shape[1]), F32)], axis=0)
    return pl.pallas_call(
        functools.partial(_gdn_conv_kernel, normalize=normalize,
                          n_scaled_blocks=n_scaled_cols // tc, n_lat_blocks=n_lat // tm),
        grid=(m // tm, ncols // tc),
        in_specs=specs,
        out_specs=pl.BlockSpec((tm, tc), lambda i, j: (i, j)),
        out_shape=jax.ShapeDtypeStruct((m, ncols), BF16),
        scratch_shapes=[pltpu.VMEM((tm + 2 * HALO, tc), F32)],
        compiler_params=_cparams(("arbitrary", "arbitrary")),
        name="gdn_conv",
    )(src, src, src, w8)


def _split3(x):
    x1 = x.astype(BF16)
    r1 = x - x1.astype(F32)
    x2 = r1.astype(BF16)
    x3 = (r1 - x2.astype(F32)).astype(BF16)
    return x1, x2, x3


def _gdn_gate_kernel(ab_ref, alog_ref, dtb_ref, g_ref, beta_ref):
    tm, w = g_ref.shape
    ab = ab_ref[...]
    x = ab[:, :w] + dtb_ref[...]
    softplus = jnp.maximum(x, 0.0) + jnp.log1p(jnp.exp(-jnp.abs(x)))
    g = -jnp.exp(alog_ref[...]) * softplus
    beta_ref[...] = jax.nn.sigmoid(ab[:, w:])
    ch = GDN_CHUNK
    ri = lax.broadcasted_iota(jnp.int32, (ch, ch), 0)
    ci = lax.broadcasted_iota(jnp.int32, (ch, ch), 1)
    lower = (ci <= ri).astype(BF16)
    upper = (ci >= ri).astype(BF16)
    fwd_lane = lax.broadcasted_iota(jnp.int32, (ch, w), 1) < w // 2
    for cidx in range(tm // ch):
        parts = _split3(g[cidx * ch:(cidx + 1) * ch, :])
        pre = _dot(lower, parts[0]) + (_dot(lower, parts[1]) + _dot(lower, parts[2]))
        suf = _dot(upper, parts[0]) + (_dot(upper, parts[1]) + _dot(upper, parts[2]))
        g_ref[cidx * ch:(cidx + 1) * ch, :] = jnp.where(fwd_lane, pre, suf)


def _gdn_gates(ab, a_log, dt_bias):
    m, w4 = ab.shape
    w = w4 // 2
    tm = SEG_TILE
    return pl.pallas_call(
        _gdn_gate_kernel,
        grid=(m // tm,),
        in_specs=[pl.BlockSpec((tm, w4), lambda i: (i, 0)),
                  pl.BlockSpec((1, w), lambda i: (0, 0)),
                  pl.BlockSpec((1, w), lambda i: (0, 0))],
        out_specs=[pl.BlockSpec((tm, w), lambda i: (i, 0)),
                   pl.BlockSpec((tm, w), lambda i: (i, 0))],
        out_shape=[jax.ShapeDtypeStruct((m, w), F32), jax.ShapeDtypeStruct((m, w), F32)],
        compiler_params=_cparams(("arbitrary",)),
        name="gdn_gates",
    )(ab, a_log.reshape(1, w).astype(F32), dt_bias.reshape(1, w).astype(F32))


def _bmm(a, b):
    return lax.dot_general(a, b, (((2,), (1,)), ((0,), (0,))), preferred_element_type=F32)


def _unit_tri_inverse(a, ri, ci):
    def same(s):
        return (ri // s) == (ci // s)

    t = (ri == ci).astype(F32) - jnp.where(same(2), a, 0.0)
    s = 2
    while s < GDN_CHUNK:
        coupling = jnp.logical_and(same(2 * s), jnp.logical_not(same(s)))
        tb = t.astype(BF16)
        tl = _bmm(tb, jnp.where(coupling, a, 0.0).astype(BF16))
        t = t - _bmm(tl.astype(BF16), tb)
        s *= 2
    return t


def _gdn_kernel(q_ref, k_ref, v_ref, col_ref, row_ref, *rest, kg, fwd):
    o_ref, state = rest[-2:]
    ch = GDN_CHUNK
    p2 = 2 * ch

    @pl.when(pl.program_id(1) == 0)
    def _():
        state[...] = jnp.zeros_like(state)

    ri = lax.broadcasted_iota(jnp.int32, (1, p2, p2), 1)
    ci = lax.broadcasted_iota(jnp.int32, (1, p2, p2), 2)
    same_head = (ri // ch) == (ci // ch)
    ahead = ci - ri if fwd else ri - ci
    incl = jnp.logical_and(same_head, ahead <= 0)
    strict = jnp.logical_and(incl, ci != ri)
    first_lanes = lax.broadcasted_iota(jnp.int32, (1, 1, p2), 2) < ch

    def pair_rows(ref):
        x = jnp.stack([ref[:, p * HEAD:(p + 1) * HEAD] for p in range(kg)]).astype(F32)
        return jnp.concatenate([x, x], axis=1)

    def head_rows(ref):
        return jnp.stack([jnp.concatenate([ref[:, (2 * p) * HEAD:(2 * p + 1) * HEAD],
                                           ref[:, (2 * p + 1) * HEAD:(2 * p + 2) * HEAD]], axis=0)
                          for p in range(kg)]).astype(F32)

    k2 = pair_rows(k_ref)
    q2 = pair_rows(q_ref)
    vf = head_rows(v_ref)
    kt = jnp.stack([k2[p].T for p in range(kg)])
    kkqk = _bmm(jnp.concatenate([k2, q2], axis=1).astype(BF16), kt.astype(BF16))
    kk, qk = kkqk[:, :p2], kkqk[:, p2:]

    def col(j0):
        c = jnp.stack([col_ref[:, j0 + p:j0 + p + 1] for p in range(kg)])
        return jnp.broadcast_to(c, (kg, p2, p2))

    gcol, bcol = col(0), col(kg)
    grow = jnp.stack([row_ref[p:p + 1, :] for p in range(kg)])
    last = ch - 1 if fwd else 0
    tot_a = grow[:, :, last:last + 1]
    tot_b = grow[:, :, ch + last:ch + last + 1]
    tot_row = jnp.where(first_lanes, tot_a, tot_b)

    decay = jnp.exp(jnp.where(incl, gcol - grow, -jnp.inf))
    tinv = _unit_tri_inverse(jnp.where(strict, bcol * kk * decay, 0.0), ri, ci)

    eg = jnp.exp(gcol)
    rhs = jnp.concatenate([vf * bcol, k2 * (bcol * eg)], axis=2).astype(BF16)
    uw = _bmm(tinv.astype(BF16), rhs)
    u, w_b = uw[:, :, :HEAD], uw[:, :, HEAD:].astype(BF16)
    qk_d = jnp.where(incl, qk * decay, 0.0).astype(BF16)
    q_dec = (q2 * eg).astype(BF16)
    kt_dec = kt * jnp.exp(tot_row - grow)

    s_a = state[:, 0]
    s_b = state[:, 1]
    ws_a = _bmm(jnp.concatenate([w_b[:, :ch], q_dec[:, :ch]], axis=1), s_a.astype(BF16))
    ws_b = _bmm(jnp.concatenate([w_b[:, ch:], q_dec[:, ch:]], axis=1), s_b.astype(BF16))
    v_new = (u - jnp.concatenate([ws_a[:, :ch], ws_b[:, :ch]], axis=1)).astype(BF16)
    o = jnp.concatenate([ws_a[:, ch:], ws_b[:, ch:]], axis=1) + _bmm(qk_d, v_new)
    kd_a = jnp.where(first_lanes, kt_dec, 0.0).astype(BF16)
    kd_b = jnp.where(first_lanes, 0.0, kt_dec).astype(BF16)
    state[:, 0] = s_a * jnp.exp(tot_a) + _bmm(kd_a, v_new)
    state[:, 1] = s_b * jnp.exp(tot_b) + _bmm(kd_b, v_new)
    if len(rest) > 2:
        oprev_ref, z_ref, gn_ref = rest[:3]
        o = _rms(o + head_rows(oprev_ref)) * gn_ref[...] * _silu(head_rows(z_ref))
    for p in range(kg):
        o_ref[:, (2 * p) * HEAD:(2 * p + 1) * HEAD] = o[p, :ch].astype(o_ref.dtype)
        o_ref[:, (2 * p + 1) * HEAD:(2 * p + 2) * HEAD] = o[p, ch:].astype(o_ref.dtype)


def _gdn_gate_layouts(gcs, beta, kg):
    m, w = gcs.shape
    ch = GDN_CHUNK
    assert HEAD == 2 * ch
    nch, ngrp = m // ch, w // (4 * kg)
    g6 = gcs.reshape(nch, ch, 2, ngrp, kg, 2)
    b6 = beta.reshape(nch, ch, 2, ngrp, kg, 2)
    to_cols = lambda t: t.transpose(2, 3, 0, 5, 1, 4).reshape(2, ngrp, nch, 2 * ch, kg)
    cols = jnp.concatenate([to_cols(g6), to_cols(b6)], axis=4)
    rows = g6.transpose(2, 3, 0, 4, 5, 1).reshape(2, ngrp, nch, kg, 2 * ch)
    return cols, rows


def _gdn_scan(qk, v, cols, rows, n_lat, kg, fwd, fused=None):
    m = v.shape[0]
    vh = v.shape[1] // HEAD
    ch = GDN_CHUNK
    nch = m // ch
    lat_ch = n_lat // ch
    ngrp = vh // (2 * kg)
    nv = 2 * kg
    d = 0 if fwd else 1

    def chunk(n):
        return (n + lat_ch) % nch if fwd else nch - 1 - n

    head_spec = pl.BlockSpec((ch, nv * HEAD), lambda g, n: (chunk(n), g))
    in_specs = [pl.BlockSpec((ch, kg * HEAD), lambda g, n: (chunk(n), g)),
                pl.BlockSpec((ch, kg * HEAD), lambda g, n: (chunk(n), ngrp + g)),
                head_spec,
                pl.BlockSpec((None, None, None, 2 * ch, 2 * kg), lambda g, n: (d, g, chunk(n), 0, 0)),
                pl.BlockSpec((None, None, None, kg, 2 * ch), lambda g, n: (d, g, chunk(n), 0, 0))]
    args = [qk, qk, v, cols, rows]
    if fused is not None:
        o_other, zsrc, z_col, g_norm = fused
        zcb = z_col // (nv * HEAD)
        in_specs += [head_spec,
                     pl.BlockSpec((ch, nv * HEAD), lambda g, n: (chunk(n), zcb + g)),
                     pl.BlockSpec((1, HEAD), lambda g, n: (0, 0))]
        args += [o_other, zsrc, g_norm.reshape(1, HEAD).astype(F32)]
    return pl.pallas_call(
        functools.partial(_gdn_kernel, kg=kg, fwd=fwd),
        grid=(ngrp, nch),
        in_specs=in_specs,
        out_specs=head_spec,
        out_shape=jax.ShapeDtypeStruct((m, vh * HEAD), BF16),
        scratch_shapes=[pltpu.VMEM((kg, 2, HEAD, HEAD), F32)],
        compiler_params=_cparams(("arbitrary", "arbitrary")),
        name="gdn_scan",
    )(*args)


def _gdn_mixer(h, w_in, conv_w, a_log, dt_bias, g_norm, w_out, n_lat):
    m, d = h.shape
    khs = d // HEAD
    vhs = 2 * khs
    kw, vw = khs * HEAD, vhs * HEAD
    qkv_w = 2 * kw + vw
    w_big = jnp.concatenate([w_in[:, :qkv_w], w_in[:, qkv_w + 4 * vhs:]], axis=1).astype(BF16)
    w_ab = w_in[:, qkv_w:qkv_w + 4 * vhs].astype(BF16)
    qkvz = _matmul(h, w_big)
    ab = _matmul(h, w_ab, out_dtype=F32, tn=4 * vhs)
    qk = _gdn_conv(qkvz, conv_w, 0, 2 * kw, True, kw, n_lat)
    v = _gdn_conv(qkvz, conv_w, 2 * kw, vw, False, 0, n_lat)
    gcs, beta = _gdn_gates(ab, a_log, dt_bias)
    kg = _pick(khs, (GDN_PAIRS, 4, 2, 1))
    cols, rows = _gdn_gate_layouts(gcs, beta, kg)
    o_fwd = _gdn_scan(qk, v, cols, rows, n_lat, kg, True)
    y = _gdn_scan(qk, v, cols, rows, n_lat, kg, False, fused=(o_fwd, qkvz, qkv_w, g_norm))
    return _matmul(y, w_out.astype(BF16))


def _pool_kernel(prev_ref, cur_ref, next_ref, w_ref, sc_ref, z_ref, o_ref, scr, *,
                 n_lat_blocks, n_lat, n_ctx):
    tm, tc = cur_ref.shape
    _fill_window(scr, prev_ref, cur_ref, next_ref, n_lat_blocks)
    t, seg_len = _seg_position(tm, n_lat_blocks, n_lat, n_ctx)
    for g, window in enumerate(POOL_WINDOWS):
        @pl.when(pl.program_id(1) == g)
        def _(radius=window // 2):
            u = scr[HALO:HALO + tm, :]
            acc = u
            for dlt in range(1, radius + 1):
                acc = acc + (scr[HALO - dlt:HALO - dlt + tm, :] + scr[HALO + dlt:HALO + dlt + tm, :])
            cnt = jnp.minimum(t + radius + 1, seg_len) - jnp.maximum(t - radius, 0)
            mean_minus = acc / cnt.astype(F32) - u
            y = _dot(mean_minus.astype(BF16), w_ref[...]) * sc_ref[...]
            o_ref[...] = (y * _silu(z_ref[...].astype(F32))).astype(o_ref.dtype)


def _pool_mixer(h, w_in, w_grp, scale, w_out, n_lat):
    m, d = h.shape
    ng, gw, _ = w_grp.shape
    assert tuple(w // 2 for w in POOL_WINDOWS) == tuple(2 ** g for g in range(ng))
    uz = _matmul(h, w_in.astype(BF16))
    tm = SEG_TILE
    specs = _window_specs(tm, gw, m, lambda g: g)
    specs += [pl.BlockSpec((None, gw, gw), lambda i, g: (g, 0, 0)),
              pl.BlockSpec((1, gw), lambda i, g: (0, g)),
              pl.BlockSpec((tm, gw), lambda i, g: (i, ng + g))]
    yg = pl.pallas_call(
        functools.partial(_pool_kernel, n_lat_blocks=n_lat // tm, n_lat=n_lat, n_ctx=m - n_lat),
        grid=(m // tm, ng),
        in_specs=specs,
        out_specs=pl.BlockSpec((tm, gw), lambda i, g: (i, g)),
        out_shape=jax.ShapeDtypeStruct((m, d), BF16),
        scratch_shapes=[pltpu.VMEM((tm + 2 * HALO, gw), F32)],
        compiler_params=_cparams(("arbitrary", "arbitrary")),
        name="pool_group",
    )(uz, uz, uz, w_grp.astype(BF16), scale.reshape(1, d).astype(F32), uz)
    return _matmul(yg, w_out.astype(BF16))


def kernel(x, c, ctx, c_ctx, ada_w, ada_b, norm_pre, norm_post, mla_w_in, mla_g_q, mla_w_q_up,
           mla_g_kv, mla_w_kv_up, mla_w_out, gdn_w_in, gdn_conv_w, gdn_a_log, gdn_dt_bias,
           gdn_g_norm, gdn_w_out, pool_w_in, pool_w_grp, pool_scale, pool_w_out):
    assert x.shape[0] == 1 and ctx.shape[0] == 1
    n_lat, d = x.shape[1], x.shape[2]
    n_ctx = ctx.shape[1]
    depth = ada_w.shape[0]
    n_mixers = 3
    assert n_lat % SEG_TILE == 0 and n_ctx % SEG_TILE == 0

    cond8 = jnp.concatenate([c.reshape(1, d), c_ctx.reshape(1, d), jnp.zeros((6, d), F32)], axis=0)
    mods = _adaln(cond8, ada_w, ada_b)
    tables = _rope_tables(n_lat, n_ctx)
    xs = jnp.concatenate([x[0], ctx[0]], axis=0)

    counts = [0, 0, 0]
    h = _prenorm(xs, norm_pre[0], mods[0], n_lat)
    for i in range(depth):
        kind = i % n_mixers
        j = counts[kind]
        counts[kind] += 1
        need_ctx = i < depth - 1
        if kind == 0:
            y = _mla_mixer(h, tables, mla_w_in[j], mla_g_q[j], mla_w_q_up[j], mla_g_kv[j],
                           mla_w_kv_up[j], mla_w_out[j], n_lat, need_ctx)
        elif kind == 1:
            y = _gdn_mixer(h, gdn_w_in[j], gdn_conv_w[j], gdn_a_log[j], gdn_dt_bias[j],
                           gdn_g_norm[j], gdn_w_out[j], n_lat)
        else:
            y = _pool_mixer(h, pool_w_in[j], pool_w_grp[j], pool_scale[j], pool_w_out[j], n_lat)
        if need_ctx:
            xs, h = _residual(xs, y, norm_post[i], mods[i], n_lat, xs.shape[0],
                              norm_pre[i + 1], mods[i + 1])
        else:
            xs = _residual(xs, y, norm_post[i], mods[i], n_lat, n_lat)
    return xs[None]
```

```python
import functools
import math

import jax
import jax.numpy as jnp
from jax import lax
from jax.experimental import pallas as pl
from jax.experimental.pallas import tpu as pltpu

F32 = jnp.float32
BF16 = jnp.bfloat16
EPS = 1e-6

HEAD = 128
MLA_ROPE = 64
MLA_KV_RANK = 512
MLA_QK_PAD = 256
ROPE_BASE = 10000.0
GRID_W = 64
GDN_CONV = 5
GDN_CHUNK = 64
GDN_PAIRS = 16
POOL_WINDOWS = (2, 4, 8, 16)
SEG_TILE = 256
HALO = 16
VMEM_LIMIT = 56 * 1024 * 1024
MM_MAX_TK = 8192
MM_W_TILE_ELEMS = 2 * 1024 * 1024
MM_A_TILE_ELEMS = 6 * 1024 * 1024


def _pick(n, prefs):
    for p in prefs:
        if n % p == 0:
            return p
    return n


def _cparams(sem):
    return pltpu.CompilerParams(dimension_semantics=sem, vmem_limit_bytes=VMEM_LIMIT)


def _dot(a, b):
    return jnp.dot(a, b, preferred_element_type=F32)


def _dot_nt(a, b):
    return lax.dot_general(a, b, (((1,), (1,)), ((), ())), preferred_element_type=F32)


def _silu(x):
    return x * jax.nn.sigmoid(x)


def _adaln_kernel(c_ref, w_ref, b_ref, o_ref):
    a = _silu(c_ref[...]).astype(BF16)
    o_ref[...] = _dot(a, w_ref[...].astype(BF16)) + b_ref[...]


def _adaln(cond8, ada_w, ada_b):
    depth, d, n = ada_w.shape
    tn = _pick(n, (512, 256, 128))
    return pl.pallas_call(
        _adaln_kernel,
        grid=(depth, n // tn),
        in_specs=[pl.BlockSpec((8, d), lambda l, j: (0, 0)),
                  pl.BlockSpec((None, d, tn), lambda l, j: (l, 0, j)),
                  pl.BlockSpec((None, 1, tn), lambda l, j: (l, 0, j))],
        out_specs=pl.BlockSpec((None, 8, tn), lambda l, j: (l, 0, j)),
        out_shape=jax.ShapeDtypeStruct((depth, 8, n), F32),
        compiler_params=_cparams(("arbitrary", "arbitrary")),
        name="adaln",
    )(cond8, ada_w, ada_b.reshape(depth, 1, n))


def _rms(xf):
    return xf * lax.rsqrt(jnp.mean(xf * xf, axis=-1, keepdims=True) + EPS)


def _prenorm_kernel(xl_ref, xc_ref, g_ref, mod_ref, xs_ref, h_ref, *, n_lat_blocks, d):
    is_ctx = pl.program_id(0) >= n_lat_blocks
    x = jnp.where(is_ctx, xc_ref[...], xl_ref[...])
    mod = jnp.where(is_ctx, mod_ref[1:2, :], mod_ref[0:1, :])
    sh, sc = mod[:, :d], mod[:, d:2 * d]
    y = _rms(x) * g_ref[...]
    xs_ref[...] = x
    h_ref[...] = (y * (1.0 + sc) + sh).astype(h_ref.dtype)


def _prenorm(x_lat, x_ctx, g, mod):
    n_lat, d = x_lat.shape
    m = n_lat + x_ctx.shape[0]
    tm = SEG_TILE
    nlb = n_lat // tm
    row_spec = pl.BlockSpec((tm, d), lambda i: (i, 0))
    return pl.pallas_call(
        functools.partial(_prenorm_kernel, n_lat_blocks=nlb, d=d),
        grid=(m // tm,),
        in_specs=[pl.BlockSpec((tm, d), lambda i: (jnp.minimum(i, nlb - 1), 0)),
                  pl.BlockSpec((tm, d), lambda i: (jnp.maximum(i - nlb, 0), 0)),
                  pl.BlockSpec((1, d), lambda i: (0, 0)),
                  pl.BlockSpec((8, 3 * d), lambda i: (0, 0))],
        out_specs=[row_spec, row_spec],
        out_shape=[jax.ShapeDtypeStruct((m, d), F32), jax.ShapeDtypeStruct((m, d), BF16)],
        compiler_params=_cparams(("arbitrary",)),
        name="prenorm",
    )(x_lat, x_ctx, g.reshape(1, d), mod)


def _residual_kernel(x_ref, y_ref, g_ref, mod_ref, *rest, n_lat_blocks, d, with_next):
    is_ctx = pl.program_id(0) >= n_lat_blocks
    mod = jnp.where(is_ctx, mod_ref[1:2, :], mod_ref[0:1, :])
    gt = mod[:, 2 * d:]
    y = _rms(y_ref[...].astype(F32)) * g_ref[...]
    x_new = x_ref[...] + gt * y
    if not with_next:
        rest[0][...] = x_new
        return
    gn_ref, modn_ref, o_ref, h_ref = rest
    o_ref[...] = x_new
    modn = jnp.where(is_ctx, modn_ref[1:2, :], modn_ref[0:1, :])
    h = _rms(x_new) * gn_ref[...]
    h_ref[...] = (h * (1.0 + modn[:, d:2 * d]) + modn[:, :d]).astype(h_ref.dtype)


def _residual(x, y, g, mod, n_lat, rows, g_next=None, mod_next=None):
    d = x.shape[1]
    tm = SEG_TILE
    with_next = g_next is not None
    row_spec = pl.BlockSpec((tm, d), lambda i: (i, 0))
    vec_spec = pl.BlockSpec((1, d), lambda i: (0, 0))
    mod_spec = pl.BlockSpec((8, 3 * d), lambda i: (0, 0))
    in_specs = [row_spec, row_spec, vec_spec, mod_spec]
    args = [x, y, g.reshape(1, d), mod]
    out_specs, out_shape = row_spec, jax.ShapeDtypeStruct((rows, d), F32)
    if with_next:
        in_specs += [vec_spec, mod_spec]
        args += [g_next.reshape(1, d), mod_next]
        out_specs = [row_spec, row_spec]
        out_shape = [out_shape, jax.ShapeDtypeStruct((rows, d), BF16)]
    return pl.pallas_call(
        functools.partial(_residual_kernel, n_lat_blocks=n_lat // tm, d=d, with_next=with_next),
        grid=(rows // tm,),
        in_specs=in_specs,
        out_specs=out_specs,
        out_shape=out_shape,
        compiler_params=_cparams(("arbitrary",)),
        name="residual",
    )(*args)


def _mm_kernel(a_ref, w_ref, *rest, nk, has_gain, epilogue, n_extra):
    pos = 0
    g_ref = rest[pos] if has_gain else None
    pos += int(has_gain)
    extra = rest[pos:pos + n_extra]
    pos += n_extra
    o_ref = rest[pos]
    acc_ref = rest[pos + 1] if nk > 1 else None

    if has_gain:
        an_ref = rest[pos + 1]

        @pl.when(pl.program_id(1) == 0)
        def _():
            an_ref[...] = (_rms(a_ref[...].astype(F32)) * g_ref[...]).astype(BF16)

        a = an_ref[...]
    else:
        a = a_ref[...]
    r = _dot(a, w_ref[...].astype(BF16))

    def finish(res):
        if epilogue is not None:
            res = epilogue(res, *[e[...] for e in extra])
        o_ref[...] = res.astype(o_ref.dtype)

    if nk == 1:
        finish(r)
    else:
        k = pl.program_id(2)

        @pl.when(k == 0)
        def _():
            acc_ref[...] = r

        @pl.when(k > 0)
        def _():
            acc_ref[...] += r

        @pl.when(k == nk - 1)
        def _():
            finish(acc_ref[...])


def _matmul(a, w, *, rows=None, a_col=0, k=None, out_dtype=BF16, gain=None,
            epilogue=None, extras=(), tn=None, tm=None, out_widen=1, w_cols=None):
    if w_cols is not None:
        layer, col0, n, skip = w_cols
        k = w.shape[1]
    else:
        k = w.shape[0] if k is None else k
        n = w.shape[1]
    rows = a.shape[0] if rows is None else rows
    tk = k if k <= MM_MAX_TK else _pick(k, (4096, 2048, 1024, 512))
    nk = k // tk
    tm = tm or _pick(rows, [t for t in (1408, 1024, 768, 512, 256, 128, 64, 32, 16)
                            if t * tk <= MM_A_TILE_ELEMS] or [8])
    tn = tn or _pick(n, [t for t in (1024, 512, 256, 128) if tk * t <= MM_W_TILE_ELEMS] or [128])
    assert a_col % tk == 0 and (gain is None or nk == 1)
    a_cb = a_col // tk
    if w_cols is None:
        w_spec = pl.BlockSpec((tk, tn), lambda i, j, kk: (kk, j))
    else:
        skip_at, skip_n = (0, 0) if skip is None else skip
        assert col0 % tn == 0 and skip_at % tn == 0 and skip_n % tn == 0

        def w_block(j):
            jb = col0 // tn + j
            return jb + jnp.where(jb >= skip_at // tn, skip_n // tn, 0) if skip_n else jb

        w_spec = pl.BlockSpec((None, tk, tn), lambda i, j, kk: (layer, kk, w_block(j)))
    in_specs = [pl.BlockSpec((tm, tk), lambda i, j, kk: (i, a_cb + kk)), w_spec]
    args = [a, w]
    if gain is not None:
        in_specs.append(pl.BlockSpec((1, tk), lambda i, j, kk: (0, 0)))
        args.append(gain.reshape(1, tk).astype(F32))
    for arr, cols, cb in extras:
        in_specs.append(pl.BlockSpec((tm, cols), lambda i, j, kk, cb=cb: (i, cb(j))))
        args.append(arr)
    scratch = [pltpu.VMEM((tm, tn), F32)] if nk > 1 else []
    if gain is not None:
        scratch = [pltpu.VMEM((tm, tk), BF16)]
    return pl.pallas_call(
        functools.partial(_mm_kernel, nk=nk, has_gain=gain is not None,
                          epilogue=epilogue, n_extra=len(extras)),
        grid=(rows // tm, n // tn, nk),
        in_specs=in_specs,
        out_specs=pl.BlockSpec((tm, tn * out_widen), lambda i, j, kk: (i, j)),
        out_shape=jax.ShapeDtypeStruct((rows, n * out_widen), out_dtype),
        scratch_shapes=scratch,
        compiler_params=_cparams(("arbitrary", "arbitrary", "arbitrary")),
        name="matmul",
    )(*args)


def _rope_tables(n_lat, n_ctx):
    rows = n_lat // GRID_W
    row = jnp.repeat(jnp.arange(rows, dtype=F32), GRID_W)
    col = jnp.tile(jnp.arange(GRID_W, dtype=F32), rows)
    n_freq = MLA_ROPE // 4
    inv_freq = ROPE_BASE ** (-jnp.arange(n_freq, dtype=F32) / n_freq)
    ang = jnp.concatenate([row[:, None] * inv_freq, col[:, None] * inv_freq], axis=-1)
    ang = jnp.concatenate([ang, jnp.zeros((n_ctx, ang.shape[1]), F32)], axis=0)
    cos, sin = jnp.cos(ang), jnp.sin(ang)
    m = ang.shape[0]
    half = MLA_ROPE // 2
    ones = jnp.ones((m, HEAD), F32)
    z_h = jnp.zeros((m, HEAD), F32)
    z_p = jnp.zeros((m, MLA_QK_PAD - HEAD - MLA_ROPE), F32)
    z_r = jnp.zeros((m, half), F32)
    c = jnp.concatenate([ones, cos, cos, z_p], axis=1)
    s1 = jnp.concatenate([z_h, -sin, z_r, z_p], axis=1)
    s2 = jnp.concatenate([z_h, z_r, sin, z_p], axis=1)
    return c, s1, s2


def _rope_heads(t, c, s1, s2):
    n = t.shape[1]
    reps = n // MLA_QK_PAD
    half = MLA_ROPE // 2
    if reps > 1:
        c, s1, s2 = (jnp.concatenate([u] * reps, axis=1) for u in (c, s1, s2))
    return t * c + pltpu.roll(t, n - half, 1) * s1 + pltpu.roll(t, half, 1) * s2


def _k_epilogue(t, kr):
    krf = kr.astype(F32)
    parts = []
    for hh in range(t.shape[1] // HEAD):
        parts += [t[:, hh * HEAD:(hh + 1) * HEAD], krf]
    return jnp.concatenate(parts, axis=1)


def _ckv_epilogue(t, c, s1, s2):
    r = MLA_KV_RANK
    return jnp.concatenate([t[:, :r], _rope_heads(t[:, r:], c, s1, s2)], axis=1)


def _flash_kernel(q_ref, k_ref, v_ref, z_ref, o_ref, *, tk, nk):
    q = q_ref[...]
    tq = q.shape[0]
    ones_col = (lax.broadcasted_iota(jnp.int32, (tk, HEAD), 1) == 0).astype(BF16)
    m = jnp.full((tq, 1), -jnp.inf, F32)
    acc = jnp.zeros((tq, 2 * HEAD), F32)
    s = _dot_nt(q, k_ref[0:tk, :])
    for c in range(nk):
        s_next = _dot_nt(q, k_ref[(c + 1) * tk:(c + 2) * tk, :]) if c + 1 < nk else None
        m_new = jnp.maximum(m, jnp.max(s, axis=-1, keepdims=True))
        p = jnp.exp2(s - m_new)
        alpha = jnp.exp2(m - m_new)
        v_ext = jnp.concatenate([v_ref[c * tk:(c + 1) * tk, :], ones_col], axis=1)
        acc = alpha * acc + _dot(p.astype(BF16), v_ext)
        m, s = m_new, s_next
    o = acc[:, :HEAD] / acc[:, HEAD:HEAD + 1]
    o_ref[...] = (o * _silu(z_ref[...].astype(F32))).astype(o_ref.dtype)


def _flash(q, k, v, zsrc, z_col, og_prev, *, q_row0, n_q, k_row0, n_k, total_rows):
    heads = v.shape[1] // HEAD
    tq = _pick(n_q, (512, 256, 128))
    tk = _pick(n_k, (768, 512, 256, 128))
    assert q_row0 % tq == 0 and k_row0 % n_k == 0
    qb0, kb0, zcb = q_row0 // tq, k_row0 // n_k, z_col // HEAD
    in_specs = [pl.BlockSpec((tq, MLA_QK_PAD), lambda h, i: (qb0 + i, h)),
                pl.BlockSpec((n_k, MLA_QK_PAD), lambda h, i: (kb0, h)),
                pl.BlockSpec((n_k, HEAD), lambda h, i: (kb0, h)),
                pl.BlockSpec((tq, HEAD), lambda h, i: (qb0 + i, zcb + h))]
    args = [q, k, v, zsrc]
    aliases = {}
    if og_prev is not None:
        in_specs.append(pl.BlockSpec(memory_space=pl.ANY))
        args.append(og_prev)
        aliases = {4: 0}

    def kern(q_ref, k_ref, v_ref, z_ref, *rest):
        _flash_kernel(q_ref, k_ref, v_ref, z_ref, rest[-1], tk=tk, nk=n_k // tk)

    return pl.pallas_call(
        kern,
        grid=(heads, n_q // tq),
        in_specs=in_specs,
        out_specs=pl.BlockSpec((tq, HEAD), lambda h, i: (qb0 + i, h)),
        out_shape=jax.ShapeDtypeStruct((total_rows, heads * HEAD), BF16),
        input_output_aliases=aliases,
        compiler_params=_cparams(("arbitrary", "arbitrary")),
        name="mla_flash",
    )(*args)


def _mla_weights(w_in, w_q_up, w_kv_up):
    d = w_in.shape[0]
    q_rank = d // 4
    heads = d // HEAD
    r0, r1, r2 = q_rank, q_rank + MLA_KV_RANK, q_rank + MLA_KV_RANK + MLA_ROPE
    pad = MLA_QK_PAD - HEAD - MLA_ROPE
    scale = (HEAD + MLA_ROPE) ** -0.5 * math.log2(math.e)
    w_z = w_in[:, r2:].astype(BF16)
    w_ckv = jnp.concatenate([w_in[:, r0:r1], jnp.zeros((d, HEAD), F32), w_in[:, r1:r2],
                             jnp.zeros((d, pad), F32)], axis=1).astype(BF16)
    wq = (w_q_up * scale).reshape(q_rank, heads, HEAD + MLA_ROPE)
    wq = jnp.concatenate([wq, jnp.zeros((q_rank, heads, pad), F32)], axis=2)
    wq = wq.reshape(q_rank, heads * MLA_QK_PAD).astype(BF16)
    wkv = w_kv_up.reshape(MLA_KV_RANK, heads, 2 * HEAD)
    wk = wkv[:, :, :HEAD].reshape(MLA_KV_RANK, heads * HEAD).astype(BF16)
    wv = wkv[:, :, HEAD:].reshape(MLA_KV_RANK, heads * HEAD).astype(BF16)
    return w_z, w_ckv, wq, wk, wv


def _mla_mixer(h, tables, layer, w_in_all, g_q, w_q_up, g_kv, w_kv_up, w_out_all, n_lat, need_ctx):
    m, d = h.shape
    n_ctx = m - n_lat
    q_rank = d // 4
    c, s1, s2 = tables
    w_z, w_ckv, wq, wk, wv = _mla_weights(w_in_all[layer], w_q_up, w_kv_up)
    first = lambda j: 0

    cq = _matmul(h, w_in_all, w_cols=(layer, 0, q_rank, None))
    z = _matmul(h, w_z)
    ckvkr = _matmul(h, w_ckv, epilogue=_ckv_epilogue, tn=w_ckv.shape[1],
                    extras=[(c, MLA_QK_PAD, first), (s1, MLA_QK_PAD, first),
                            (s2, MLA_QK_PAD, first)])
    q = _matmul(cq, wq, gain=g_q, epilogue=_rope_heads,
                extras=[(c, MLA_QK_PAD, first), (s1, MLA_QK_PAD, first),
                        (s2, MLA_QK_PAD, first)])
    kr_cb = (MLA_KV_RANK + HEAD) // HEAD
    kk = _matmul(ckvkr, wk, k=MLA_KV_RANK, gain=g_kv, epilogue=_k_epilogue,
                 extras=[(ckvkr, HEAD, lambda j: kr_cb)], out_widen=MLA_QK_PAD // HEAD)
    v = _matmul(ckvkr, wv, k=MLA_KV_RANK, gain=g_kv)

    og = _flash(q, kk, v, z, 0, None, q_row0=0, n_q=n_lat, k_row0=0, n_k=m, total_rows=m)
    if need_ctx:
        og = _flash(q, kk, v, z, 0, og, q_row0=n_lat, n_q=n_ctx, k_row0=n_lat, n_k=n_ctx,
                    total_rows=m)
    rows = m if need_ctx else n_lat
    return _matmul(og, w_out_all, rows=rows, w_cols=(layer, 0, d, None))


def _fill_window(scr, prev_ref, cur_ref, next_ref, n_lat_blocks):
    tm = cur_ref.shape[0]
    i = pl.program_id(0)
    first = jnp.logical_or(i == 0, i == n_lat_blocks)
    last = jnp.logical_or(i == n_lat_blocks - 1, i == pl.num_programs(0) - 1)
    scr[0:HALO, :] = jnp.where(first, 0.0, prev_ref[...].astype(F32))
    scr[HALO:HALO + tm, :] = cur_ref[...].astype(F32)
    scr[HALO + tm:, :] = jnp.where(last, 0.0, next_ref[...].astype(F32))


def _seg_position(tm, n_lat_blocks, n_lat, n_ctx):
    i = pl.program_id(0)
    is_ctx = i >= n_lat_blocks
    local0 = jnp.where(is_ctx, i - n_lat_blocks, i) * tm
    t = local0 + lax.broadcasted_iota(jnp.int32, (tm, 1), 0)
    return t, jnp.where(is_ctx, n_ctx, n_lat)


def _window_specs(tm, cols, n_rows, col_fn):
    hb = tm // HALO
    last = n_rows // HALO - 1
    return [pl.BlockSpec((HALO, cols), lambda i, j: (jnp.maximum(i * hb - 1, 0), col_fn(j))),
            pl.BlockSpec((tm, cols), lambda i, j: (i, col_fn(j))),
            pl.BlockSpec((HALO, cols), lambda i, j: (jnp.minimum((i + 1) * hb, last), col_fn(j)))]


def _gdn_conv_kernel(prev_ref, cur_ref, next_ref, w_ref, o_ref, scr, *, normalize,
                     n_scaled_blocks, n_lat_blocks):
    tm, tc = cur_ref.shape
    _fill_window(scr, prev_ref, cur_ref, next_ref, n_lat_blocks)
    r = GDN_CONV // 2
    acc = scr[HALO - r:HALO - r + tm, :] * w_ref[0:1, :]
    for j in range(1, GDN_CONV):
        acc = acc + scr[HALO + j - r:HALO + j - r + tm, :] * w_ref[j:j + 1, :]
    y = _silu(acc)
    if normalize:
        scale = jnp.where(pl.program_id(1) < n_scaled_blocks, HEAD ** -0.5, 1.0)
        parts = []
        for hh in range(tc // HEAD):
            u = y[:, hh * HEAD:(hh + 1) * HEAD]
            parts.append(u * (lax.rsqrt(jnp.sum(u * u, axis=-1, keepdims=True) + EPS) * scale))
        y = jnp.concatenate(parts, axis=1) if len(parts) > 1 else parts[0]
    o_ref[...] = y.astype(o_ref.dtype)


def _gdn_conv(src, conv_w, col0, ncols, normalize, n_scaled_cols, n_lat):
    m = src.shape[0]
    tm = SEG_TILE
    tc = _pick(ncols, (512, 256, 128))
    cb0 = col0 // tc
    specs = _window_specs(tm, tc, m, lambda j: cb0 + j)
    specs.append(pl.BlockSpec((8, tc), lambda i, j: (0, cb0 + j)))
    w8 = jnp.concatenate([conv_w, jnp.zeros((8 - GDN_CONV, conv_w.shape[1]), F32)], axis=0)
    return pl.pallas_call(
        functools.partial(_gdn_conv_kernel, normalize=normalize,
                          n_scaled_blocks=n_scaled_cols // tc, n_lat_blocks=n_lat // tm),
        grid=(m // tm, ncols // tc),
        in_specs=specs,
        out_specs=pl.BlockSpec((tm, tc), lambda i, j: (i, j)),
        out_shape=jax.ShapeDtypeStruct((m, ncols), BF16),
        scratch_shapes=[pltpu.VMEM((tm + 2 * HALO, tc), F32)],
        compiler_params=_cparams(("arbitrary", "arbitrary")),
        name="gdn_conv",
    )(src, src, src, w8)


def _split3(x):
    x1 = x.astype(BF16)
    r1 = x - x1.astype(F32)
    x2 = r1.astype(BF16)
    x3 = (r1 - x2.astype(F32)).astype(BF16)
    return x1, x2, x3


def _gdn_gate_kernel(ab_ref, alog_ref, dtb_ref, g_ref, beta_ref):
    tm, w = g_ref.shape
    ab = ab_ref[...]
    x = ab[:, :w] + dtb_ref[...]
    softplus = jnp.maximum(x, 0.0) + jnp.log1p(jnp.exp(-jnp.abs(x)))
    g = -jnp.exp(alog_ref[...]) * softplus
    beta_ref[...] = jax.nn.sigmoid(ab[:, w:])
    ch = GDN_CHUNK
    ri = lax.broadcasted_iota(jnp.int32, (ch, ch), 0)
    ci = lax.broadcasted_iota(jnp.int32, (ch, ch), 1)
    lower = (ci <= ri).astype(BF16)
    upper = (ci >= ri).astype(BF16)
    fwd_lane = lax.broadcasted_iota(jnp.int32, (ch, w), 1) < w // 2
    for cidx in range(tm // ch):
        parts = _split3(g[cidx * ch:(cidx + 1) * ch, :])
        pre = _dot(lower, parts[0]) + (_dot(lower, parts[1]) + _dot(lower, parts[2]))
        suf = _dot(upper, parts[0]) + (_dot(upper, parts[1]) + _dot(upper, parts[2]))
        g_ref[cidx * ch:(cidx + 1) * ch, :] = jnp.where(fwd_lane, pre, suf)


def _gdn_gates(ab, a_log, dt_bias):
    m, w4 = ab.shape
    w = w4 // 2
    tm = SEG_TILE
    return pl.pallas_call(
        _gdn_gate_kernel,
        grid=(m // tm,),
        in_specs=[pl.BlockSpec((tm, w4), lambda i: (i, 0)),
                  pl.BlockSpec((1, w), lambda i: (0, 0)),
                  pl.BlockSpec((1, w), lambda i: (0, 0))],
        out_specs=[pl.BlockSpec((tm, w), lambda i: (i, 0)),
                   pl.BlockSpec((tm, w), lambda i: (i, 0))],
        out_shape=[jax.ShapeDtypeStruct((m, w), F32), jax.ShapeDtypeStruct((m, w), F32)],
        compiler_params=_cparams(("arbitrary",)),
        name="gdn_gates",
    )(ab, a_log.reshape(1, w).astype(F32), dt_bias.reshape(1, w).astype(F32))


def _bmm(a, b):
    return lax.dot_general(a, b, (((2,), (1,)), ((0,), (0,))), preferred_element_type=F32)


def _unit_tri_inverse(a, ri, ci):
    def same(s):
        return (ri // s) == (ci // s)

    t = (ri == ci).astype(F32) - jnp.where(same(2), a, 0.0)
    s = 2
    while s < GDN_CHUNK:
        coupling = jnp.logical_and(same(2 * s), jnp.logical_not(same(s)))
        tb = t.astype(BF16)
        tl = _bmm(tb, jnp.where(coupling, a, 0.0).astype(BF16))
        t = t - _bmm(tl.astype(BF16), tb)
        s *= 2
    return t


def _gdn_kernel(q_ref, k_ref, v_ref, col_ref, row_ref, *rest, kg, fwd):
    o_ref, state = rest[-2:]
    ch = GDN_CHUNK
    p2 = 2 * ch

    @pl.when(pl.program_id(1) == 0)
    def _():
        state[...] = jnp.zeros_like(state)

    ri = lax.broadcasted_iota(jnp.int32, (1, p2, p2), 1)
    ci = lax.broadcasted_iota(jnp.int32, (1, p2, p2), 2)
    same_head = (ri // ch) == (ci // ch)
    ahead = ci - ri if fwd else ri - ci
    incl = jnp.logical_and(same_head, ahead <= 0)
    strict = jnp.logical_and(incl, ci != ri)
    first_lanes = lax.broadcasted_iota(jnp.int32, (1, 1, p2), 2) < ch

    def pair_rows(ref):
        x = jnp.stack([ref[:, p * HEAD:(p + 1) * HEAD] for p in range(kg)]).astype(F32)
        return jnp.concatenate([x, x], axis=1)

    def head_rows(ref):
        return jnp.stack([jnp.concatenate([ref[:, (2 * p) * HEAD:(2 * p + 1) * HEAD],
                                           ref[:, (2 * p + 1) * HEAD:(2 * p + 2) * HEAD]], axis=0)
                          for p in range(kg)]).astype(F32)

    k2 = pair_rows(k_ref)
    q2 = pair_rows(q_ref)
    vf = head_rows(v_ref)
    kt = jnp.stack([k2[p].T for p in range(kg)])
    kkqk = _bmm(jnp.concatenate([k2, q2], axis=1).astype(BF16), kt.astype(BF16))
    kk, qk = kkqk[:, :p2], kkqk[:, p2:]

    def col(j0):
        c = jnp.stack([col_ref[:, j0 + p:j0 + p + 1] for p in range(kg)])
        return jnp.broadcast_to(c, (kg, p2, p2))

    gcol, bcol = col(0), col(kg)
    grow = jnp.stack([row_ref[p:p + 1, :] for p in range(kg)])
    last = ch - 1 if fwd else 0
    tot_a = grow[:, :, last:last + 1]
    tot_b = grow[:, :, ch + last:ch + last + 1]
    tot_row = jnp.where(first_lanes, tot_a, tot_b)

    decay = jnp.exp(jnp.where(incl, gcol - grow, -jnp.inf))
    tinv = _unit_tri_inverse(jnp.where(strict, bcol * kk * decay, 0.0), ri, ci)

    eg = jnp.exp(gcol)
    rhs = jnp.concatenate([vf * bcol, k2 * (bcol * eg)], axis=2).astype(BF16)
    uw = _bmm(tinv.astype(BF16), rhs)
    u, w_b = uw[:, :, :HEAD], uw[:, :, HEAD:].astype(BF16)
    qk_d = jnp.where(incl, qk * decay, 0.0).astype(BF16)
    q_dec = (q2 * eg).astype(BF16)
    kt_dec = kt * jnp.exp(tot_row - grow)

    s_a = state[:, 0]
    s_b = state[:, 1]
    ws_a = _bmm(jnp.concatenate([w_b[:, :ch], q_dec[:, :ch]], axis=1), s_a.astype(BF16))
    ws_b = _bmm(jnp.concatenate([w_b[:, ch:], q_dec[:, ch:]], axis=1), s_b.astype(BF16))
    v_new = (u - jnp.concatenate([ws_a[:, :ch], ws_b[:, :ch]], axis=1)).astype(BF16)
    o = jnp.concatenate([ws_a[:, ch:], ws_b[:, ch:]], axis=1) + _bmm(qk_d, v_new)
    kd_a = jnp.where(first_lanes, kt_dec, 0.0).astype(BF16)
    kd_b = jnp.where(first_lanes, 0.0, kt_dec).astype(BF16)
    state[:, 0] = s_a * jnp.exp(tot_a) + _bmm(kd_a, v_new)
    state[:, 1] = s_b * jnp.exp(tot_b) + _bmm(kd_b, v_new)
    if len(rest) > 2:
        oprev_ref, z_ref, gn_ref = rest[:3]
        o = _rms(o + head_rows(oprev_ref)) * gn_ref[...] * _silu(head_rows(z_ref))
    for p in range(kg):
        o_ref[:, (2 * p) * HEAD:(2 * p + 1) * HEAD] = o[p, :ch].astype(o_ref.dtype)
        o_ref[:, (2 * p + 1) * HEAD:(2 * p + 2) * HEAD] = o[p, ch:].astype(o_ref.dtype)


def _gdn_gate_layouts(gcs, beta, kg):
    m, w = gcs.shape
    ch = GDN_CHUNK
    assert HEAD == 2 * ch
    nch, ngrp = m // ch, w // (4 * kg)
    g6 = gcs.reshape(nch, ch, 2, ngrp, kg, 2)
    b6 = beta.reshape(nch, ch, 2, ngrp, kg, 2)
    to_cols = lambda t: t.transpose(2, 3, 0, 5, 1, 4).reshape(2, ngrp, nch, 2 * ch, kg)
    cols = jnp.concatenate([to_cols(g6), to_cols(b6)], axis=4)
    rows = g6.transpose(2, 3, 0, 4, 5, 1).reshape(2, ngrp, nch, kg, 2 * ch)
    return cols, rows


def _gdn_scan(qk, v, cols, rows, n_lat, kg, fwd, fused=None):
    m = v.shape[0]
    vh = v.shape[1] // HEAD
    ch = GDN_CHUNK
    nch = m // ch
    lat_ch = n_lat // ch
    ngrp = vh // (2 * kg)
    nv = 2 * kg
    d = 0 if fwd else 1

    def chunk(n):
        return (n + lat_ch) % nch if fwd else nch - 1 - n

    head_spec = pl.BlockSpec((ch, nv * HEAD), lambda g, n: (chunk(n), g))
    in_specs = [pl.BlockSpec((ch, kg * HEAD), lambda g, n: (chunk(n), g)),
                pl.BlockSpec((ch, kg * HEAD), lambda g, n: (chunk(n), ngrp + g)),
                head_spec,
                pl.BlockSpec((None, None, None, 2 * ch, 2 * kg), lambda g, n: (d, g, chunk(n), 0, 0)),
                pl.BlockSpec((None, None, None, kg, 2 * ch), lambda g, n: (d, g, chunk(n), 0, 0))]
    args = [qk, qk, v, cols, rows]
    if fused is not None:
        o_other, zsrc, z_col, g_norm = fused
        zcb = z_col // (nv * HEAD)
        in_specs += [head_spec,
                     pl.BlockSpec((ch, nv * HEAD), lambda g, n: (chunk(n), zcb + g)),
                     pl.BlockSpec((1, HEAD), lambda g, n: (0, 0))]
        args += [o_other, zsrc, g_norm.reshape(1, HEAD).astype(F32)]
    return pl.pallas_call(
        functools.partial(_gdn_kernel, kg=kg, fwd=fwd),
        grid=(ngrp, nch),
        in_specs=in_specs,
        out_specs=head_spec,
        out_shape=jax.ShapeDtypeStruct((m, vh * HEAD), BF16),
        scratch_shapes=[pltpu.VMEM((kg, 2, HEAD, HEAD), F32)],
        compiler_params=_cparams(("arbitrary", "arbitrary")),
        name="gdn_scan",
    )(*args)


def _gdn_mixer(h, layer, w_in_all, conv_w, a_log, dt_bias, g_norm, w_out_all, n_lat):
    m, d = h.shape
    khs = d // HEAD
    vhs = 2 * khs
    kw, vw = khs * HEAD, vhs * HEAD
    qkv_w = 2 * kw + vw
    n_ab = 4 * vhs
    if n_ab % 256 == 0 and qkv_w % n_ab == 0:
        qkvz = _matmul(h, w_in_all, tn=n_ab, w_cols=(layer, 0, qkv_w + vw, (qkv_w, n_ab)))
        ab = _matmul(h, w_in_all, out_dtype=F32, tn=n_ab, w_cols=(layer, qkv_w, n_ab, None))
    else:
        w_in = w_in_all[layer]
        w_big = jnp.concatenate([w_in[:, :qkv_w], w_in[:, qkv_w + n_ab:]], axis=1).astype(BF16)
        qkvz = _matmul(h, w_big)
        ab = _matmul(h, w_in[:, qkv_w:qkv_w + n_ab].astype(BF16), out_dtype=F32, tn=n_ab)
    qk = _gdn_conv(qkvz, conv_w, 0, 2 * kw, True, kw, n_lat)
    v = _gdn_conv(qkvz, conv_w, 2 * kw, vw, False, 0, n_lat)
    gcs, beta = _gdn_gates(ab, a_log, dt_bias)
    kg = _pick(khs, (GDN_PAIRS, 4, 2, 1))
    cols, rows = _gdn_gate_layouts(gcs, beta, kg)
    o_fwd = _gdn_scan(qk, v, cols, rows, n_lat, kg, True)
    y = _gdn_scan(qk, v, cols, rows, n_lat, kg, False, fused=(o_fwd, qkvz, qkv_w, g_norm))
    return _matmul(y, w_out_all, w_cols=(layer, 0, d, None))


def _pool_kernel(prev_ref, cur_ref, next_ref, w_ref, sc_ref, z_ref, o_ref, scr, *,
                 n_lat_blocks, n_lat, n_ctx):
    tm, tc = cur_ref.shape
    _fill_window(scr, prev_ref, cur_ref, next_ref, n_lat_blocks)
    t, seg_len = _seg_position(tm, n_lat_blocks, n_lat, n_ctx)
    for g, window in enumerate(POOL_WINDOWS):
        @pl.when(pl.program_id(1) == g)
        def _(radius=window // 2):
            u = scr[HALO:HALO + tm, :]
            acc = u
            for dlt in range(1, radius + 1):
                acc = acc + (scr[HALO - dlt:HALO - dlt + tm, :] + scr[HALO + dlt:HALO + dlt + tm, :])
            cnt = jnp.minimum(t + radius + 1, seg_len) - jnp.maximum(t - radius, 0)
            mean_minus = acc / cnt.astype(F32) - u
            y = _dot(mean_minus.astype(BF16), w_ref[...]) * sc_ref[...]
            o_ref[...] = (y * _silu(z_ref[...].astype(F32))).astype(o_ref.dtype)


def _pool_mixer(h, layer, w_in_all, w_grp, scale, w_out_all, n_lat):
    m, d = h.shape
    ng, gw, _ = w_grp.shape
    assert tuple(w // 2 for w in POOL_WINDOWS) == tuple(2 ** g for g in range(ng))
    uz = _matmul(h, w_in_all, w_cols=(layer, 0, 2 * d, None))
    tm = SEG_TILE
    specs = _window_specs(tm, gw, m, lambda g: g)
    specs += [pl.BlockSpec((None, gw, gw), lambda i, g: (g, 0, 0)),
              pl.BlockSpec((1, gw), lambda i, g: (0, g)),
              pl.BlockSpec((tm, gw), lambda i, g: (i, ng + g))]
    yg = pl.pallas_call(
        functools.partial(_pool_kernel, n_lat_blocks=n_lat // tm, n_lat=n_lat, n_ctx=m - n_lat),
        grid=(m // tm, ng),
        in_specs=specs,
        out_specs=pl.BlockSpec((tm, gw), lambda i, g: (i, g)),
        out_shape=jax.ShapeDtypeStruct((m, d), BF16),
        scratch_shapes=[pltpu.VMEM((tm + 2 * HALO, gw), F32)],
        compiler_params=_cparams(("arbitrary", "arbitrary")),
        name="pool_group",
    )(uz, uz, uz, w_grp.astype(BF16), scale.reshape(1, d).astype(F32), uz)
    return _matmul(yg, w_out_all, w_cols=(layer, 0, d, None))


def kernel(x, c, ctx, c_ctx, ada_w, ada_b, norm_pre, norm_post, mla_w_in, mla_g_q, mla_w_q_up,
           mla_g_kv, mla_w_kv_up, mla_w_out, gdn_w_in, gdn_conv_w, gdn_a_log, gdn_dt_bias,
           gdn_g_norm, gdn_w_out, pool_w_in, pool_w_grp, pool_scale, pool_w_out):
    assert x.shape[0] == 1 and ctx.shape[0] == 1
    n_lat, d = x.shape[1], x.shape[2]
    n_ctx = ctx.shape[1]
    depth = ada_w.shape[0]
    n_mixers = 3
    assert n_lat % SEG_TILE == 0 and n_ctx % SEG_TILE == 0

    cond8 = jnp.concatenate([c.reshape(1, d), c_ctx.reshape(1, d), jnp.zeros((6, d), F32)], axis=0)
    mods = _adaln(cond8, ada_w, ada_b)
    tables = _rope_tables(n_lat, n_ctx)
    xs, h = _prenorm(x[0], ctx[0], norm_pre[0], mods[0])

    counts = [0, 0, 0]
    for i in range(depth):
        kind = i % n_mixers
        j = counts[kind]
        counts[kind] += 1
        need_ctx = i < depth - 1
        if kind == 0:
            y = _mla_mixer(h, tables, j, mla_w_in, mla_g_q[j], mla_w_q_up[j], mla_g_kv[j],
                           mla_w_kv_up[j], mla_w_out, n_lat, need_ctx)
        elif kind == 1:
            y = _gdn_mixer(h, j, gdn_w_in, gdn_conv_w[j], gdn_a_log[j], gdn_dt_bias[j],
                           gdn_g_norm[j], gdn_w_out, n_lat)
        else:
            y = _pool_mixer(h, j, pool_w_in, pool_w_grp[j], pool_scale[j], pool_w_out, n_lat)
        if need_ctx:
            xs, h = _residual(xs, y, norm_post[i], mods[i], n_lat, xs.shape[0],
                              norm_pre[i + 1], mods[i + 1])
        else:
            xs = _residual(xs, y, norm_post[i], mods[i], n_lat, n_lat)
    return xs[None]
```

```python
import functools
import math

import jax
import jax.numpy as jnp
from jax import lax
from jax.experimental import pallas as pl
from jax.experimental.pallas import tpu as pltpu

F32 = jnp.float32
BF16 = jnp.bfloat16
EPS = 1e-6

HEAD = 128
MLA_ROPE = 64
MLA_KV_RANK = 512
MLA_QK_PAD = 256
ROPE_BASE = 10000.0
GRID_W = 64
GDN_CONV = 5
GDN_CHUNK = 64
GDN_PAIRS = 16
POOL_WINDOWS = (2, 4, 8, 16)
SEG_TILE = 256
HALO = 16
VMEM_LIMIT = 56 * 1024 * 1024
MM_MAX_TK = 8192
MM_W_TILE_ELEMS = 2 * 1024 * 1024
MM_A_TILE_ELEMS = 6 * 1024 * 1024


def _pick(n, prefs):
    for p in prefs:
        if n % p == 0:
            return p
    return n


def _cparams(sem):
    return pltpu.CompilerParams(dimension_semantics=sem, vmem_limit_bytes=VMEM_LIMIT)


def _dot(a, b):
    return jnp.dot(a, b, preferred_element_type=F32)


def _dot_nt(a, b):
    return lax.dot_general(a, b, (((1,), (1,)), ((), ())), preferred_element_type=F32)


def _silu(x):
    return x * jax.nn.sigmoid(x)


def _adaln_kernel(c_ref, w_ref, b_ref, o_ref):
    a = _silu(c_ref[...]).astype(BF16)
    o_ref[...] = _dot(a, w_ref[...].astype(BF16)) + b_ref[...]


def _adaln(cond8, ada_w, ada_b):
    depth, d, n = ada_w.shape
    tn = _pick(n, (512, 256, 128))
    return pl.pallas_call(
        _adaln_kernel,
        grid=(depth, n // tn),
        in_specs=[pl.BlockSpec((8, d), lambda l, j: (0, 0)),
                  pl.BlockSpec((None, d, tn), lambda l, j: (l, 0, j)),
                  pl.BlockSpec((None, 1, tn), lambda l, j: (l, 0, j))],
        out_specs=pl.BlockSpec((None, 8, tn), lambda l, j: (l, 0, j)),
        out_shape=jax.ShapeDtypeStruct((depth, 8, n), F32),
        compiler_params=_cparams(("arbitrary", "arbitrary")),
        name="adaln",
    )(cond8, ada_w, ada_b.reshape(depth, 1, n))


def _rms(xf):
    return xf * lax.rsqrt(jnp.mean(xf * xf, axis=-1, keepdims=True) + EPS)


def _prenorm_kernel(xl_ref, xc_ref, g_ref, mod_ref, xs_ref, h_ref, *, n_lat_blocks, d):
    is_ctx = pl.program_id(0) >= n_lat_blocks
    x = jnp.where(is_ctx, xc_ref[...], xl_ref[...])
    mod = jnp.where(is_ctx, mod_ref[1:2, :], mod_ref[0:1, :])
    sh, sc = mod[:, :d], mod[:, d:2 * d]
    y = _rms(x) * g_ref[...]
    xs_ref[...] = x
    h_ref[...] = (y * (1.0 + sc) + sh).astype(h_ref.dtype)


def _prenorm(x_lat, x_ctx, g, mod):
    n_lat, d = x_lat.shape
    m = n_lat + x_ctx.shape[0]
    tm = SEG_TILE
    nlb = n_lat // tm
    row_spec = pl.BlockSpec((tm, d), lambda i: (i, 0))
    return pl.pallas_call(
        functools.partial(_prenorm_kernel, n_lat_blocks=nlb, d=d),
        grid=(m // tm,),
        in_specs=[pl.BlockSpec((tm, d), lambda i: (jnp.minimum(i, nlb - 1), 0)),
                  pl.BlockSpec((tm, d), lambda i: (jnp.maximum(i - nlb, 0), 0)),
                  pl.BlockSpec((1, d), lambda i: (0, 0)),
                  pl.BlockSpec((8, 3 * d), lambda i: (0, 0))],
        out_specs=[row_spec, row_spec],
        out_shape=[jax.ShapeDtypeStruct((m, d), F32), jax.ShapeDtypeStruct((m, d), BF16)],
        compiler_params=_cparams(("arbitrary",)),
        name="prenorm",
    )(x_lat, x_ctx, g.reshape(1, d), mod)


def _residual_kernel(x_ref, y_ref, g_ref, mod_ref, *rest, n_lat_blocks, d, with_next):
    is_ctx = pl.program_id(0) >= n_lat_blocks
    mod = jnp.where(is_ctx, mod_ref[1:2, :], mod_ref[0:1, :])
    gt = mod[:, 2 * d:]
    y = _rms(y_ref[...].astype(F32)) * g_ref[...]
    x_new = x_ref[...] + gt * y
    if not with_next:
        rest[0][...] = x_new
        return
    gn_ref, modn_ref, o_ref, h_ref = rest
    o_ref[...] = x_new
    modn = jnp.where(is_ctx, modn_ref[1:2, :], modn_ref[0:1, :])
    h = _rms(x_new) * gn_ref[...]
    h_ref[...] = (h * (1.0 + modn[:, d:2 * d]) + modn[:, :d]).astype(h_ref.dtype)


def _residual(x, y, g, mod, n_lat, rows, g_next=None, mod_next=None):
    d = x.shape[1]
    tm = SEG_TILE
    with_next = g_next is not None
    row_spec = pl.BlockSpec((tm, d), lambda i: (i, 0))
    vec_spec = pl.BlockSpec((1, d), lambda i: (0, 0))
    mod_spec = pl.BlockSpec((8, 3 * d), lambda i: (0, 0))
    in_specs = [row_spec, row_spec, vec_spec, mod_spec]
    args = [x, y, g.reshape(1, d), mod]
    out_specs, out_shape = row_spec, jax.ShapeDtypeStruct((rows, d), F32)
    if with_next:
        in_specs += [vec_spec, mod_spec]
        args += [g_next.reshape(1, d), mod_next]
        out_specs = [row_spec, row_spec]
        out_shape = [out_shape, jax.ShapeDtypeStruct((rows, d), BF16)]
    return pl.pallas_call(
        functools.partial(_residual_kernel, n_lat_blocks=n_lat // tm, d=d, with_next=with_next),
        grid=(rows // tm,),
        in_specs=in_specs,
        out_specs=out_specs,
        out_shape=out_shape,
        compiler_params=_cparams(("arbitrary",)),
        name="residual",
    )(*args)


def _mm_kernel(a_ref, w_ref, *rest, nk, has_gain, epilogue, n_extra):
    pos = 0
    g_ref = rest[pos] if has_gain else None
    pos += int(has_gain)
    extra = rest[pos:pos + n_extra]
    pos += n_extra
    o_ref = rest[pos]
    acc_ref = rest[pos + 1] if nk > 1 else None

    if has_gain:
        an_ref = rest[pos + 1]

        @pl.when(pl.program_id(1) == 0)
        def _():
            an_ref[...] = (_rms(a_ref[...].astype(F32)) * g_ref[...]).astype(BF16)

        a = an_ref[...]
    else:
        a = a_ref[...]
    r = _dot(a, w_ref[...].astype(BF16))

    def finish(res):
        if epilogue is not None:
            res = epilogue(res, *[e[...] for e in extra])
        o_ref[...] = res.astype(o_ref.dtype)

    if nk == 1:
        finish(r)
    else:
        k = pl.program_id(2)

        @pl.when(k == 0)
        def _():
            acc_ref[...] = r

        @pl.when(k > 0)
        def _():
            acc_ref[...] += r

        @pl.when(k == nk - 1)
        def _():
            finish(acc_ref[...])


def _matmul(a, w, *, rows=None, a_col=0, k=None, out_dtype=BF16, gain=None,
            epilogue=None, extras=(), tn=None, tm=None, out_widen=1, w_cols=None):
    if w_cols is not None:
        layer, col0, n, skip = w_cols
        k = w.shape[1]
    else:
        k = w.shape[0] if k is None else k
        n = w.shape[1]
    rows = a.shape[0] if rows is None else rows
    tk = k if k <= MM_MAX_TK else _pick(k, (4096, 2048, 1024, 512))
    nk = k // tk
    tm = tm or _pick(rows, [t for t in (1408, 1024, 768, 512, 256, 128, 64, 32, 16)
                            if t * tk <= MM_A_TILE_ELEMS] or [8])
    tn = tn or _pick(n, [t for t in (1024, 512, 256, 128) if tk * t <= MM_W_TILE_ELEMS] or [128])
    assert a_col % tk == 0 and (gain is None or nk == 1)
    a_cb = a_col // tk
    if w_cols is None:
        w_spec = pl.BlockSpec((tk, tn), lambda i, j, kk: (kk, j))
    else:
        skip_at, skip_n = (0, 0) if skip is None else skip
        assert col0 % tn == 0 and skip_at % tn == 0 and skip_n % tn == 0

        def w_block(j):
            jb = col0 // tn + j
            return jb + jnp.where(jb >= skip_at // tn, skip_n // tn, 0) if skip_n else jb

        w_spec = pl.BlockSpec((None, tk, tn), lambda i, j, kk: (layer, kk, w_block(j)))
    in_specs = [pl.BlockSpec((tm, tk), lambda i, j, kk: (i, a_cb + kk)), w_spec]
    args = [a, w]
    if gain is not None:
        in_specs.append(pl.BlockSpec((1, tk), lambda i, j, kk: (0, 0)))
        args.append(gain.reshape(1, tk).astype(F32))
    for arr, cols, cb in extras:
        in_specs.append(pl.BlockSpec((tm, cols), lambda i, j, kk, cb=cb: (i, cb(j))))
        args.append(arr)
    scratch = [pltpu.VMEM((tm, tn), F32)] if nk > 1 else []
    if gain is not None:
        scratch = [pltpu.VMEM((tm, tk), BF16)]
    return pl.pallas_call(
        functools.partial(_mm_kernel, nk=nk, has_gain=gain is not None,
                          epilogue=epilogue, n_extra=len(extras)),
        grid=(rows // tm, n // tn, nk),
        in_specs=in_specs,
        out_specs=pl.BlockSpec((tm, tn * out_widen), lambda i, j, kk: (i, j)),
        out_shape=jax.ShapeDtypeStruct((rows, n * out_widen), out_dtype),
        scratch_shapes=scratch,
        compiler_params=_cparams(("arbitrary", "arbitrary", "arbitrary")),
        name="matmul",
    )(*args)


def _rope_tables(n_lat, n_ctx):
    rows = n_lat // GRID_W
    row = jnp.repeat(jnp.arange(rows, dtype=F32), GRID_W)
    col = jnp.tile(jnp.arange(GRID_W, dtype=F32), rows)
    n_freq = MLA_ROPE // 4
    inv_freq = ROPE_BASE ** (-jnp.arange(n_freq, dtype=F32) / n_freq)
    ang = jnp.concatenate([row[:, None] * inv_freq, col[:, None] * inv_freq], axis=-1)
    ang = jnp.concatenate([ang, jnp.zeros((n_ctx, ang.shape[1]), F32)], axis=0)
    cos, sin = jnp.cos(ang), jnp.sin(ang)
    m = ang.shape[0]
    half = MLA_ROPE // 2
    ones = jnp.ones((m, HEAD), F32)
    z_h = jnp.zeros((m, HEAD), F32)
    z_p = jnp.zeros((m, MLA_QK_PAD - HEAD - MLA_ROPE), F32)
    z_r = jnp.zeros((m, half), F32)
    c = jnp.concatenate([ones, cos, cos, z_p], axis=1)
    s1 = jnp.concatenate([z_h, -sin, z_r, z_p], axis=1)
    s2 = jnp.concatenate([z_h, z_r, sin, z_p], axis=1)
    return c, s1, s2


def _rope_heads(t, c, s1, s2):
    n = t.shape[1]
    reps = n // MLA_QK_PAD
    half = MLA_ROPE // 2
    if reps > 1:
        c, s1, s2 = (jnp.concatenate([u] * reps, axis=1) for u in (c, s1, s2))
    return t * c + pltpu.roll(t, n - half, 1) * s1 + pltpu.roll(t, half, 1) * s2


def _k_epilogue(t, kr):
    krf = kr.astype(F32)
    parts = []
    for hh in range(t.shape[1] // HEAD):
        parts += [t[:, hh * HEAD:(hh + 1) * HEAD], krf]
    return jnp.concatenate(parts, axis=1)


def _ckv_epilogue(t, c, s1, s2):
    r = MLA_KV_RANK
    return jnp.concatenate([t[:, :r], _rope_heads(t[:, r:], c, s1, s2)], axis=1)


def _flash_kernel(q_ref, k_ref, v_ref, z_ref, o_ref, *, tk, nk):
    q = q_ref[...]
    tq = q.shape[0]
    ones_col = (lax.broadcasted_iota(jnp.int32, (tk, HEAD), 1) == 0).astype(BF16)
    m = jnp.full((tq, 1), -jnp.inf, F32)
    acc = jnp.zeros((tq, 2 * HEAD), F32)
    s = _dot_nt(q, k_ref[0:tk, :])
    for c in range(nk):
        s_next = _dot_nt(q, k_ref[(c + 1) * tk:(c + 2) * tk, :]) if c + 1 < nk else None
        m_new = jnp.maximum(m, jnp.max(s, axis=-1, keepdims=True))
        p = jnp.exp2(s - m_new)
        alpha = jnp.exp2(m - m_new)
        v_ext = jnp.concatenate([v_ref[c * tk:(c + 1) * tk, :], ones_col], axis=1)
        acc = alpha * acc + _dot(p.astype(BF16), v_ext)
        m, s = m_new, s_next
    o = acc[:, :HEAD] / acc[:, HEAD:HEAD + 1]
    o_ref[...] = (o * _silu(z_ref[...].astype(F32))).astype(o_ref.dtype)


def _flash(q, k, v, zsrc, z_col, og_prev, *, q_row0, n_q, k_row0, n_k, total_rows):
    heads = v.shape[1] // HEAD
    tq = _pick(n_q, (1024, 512, 256, 128))
    tk = _pick(n_k, (2816, 768, 512, 256, 128))
    assert q_row0 % tq == 0 and k_row0 % n_k == 0
    qb0, kb0, zcb = q_row0 // tq, k_row0 // n_k, z_col // HEAD
    in_specs = [pl.BlockSpec((tq, MLA_QK_PAD), lambda h, i: (qb0 + i, h)),
                pl.BlockSpec((n_k, MLA_QK_PAD), lambda h, i: (kb0, h)),
                pl.BlockSpec((n_k, HEAD), lambda h, i: (kb0, h)),
                pl.BlockSpec((tq, HEAD), lambda h, i: (qb0 + i, zcb + h))]
    args = [q, k, v, zsrc]
    aliases = {}
    if og_prev is not None:
        in_specs.append(pl.BlockSpec(memory_space=pl.ANY))
        args.append(og_prev)
        aliases = {4: 0}

    def kern(q_ref, k_ref, v_ref, z_ref, *rest):
        _flash_kernel(q_ref, k_ref, v_ref, z_ref, rest[-1], tk=tk, nk=n_k // tk)

    return pl.pallas_call(
        kern,
        grid=(heads, n_q // tq),
        in_specs=in_specs,
        out_specs=pl.BlockSpec((tq, HEAD), lambda h, i: (qb0 + i, h)),
        out_shape=jax.ShapeDtypeStruct((total_rows, heads * HEAD), BF16),
        input_output_aliases=aliases,
        compiler_params=_cparams(("arbitrary", "arbitrary")),
        name="mla_flash",
    )(*args)


def _mla_weights(w_in, w_q_up, w_kv_up):
    d = w_in.shape[0]
    q_rank = d // 4
    heads = d // HEAD
    r0, r1, r2 = q_rank, q_rank + MLA_KV_RANK, q_rank + MLA_KV_RANK + MLA_ROPE
    pad = MLA_QK_PAD - HEAD - MLA_ROPE
    scale = (HEAD + MLA_ROPE) ** -0.5 * math.log2(math.e)
    w_z = w_in[:, r2:].astype(BF16)
    w_ckv = jnp.concatenate([w_in[:, r0:r1], jnp.zeros((d, HEAD), F32), w_in[:, r1:r2],
                             jnp.zeros((d, pad), F32)], axis=1).astype(BF16)
    wq = (w_q_up * scale).reshape(q_rank, heads, HEAD + MLA_ROPE)
    wq = jnp.concatenate([wq, jnp.zeros((q_rank, heads, pad), F32)], axis=2)
    wq = wq.reshape(q_rank, heads * MLA_QK_PAD).astype(BF16)
    wkv = w_kv_up.reshape(MLA_KV_RANK, heads, 2 * HEAD)
    wk = wkv[:, :, :HEAD].reshape(MLA_KV_RANK, heads * HEAD).astype(BF16)
    wv = wkv[:, :, HEAD:].reshape(MLA_KV_RANK, heads * HEAD).astype(BF16)
    return w_z, w_ckv, wq, wk, wv


def _mla_mixer(h, tables, layer, w_in_all, g_q, w_q_up, g_kv, w_kv_up, w_out_all, n_lat, need_ctx):
    m, d = h.shape
    n_ctx = m - n_lat
    q_rank = d // 4
    c, s1, s2 = tables
    w_z, w_ckv, wq, wk, wv = _mla_weights(w_in_all[layer], w_q_up, w_kv_up)
    first = lambda j: 0

    cq = _matmul(h, w_in_all[layer][:, :q_rank].astype(BF16))
    z = _matmul(h, w_z)
    ckvkr = _matmul(h, w_ckv, epilogue=_ckv_epilogue, tn=w_ckv.shape[1],
                    extras=[(c, MLA_QK_PAD, first), (s1, MLA_QK_PAD, first),
                            (s2, MLA_QK_PAD, first)])
    q = _matmul(cq, wq, gain=g_q, epilogue=_rope_heads,
                extras=[(c, MLA_QK_PAD, first), (s1, MLA_QK_PAD, first),
                        (s2, MLA_QK_PAD, first)])
    kr_cb = (MLA_KV_RANK + HEAD) // HEAD
    kk = _matmul(ckvkr, wk, k=MLA_KV_RANK, gain=g_kv, epilogue=_k_epilogue,
                 extras=[(ckvkr, HEAD, lambda j: kr_cb)], out_widen=MLA_QK_PAD // HEAD)
    v = _matmul(ckvkr, wv, k=MLA_KV_RANK, gain=g_kv)

    og = _flash(q, kk, v, z, 0, None, q_row0=0, n_q=n_lat, k_row0=0, n_k=m, total_rows=m)
    if need_ctx:
        og = _flash(q, kk, v, z, 0, og, q_row0=n_lat, n_q=n_ctx, k_row0=n_lat, n_k=n_ctx,
                    total_rows=m)
    rows = m if need_ctx else n_lat
    return _matmul(og, w_out_all, rows=rows, w_cols=(layer, 0, d, None))


def _fill_window(scr, prev_ref, cur_ref, next_ref, n_lat_blocks):
    tm = cur_ref.shape[0]
    i = pl.program_id(0)
    first = jnp.logical_or(i == 0, i == n_lat_blocks)
    last = jnp.logical_or(i == n_lat_blocks - 1, i == pl.num_programs(0) - 1)
    scr[0:HALO, :] = jnp.where(first, 0.0, prev_ref[...].astype(F32))
    scr[HALO:HALO + tm, :] = cur_ref[...].astype(F32)
    scr[HALO + tm:, :] = jnp.where(last, 0.0, next_ref[...].astype(F32))


def _seg_position(tm, n_lat_blocks, n_lat, n_ctx):
    i = pl.program_id(0)
    is_ctx = i >= n_lat_blocks
    local0 = jnp.where(is_ctx, i - n_lat_blocks, i) * tm
    t = local0 + lax.broadcasted_iota(jnp.int32, (tm, 1), 0)
    return t, jnp.where(is_ctx, n_ctx, n_lat)


def _window_specs(tm, cols, n_rows, col_fn):
    hb = tm // HALO
    last = n_rows // HALO - 1
    return [pl.BlockSpec((HALO, cols), lambda i, j: (jnp.maximum(i * hb - 1, 0), col_fn(j))),
            pl.BlockSpec((tm, cols), lambda i, j: (i, col_fn(j))),
            pl.BlockSpec((HALO, cols), lambda i, j: (jnp.minimum((i + 1) * hb, last), col_fn(j)))]


def _gdn_conv_kernel(prev_ref, cur_ref, next_ref, w_ref, o_ref, scr, *, normalize,
                     n_scaled_blocks, n_lat_blocks):
    tm, tc = cur_ref.shape
    _fill_window(scr, prev_ref, cur_ref, next_ref, n_lat_blocks)
    r = GDN_CONV // 2
    acc = scr[HALO - r:HALO - r + tm, :] * w_ref[0:1, :]
    for j in range(1, GDN_CONV):
        acc = acc + scr[HALO + j - r:HALO + j - r + tm, :] * w_ref[j:j + 1, :]
    y = _silu(acc)
    if normalize:
        scale = jnp.where(pl.program_id(1) < n_scaled_blocks, HEAD ** -0.5, 1.0)
        parts = []
        for hh in range(tc // HEAD):
            u = y[:, hh * HEAD:(hh + 1) * HEAD]
            parts.append(u * (lax.rsqrt(jnp.sum(u * u, axis=-1, keepdims=True) + EPS) * scale))
        y = jnp.concatenate(parts, axis=1) if len(parts) > 1 else parts[0]
    o_ref[...] = y.astype(o_ref.dtype)


def _gdn_conv(src, conv_w, col0, ncols, normalize, n_scaled_cols, n_lat):
    m = src.shape[0]
    tm = SEG_TILE
    tc = _pick(ncols, (512, 256, 128))
    cb0 = col0 // tc
    specs = _window_specs(tm, tc, m, lambda j: cb0 + j)
    specs.append(pl.BlockSpec((8, tc), lambda i, j: (0, cb0 + j)))
    w8 = jnp.concatenate([conv_w, jnp.zeros((8 - GDN_CONV, conv_w.shape[1]), F32)], axis=0)
    return pl.pallas_call(
        functools.partial(_gdn_conv_kernel, normalize=normalize,
                          n_scaled_blocks=n_scaled_cols // tc, n_lat_blocks=n_lat // tm),
        grid=(m // tm, ncols // tc),
        in_specs=specs,
        out_specs=pl.BlockSpec((tm, tc), lambda i, j: (i, j)),
        out_shape=jax.ShapeDtypeStruct((m, ncols), BF16),
        scratch_shapes=[pltpu.VMEM((tm + 2 * HALO, tc), F32)],
        compiler_params=_cparams(("arbitrary", "arbitrary")),
        name="gdn_conv",
    )(src, src, src, w8)


def _split3(x):
    x1 = x.astype(BF16)
    r1 = x - x1.astype(F32)
    x2 = r1.astype(BF16)
    x3 = (r1 - x2.astype(F32)).astype(BF16)
    return x1, x2, x3


def _gdn_gate_kernel(ab_ref, alog_ref, dtb_ref, g_ref, beta_ref):
    tm, w = g_ref.shape
    ab = ab_ref[...]
    x = ab[:, :w] + dtb_ref[...]
    softplus = jnp.maximum(x, 0.0) + jnp.log1p(jnp.exp(-jnp.abs(x)))
    g = -jnp.exp(alog_ref[...]) * softplus
    beta_ref[...] = jax.nn.sigmoid(ab[:, w:])
    ch = GDN_CHUNK
    ri = lax.broadcasted_iota(jnp.int32, (ch, ch), 0)
    ci = lax.broadcasted_iota(jnp.int32, (ch, ch), 1)
    lower = (ci <= ri).astype(BF16)
    upper = (ci >= ri).astype(BF16)
    fwd_lane = lax.broadcasted_iota(jnp.int32, (ch, w), 1) < w // 2
    for cidx in range(tm // ch):
        parts = _split3(g[cidx * ch:(cidx + 1) * ch, :])
        pre = _dot(lower, parts[0]) + (_dot(lower, parts[1]) + _dot(lower, parts[2]))
        suf = _dot(upper, parts[0]) + (_dot(upper, parts[1]) + _dot(upper, parts[2]))
        g_ref[cidx * ch:(cidx + 1) * ch, :] = jnp.where(fwd_lane, pre, suf)


def _gdn_gates(ab, a_log, dt_bias):
    m, w4 = ab.shape
    w = w4 // 2
    tm = SEG_TILE
    return pl.pallas_call(
        _gdn_gate_kernel,
        grid=(m // tm,),
        in_specs=[pl.BlockSpec((tm, w4), lambda i: (i, 0)),
                  pl.BlockSpec((1, w), lambda i: (0, 0)),
                  pl.BlockSpec((1, w), lambda i: (0, 0))],
        out_specs=[pl.BlockSpec((tm, w), lambda i: (i, 0)),
                   pl.BlockSpec((tm, w), lambda i: (i, 0))],
        out_shape=[jax.ShapeDtypeStruct((m, w), F32), jax.ShapeDtypeStruct((m, w), F32)],
        compiler_params=_cparams(("arbitrary",)),
        name="gdn_gates",
    )(ab, a_log.reshape(1, w).astype(F32), dt_bias.reshape(1, w).astype(F32))


def _bmm(a, b):
    return lax.dot_general(a, b, (((2,), (1,)), ((0,), (0,))), preferred_element_type=F32)


def _unit_tri_inverse(a, ri, ci):
    def same(s):
        return (ri // s) == (ci // s)

    t = (ri == ci).astype(F32) - jnp.where(same(2), a, 0.0)
    s = 2
    while s < GDN_CHUNK:
        coupling = jnp.logical_and(same(2 * s), jnp.logical_not(same(s)))
        tb = t.astype(BF16)
        tl = _bmm(tb, jnp.where(coupling, a, 0.0).astype(BF16))
        t = t - _bmm(tl.astype(BF16), tb)
        s *= 2
    return t


def _gdn_kernel(q_ref, k_ref, v_ref, col_ref, row_ref, *rest, kg, fwd):
    o_ref, state = rest[-2:]
    ch = GDN_CHUNK
    p2 = 2 * ch

    @pl.when(pl.program_id(1) == 0)
    def _():
        state[...] = jnp.zeros_like(state)

    ri = lax.broadcasted_iota(jnp.int32, (1, p2, p2), 1)
    ci = lax.broadcasted_iota(jnp.int32, (1, p2, p2), 2)
    same_head = (ri // ch) == (ci // ch)
    ahead = ci - ri if fwd else ri - ci
    incl = jnp.logical_and(same_head, ahead <= 0)
    strict = jnp.logical_and(incl, ci != ri)
    first_lanes = lax.broadcasted_iota(jnp.int32, (1, 1, p2), 2) < ch

    def pair_rows(ref):
        x = jnp.stack([ref[:, p * HEAD:(p + 1) * HEAD] for p in range(kg)]).astype(F32)
        return jnp.concatenate([x, x], axis=1)

    def head_rows(ref):
        return jnp.stack([jnp.concatenate([ref[:, (2 * p) * HEAD:(2 * p + 1) * HEAD],
                                           ref[:, (2 * p + 1) * HEAD:(2 * p + 2) * HEAD]], axis=0)
                          for p in range(kg)]).astype(F32)

    k2 = pair_rows(k_ref)
    q2 = pair_rows(q_ref)
    vf = head_rows(v_ref)
    kt = jnp.stack([k2[p].T for p in range(kg)])
    kkqk = _bmm(jnp.concatenate([k2, q2], axis=1).astype(BF16), kt.astype(BF16))
    kk, qk = kkqk[:, :p2], kkqk[:, p2:]

    def col(j0):
        c = jnp.stack([col_ref[:, j0 + p:j0 + p + 1] for p in range(kg)])
        return jnp.broadcast_to(c, (kg, p2, p2))

    gcol, bcol = col(0), col(kg)
    grow = jnp.stack([row_ref[p:p + 1, :] for p in range(kg)])
    last = ch - 1 if fwd else 0
    tot_a = grow[:, :, last:last + 1]
    tot_b = grow[:, :, ch + last:ch + last + 1]
    tot_row = jnp.where(first_lanes, tot_a, tot_b)

    decay = jnp.exp(jnp.where(incl, gcol - grow, -jnp.inf))
    tinv = _unit_tri_inverse(jnp.where(strict, bcol * kk * decay, 0.0), ri, ci)

    eg = jnp.exp(gcol)
    rhs = jnp.concatenate([vf * bcol, k2 * (bcol * eg)], axis=2).astype(BF16)
    uw = _bmm(tinv.astype(BF16), rhs)
    u, w_b = uw[:, :, :HEAD], uw[:, :, HEAD:].astype(BF16)
    qk_d = jnp.where(incl, qk * decay, 0.0).astype(BF16)
    q_dec = (q2 * eg).astype(BF16)
    kt_dec = kt * jnp.exp(tot_row - grow)

    s_a = state[:, 0]
    s_b = state[:, 1]
    ws_a = _bmm(jnp.concatenate([w_b[:, :ch], q_dec[:, :ch]], axis=1), s_a.astype(BF16))
    ws_b = _bmm(jnp.concatenate([w_b[:, ch:], q_dec[:, ch:]], axis=1), s_b.astype(BF16))
    v_new = (u - jnp.concatenate([ws_a[:, :ch], ws_b[:, :ch]], axis=1)).astype(BF16)
    o = jnp.concatenate([ws_a[:, ch:], ws_b[:, ch:]], axis=1) + _bmm(qk_d, v_new)
    kd_a = jnp.where(first_lanes, kt_dec, 0.0).astype(BF16)
    kd_b = jnp.where(first_lanes, 0.0, kt_dec).astype(BF16)
    state[:, 0] = s_a * jnp.exp(tot_a) + _bmm(kd_a, v_new)
    state[:, 1] = s_b * jnp.exp(tot_b) + _bmm(kd_b, v_new)
    if len(rest) > 2:
        oprev_ref, z_ref, gn_ref = rest[:3]
        o = _rms(o + head_rows(oprev_ref)) * gn_ref[...] * _silu(head_rows(z_ref))
    for p in range(kg):
        o_ref[:, (2 * p) * HEAD:(2 * p + 1) * HEAD] = o[p, :ch].astype(o_ref.dtype)
        o_ref[:, (2 * p + 1) * HEAD:(2 * p + 2) * HEAD] = o[p, ch:].astype(o_ref.dtype)


def _gdn_gate_layouts(gcs, beta, kg):
    m, w = gcs.shape
    ch = GDN_CHUNK
    assert HEAD == 2 * ch
    nch, ngrp = m // ch, w // (4 * kg)
    g6 = gcs.reshape(nch, ch, 2, ngrp, kg, 2)
    b6 = beta.reshape(nch, ch, 2, ngrp, kg, 2)
    to_cols = lambda t: t.transpose(2, 3, 0, 5, 1, 4).reshape(2, ngrp, nch, 2 * ch, kg)
    cols = jnp.concatenate([to_cols(g6), to_cols(b6)], axis=4)
    rows = g6.transpose(2, 3, 0, 4, 5, 1).reshape(2, ngrp, nch, kg, 2 * ch)
    return cols, rows


def _gdn_scan(qk, v, cols, rows, n_lat, kg, fwd, fused=None):
    m = v.shape[0]
    vh = v.shape[1] // HEAD
    ch = GDN_CHUNK
    nch = m // ch
    lat_ch = n_lat // ch
    ngrp = vh // (2 * kg)
    nv = 2 * kg
    d = 0 if fwd else 1

    def chunk(n):
        return (n + lat_ch) % nch if fwd else nch - 1 - n

    head_spec = pl.BlockSpec((ch, nv * HEAD), lambda g, n: (chunk(n), g))
    in_specs = [pl.BlockSpec((ch, kg * HEAD), lambda g, n: (chunk(n), g)),
                pl.BlockSpec((ch, kg * HEAD), lambda g, n: (chunk(n), ngrp + g)),
                head_spec,
                pl.BlockSpec((None, None, None, 2 * ch, 2 * kg), lambda g, n: (d, g, chunk(n), 0, 0)),
                pl.BlockSpec((None, None, None, kg, 2 * ch), lambda g, n: (d, g, chunk(n), 0, 0))]
    args = [qk, qk, v, cols, rows]
    if fused is not None:
        o_other, zsrc, z_col, g_norm = fused
        zcb = z_col // (nv * HEAD)
        in_specs += [head_spec,
                     pl.BlockSpec((ch, nv * HEAD), lambda g, n: (chunk(n), zcb + g)),
                     pl.BlockSpec((1, HEAD), lambda g, n: (0, 0))]
        args += [o_other, zsrc, g_norm.reshape(1, HEAD).astype(F32)]
    return pl.pallas_call(
        functools.partial(_gdn_kernel, kg=kg, fwd=fwd),
        grid=(ngrp, nch),
        in_specs=in_specs,
        out_specs=head_spec,
        out_shape=jax.ShapeDtypeStruct((m, vh * HEAD), BF16),
        scratch_shapes=[pltpu.VMEM((kg, 2, HEAD, HEAD), F32)],
        compiler_params=_cparams(("arbitrary", "arbitrary")),
        name="gdn_scan",
    )(*args)


def _gdn_mixer(h, layer, w_in_all, conv_w, a_log, dt_bias, g_norm, w_out_all, n_lat):
    m, d = h.shape
    khs = d // HEAD
    vhs = 2 * khs
    kw, vw = khs * HEAD, vhs * HEAD
    qkv_w = 2 * kw + vw
    n_ab = 4 * vhs
    if n_ab % 256 == 0 and qkv_w % n_ab == 0:
        qkvz = _matmul(h, w_in_all, tn=n_ab, w_cols=(layer, 0, qkv_w + vw, (qkv_w, n_ab)))
        ab = _matmul(h, w_in_all, out_dtype=F32, tn=n_ab, w_cols=(layer, qkv_w, n_ab, None))
    else:
        w_in = w_in_all[layer]
        w_big = jnp.concatenate([w_in[:, :qkv_w], w_in[:, qkv_w + n_ab:]], axis=1).astype(BF16)
        qkvz = _matmul(h, w_big)
        ab = _matmul(h, w_in[:, qkv_w:qkv_w + n_ab].astype(BF16), out_dtype=F32, tn=n_ab)
    qk = _gdn_conv(qkvz, conv_w, 0, 2 * kw, True, kw, n_lat)
    v = _gdn_conv(qkvz, conv_w, 2 * kw, vw, False, 0, n_lat)
    gcs, beta = _gdn_gates(ab, a_log, dt_bias)
    kg = _pick(khs, (GDN_PAIRS, 4, 2, 1))
    cols, rows = _gdn_gate_layouts(gcs, beta, kg)
    o_fwd = _gdn_scan(qk, v, cols, rows, n_lat, kg, True)
    y = _gdn_scan(qk, v, cols, rows, n_lat, kg, False, fused=(o_fwd, qkvz, qkv_w, g_norm))
    return _matmul(y, w_out_all, w_cols=(layer, 0, d, None))


def _pool_kernel(prev_ref, cur_ref, next_ref, w_ref, sc_ref, z_ref, o_ref, scr, *,
                 n_lat_blocks, n_lat, n_ctx):
    tm, tc = cur_ref.shape
    _fill_window(scr, prev_ref, cur_ref, next_ref, n_lat_blocks)
    t, seg_len = _seg_position(tm, n_lat_blocks, n_lat, n_ctx)
    for g, window in enumerate(POOL_WINDOWS):
        @pl.when(pl.program_id(1) == g)
        def _(radius=window // 2):
            u = scr[HALO:HALO + tm, :]
            acc = u
            for dlt in range(1, radius + 1):
                acc = acc + (scr[HALO - dlt:HALO - dlt + tm, :] + scr[HALO + dlt:HALO + dlt + tm, :])
            cnt = jnp.minimum(t + radius + 1, seg_len) - jnp.maximum(t - radius, 0)
            mean_minus = acc / cnt.astype(F32) - u
            y = _dot(mean_minus.astype(BF16), w_ref[...]) * sc_ref[...]
            o_ref[...] = (y * _silu(z_ref[...].astype(F32))).astype(o_ref.dtype)


def _pool_mixer(h, layer, w_in_all, w_grp, scale, w_out_all, n_lat):
    m, d = h.shape
    ng, gw, _ = w_grp.shape
    assert tuple(w // 2 for w in POOL_WINDOWS) == tuple(2 ** g for g in range(ng))
    uz = _matmul(h, w_in_all, w_cols=(layer, 0, 2 * d, None))
    tm = SEG_TILE
    specs = _window_specs(tm, gw, m, lambda g: g)
    specs += [pl.BlockSpec((None, gw, gw), lambda i, g: (g, 0, 0)),
              pl.BlockSpec((1, gw), lambda i, g: (0, g)),
              pl.BlockSpec((tm, gw), lambda i, g: (i, ng + g))]
    yg = pl.pallas_call(
        functools.partial(_pool_kernel, n_lat_blocks=n_lat // tm, n_lat=n_lat, n_ctx=m - n_lat),
        grid=(m // tm, ng),
        in_specs=specs,
        out_specs=pl.BlockSpec((tm, gw), lambda i, g: (i, g)),
        out_shape=jax.ShapeDtypeStruct((m, d), BF16),
        scratch_shapes=[pltpu.VMEM((tm + 2 * HALO, gw), F32)],
        compiler_params=_cparams(("arbitrary", "arbitrary")),
        name="pool_group",
    )(uz, uz, uz, w_grp.astype(BF16), scale.reshape(1, d).astype(F32), uz)
    return _matmul(yg, w_out_all, w_cols=(layer, 0, d, None))


def kernel(x, c, ctx, c_ctx, ada_w, ada_b, norm_pre, norm_post, mla_w_in, mla_g_q, mla_w_q_up,
           mla_g_kv, mla_w_kv_up, mla_w_out, gdn_w_in, gdn_conv_w, gdn_a_log, gdn_dt_bias,
           gdn_g_norm, gdn_w_out, pool_w_in, pool_w_grp, pool_scale, pool_w_out):
    assert x.shape[0] == 1 and ctx.shape[0] == 1
    n_lat, d = x.shape[1], x.shape[2]
    n_ctx = ctx.shape[1]
    depth = ada_w.shape[0]
    n_mixers = 3
    assert n_lat % SEG_TILE == 0 and n_ctx % SEG_TILE == 0

    cond8 = jnp.concatenate([c.reshape(1, d), c_ctx.reshape(1, d), jnp.zeros((6, d), F32)], axis=0)
    mods = _adaln(cond8, ada_w, ada_b)
    tables = _rope_tables(n_lat, n_ctx)
    xs, h = _prenorm(x[0], ctx[0], norm_pre[0], mods[0])

    counts = [0, 0, 0]
    for i in range(depth):
        kind = i % n_mixers
        j = counts[kind]
        counts[kind] += 1
        need_ctx = i < depth - 1
        if kind == 0:
            y = _mla_mixer(h, tables, j, mla_w_in, mla_g_q[j], mla_w_q_up[j], mla_g_kv[j],
                           mla_w_kv_up[j], mla_w_out, n_lat, need_ctx)
        elif kind == 1:
            y = _gdn_mixer(h, j, gdn_w_in, gdn_conv_w[j], gdn_a_log[j], gdn_dt_bias[j],
                           gdn_g_norm[j], gdn_w_out, n_lat)
        else:
            y = _pool_mixer(h, j, pool_w_in, pool_w_grp[j], pool_scale[j], pool_w_out, n_lat)
        if need_ctx:
            xs, h = _residual(xs, y, norm_post[i], mods[i], n_lat, xs.shape[0],
                              norm_pre[i + 1], mods[i + 1])
        else:
            xs = _residual(xs, y, norm_post[i], mods[i], n_lat, n_lat)
    return xs[None]
```

```python
import functools
import math

import jax
import jax.numpy as jnp
from jax import lax
from jax.experimental import pallas as pl
from jax.experimental.pallas import tpu as pltpu

F32 = jnp.float32
BF16 = jnp.bfloat16
EPS = 1e-6

HEAD = 128
MLA_ROPE = 64
MLA_KV_RANK = 512
MLA_QK_PAD = 256
ROPE_BASE = 10000.0
GRID_W = 64
GDN_CONV = 5
GDN_CHUNK = 64
GDN_PAIRS = 16
POOL_WINDOWS = (2, 4, 8, 16)
SEG_TILE = 256
HALO = 16
VMEM_LIMIT = 56 * 1024 * 1024
MM_MAX_TK = 8192
MM_W_TILE_ELEMS = 2 * 1024 * 1024
MM_A_TILE_ELEMS = 6 * 1024 * 1024


def _pick(n, prefs):
    for p in prefs:
        if n % p == 0:
            return p
    return n


def _cparams(sem):
    return pltpu.CompilerParams(dimension_semantics=sem, vmem_limit_bytes=VMEM_LIMIT)


def _dot(a, b):
    return jnp.dot(a, b, preferred_element_type=F32)


def _dot_nt(a, b):
    return lax.dot_general(a, b, (((1,), (1,)), ((), ())), preferred_element_type=F32)


def _silu(x):
    return x * jax.nn.sigmoid(x)


def _adaln_kernel(c_ref, w_ref, b_ref, o_ref):
    a = _silu(c_ref[...]).astype(BF16)
    o_ref[...] = _dot(a, w_ref[...].astype(BF16)) + b_ref[...]


def _adaln(cond8, ada_w, ada_b):
    depth, d, n = ada_w.shape
    tn = _pick(n, (512, 256, 128))
    return pl.pallas_call(
        _adaln_kernel,
        grid=(depth, n // tn),
        in_specs=[pl.BlockSpec((8, d), lambda l, j: (0, 0)),
                  pl.BlockSpec((None, d, tn), lambda l, j: (l, 0, j)),
                  pl.BlockSpec((None, 1, tn), lambda l, j: (l, 0, j))],
        out_specs=pl.BlockSpec((None, 8, tn), lambda l, j: (l, 0, j)),
        out_shape=jax.ShapeDtypeStruct((depth, 8, n), F32),
        compiler_params=_cparams(("arbitrary", "arbitrary")),
        name="adaln",
    )(cond8, ada_w, ada_b.reshape(depth, 1, n))


def _rms(xf):
    return xf * lax.rsqrt(jnp.mean(xf * xf, axis=-1, keepdims=True) + EPS)


def _prenorm_kernel(xl_ref, xc_ref, g_ref, mod_ref, xs_ref, h_ref, *, n_lat_blocks, d):
    is_ctx = pl.program_id(0) >= n_lat_blocks
    x = jnp.where(is_ctx, xc_ref[...], xl_ref[...])
    mod = jnp.where(is_ctx, mod_ref[1:2, :], mod_ref[0:1, :])
    sh, sc = mod[:, :d], mod[:, d:2 * d]
    y = _rms(x) * g_ref[...]
    xs_ref[...] = x
    h_ref[...] = (y * (1.0 + sc) + sh).astype(h_ref.dtype)


def _prenorm(x_lat, x_ctx, g, mod):
    n_lat, d = x_lat.shape
    m = n_lat + x_ctx.shape[0]
    tm = SEG_TILE
    nlb = n_lat // tm
    row_spec = pl.BlockSpec((tm, d), lambda i: (i, 0))
    return pl.pallas_call(
        functools.partial(_prenorm_kernel, n_lat_blocks=nlb, d=d),
        grid=(m // tm,),
        in_specs=[pl.BlockSpec((tm, d), lambda i: (jnp.minimum(i, nlb - 1), 0)),
                  pl.BlockSpec((tm, d), lambda i: (jnp.maximum(i - nlb, 0), 0)),
                  pl.BlockSpec((1, d), lambda i: (0, 0)),
                  pl.BlockSpec((8, 3 * d), lambda i: (0, 0))],
        out_specs=[row_spec, row_spec],
        out_shape=[jax.ShapeDtypeStruct((m, d), F32), jax.ShapeDtypeStruct((m, d), BF16)],
        compiler_params=_cparams(("arbitrary",)),
        name="prenorm",
    )(x_lat, x_ctx, g.reshape(1, d), mod)


def _residual_kernel(x_ref, y_ref, g_ref, mod_ref, *rest, n_lat_blocks, d, with_next):
    is_ctx = pl.program_id(0) >= n_lat_blocks
    mod = jnp.where(is_ctx, mod_ref[1:2, :], mod_ref[0:1, :])
    gt = mod[:, 2 * d:]
    y = _rms(y_ref[...].astype(F32)) * g_ref[...]
    x_new = x_ref[...] + gt * y
    if not with_next:
        rest[0][...] = x_new
        return
    gn_ref, modn_ref, o_ref, h_ref = rest
    o_ref[...] = x_new
    modn = jnp.where(is_ctx, modn_ref[1:2, :], modn_ref[0:1, :])
    h = _rms(x_new) * gn_ref[...]
    h_ref[...] = (h * (1.0 + modn[:, d:2 * d]) + modn[:, :d]).astype(h_ref.dtype)


def _residual(x, y, g, mod, n_lat, rows, g_next=None, mod_next=None):
    d = x.shape[1]
    tm = SEG_TILE
    with_next = g_next is not None
    row_spec = pl.BlockSpec((tm, d), lambda i: (i, 0))
    vec_spec = pl.BlockSpec((1, d), lambda i: (0, 0))
    mod_spec = pl.BlockSpec((8, 3 * d), lambda i: (0, 0))
    in_specs = [row_spec, row_spec, vec_spec, mod_spec]
    args = [x, y, g.reshape(1, d), mod]
    out_specs, out_shape = row_spec, jax.ShapeDtypeStruct((rows, d), F32)
    if with_next:
        in_specs += [vec_spec, mod_spec]
        args += [g_next.reshape(1, d), mod_next]
        out_specs = [row_spec, row_spec]
        out_shape = [out_shape, jax.ShapeDtypeStruct((rows, d), BF16)]
    return pl.pallas_call(
        functools.partial(_residual_kernel, n_lat_blocks=n_lat // tm, d=d, with_next=with_next),
        grid=(rows // tm,),
        in_specs=in_specs,
        out_specs=out_specs,
        out_shape=out_shape,
        compiler_params=_cparams(("arbitrary",)),
        name="residual",
    )(*args)


def _mm_kernel(a_ref, w_ref, *rest, nk, has_gain, epilogue, n_extra):
    pos = 0
    g_ref = rest[pos] if has_gain else None
    pos += int(has_gain)
    extra = rest[pos:pos + n_extra]
    pos += n_extra
    o_ref = rest[pos]
    acc_ref = rest[pos + 1] if nk > 1 else None

    if has_gain:
        an_ref = rest[pos + 1]

        @pl.when(pl.program_id(1) == 0)
        def _():
            an_ref[...] = (_rms(a_ref[...].astype(F32)) * g_ref[...]).astype(BF16)

        a = an_ref[...]
    else:
        a = a_ref[...]
    r = _dot(a, w_ref[...].astype(BF16))

    def finish(res):
        if epilogue is not None:
            res = epilogue(res, *[e[...] for e in extra])
        o_ref[...] = res.astype(o_ref.dtype)

    if nk == 1:
        finish(r)
    else:
        k = pl.program_id(2)

        @pl.when(k == 0)
        def _():
            acc_ref[...] = r

        @pl.when(k > 0)
        def _():
            acc_ref[...] += r

        @pl.when(k == nk - 1)
        def _():
            finish(acc_ref[...])


def _matmul(a, w, *, rows=None, a_col=0, k=None, out_dtype=BF16, gain=None,
            epilogue=None, extras=(), tn=None, tm=None, out_widen=1, w_cols=None):
    if w_cols is not None:
        layer, col0, n, skip = w_cols
        k = w.shape[1]
    else:
        k = w.shape[0] if k is None else k
        n = w.shape[1]
    rows = a.shape[0] if rows is None else rows
    tk = k if k <= MM_MAX_TK else _pick(k, (4096, 2048, 1024, 512))
    nk = k // tk
    tm = tm or _pick(rows, [t for t in (1408, 1024, 768, 512, 256, 128, 64, 32, 16)
                            if t * tk <= MM_A_TILE_ELEMS] or [8])
    tn = tn or _pick(n, [t for t in (1024, 512, 256, 128) if tk * t <= MM_W_TILE_ELEMS] or [128])
    assert a_col % tk == 0 and (gain is None or nk == 1)
    a_cb = a_col // tk
    if w_cols is None:
        w_spec = pl.BlockSpec((tk, tn), lambda i, j, kk: (kk, j))
    else:
        skip_at, skip_n = (0, 0) if skip is None else skip
        assert col0 % tn == 0 and skip_at % tn == 0 and skip_n % tn == 0

        def w_block(j):
            jb = col0 // tn + j
            return jb + jnp.where(jb >= skip_at // tn, skip_n // tn, 0) if skip_n else jb

        w_spec = pl.BlockSpec((None, tk, tn), lambda i, j, kk: (layer, kk, w_block(j)))
    in_specs = [pl.BlockSpec((tm, tk), lambda i, j, kk: (i, a_cb + kk)), w_spec]
    args = [a, w]
    if gain is not None:
        in_specs.append(pl.BlockSpec((1, tk), lambda i, j, kk: (0, 0)))
        args.append(gain.reshape(1, tk).astype(F32))
    for arr, cols, cb in extras:
        in_specs.append(pl.BlockSpec((tm, cols), lambda i, j, kk, cb=cb: (i, cb(j))))
        args.append(arr)
    scratch = [pltpu.VMEM((tm, tn), F32)] if nk > 1 else []
    if gain is not None:
        scratch = [pltpu.VMEM((tm, tk), BF16)]
    return pl.pallas_call(
        functools.partial(_mm_kernel, nk=nk, has_gain=gain is not None,
                          epilogue=epilogue, n_extra=len(extras)),
        grid=(rows // tm, n // tn, nk),
        in_specs=in_specs,
        out_specs=pl.BlockSpec((tm, tn * out_widen), lambda i, j, kk: (i, j)),
        out_shape=jax.ShapeDtypeStruct((rows, n * out_widen), out_dtype),
        scratch_shapes=scratch,
        compiler_params=_cparams(("arbitrary", "arbitrary", "arbitrary")),
        name="matmul",
    )(*args)


def _rope_tables(n_lat, n_ctx):
    rows = n_lat // GRID_W
    row = jnp.repeat(jnp.arange(rows, dtype=F32), GRID_W)
    col = jnp.tile(jnp.arange(GRID_W, dtype=F32), rows)
    n_freq = MLA_ROPE // 4
    inv_freq = ROPE_BASE ** (-jnp.arange(n_freq, dtype=F32) / n_freq)
    ang = jnp.concatenate([row[:, None] * inv_freq, col[:, None] * inv_freq], axis=-1)
    ang = jnp.concatenate([ang, jnp.zeros((n_ctx, ang.shape[1]), F32)], axis=0)
    cos, sin = jnp.cos(ang), jnp.sin(ang)
    m = ang.shape[0]
    half = MLA_ROPE // 2
    ones = jnp.ones((m, HEAD), F32)
    z_h = jnp.zeros((m, HEAD), F32)
    z_p = jnp.zeros((m, MLA_QK_PAD - HEAD - MLA_ROPE), F32)
    z_r = jnp.zeros((m, half), F32)
    c = jnp.concatenate([ones, cos, cos, z_p], axis=1)
    s1 = jnp.concatenate([z_h, -sin, z_r, z_p], axis=1)
    s2 = jnp.concatenate([z_h, z_r, sin, z_p], axis=1)
    return c, s1, s2


def _rope_heads(t, c, s1, s2):
    n = t.shape[1]
    reps = n // MLA_QK_PAD
    half = MLA_ROPE // 2
    if reps > 1:
        c, s1, s2 = (jnp.concatenate([u] * reps, axis=1) for u in (c, s1, s2))
    return t * c + pltpu.roll(t, n - half, 1) * s1 + pltpu.roll(t, half, 1) * s2


def _k_epilogue(t, kr):
    krf = kr.astype(F32)
    parts = []
    for hh in range(t.shape[1] // HEAD):
        parts += [t[:, hh * HEAD:(hh + 1) * HEAD], krf]
    return jnp.concatenate(parts, axis=1)


def _ckv_epilogue(t, c, s1, s2):
    r = MLA_KV_RANK
    return jnp.concatenate([t[:, :r], _rope_heads(t[:, r:], c, s1, s2)], axis=1)


def _flash_kernel(q_ref, k_ref, v_ref, z_ref, o_ref, *, tk, nk):
    q = q_ref[...]
    tq = q.shape[0]
    ones_col = (lax.broadcasted_iota(jnp.int32, (tk, HEAD), 1) == 0).astype(BF16)
    m = jnp.full((tq, 1), -jnp.inf, F32)
    acc = jnp.zeros((tq, 2 * HEAD), F32)
    s = _dot_nt(q, k_ref[0:tk, :])
    for c in range(nk):
        s_next = _dot_nt(q, k_ref[(c + 1) * tk:(c + 2) * tk, :]) if c + 1 < nk else None
        m_new = jnp.maximum(m, jnp.max(s, axis=-1, keepdims=True))
        p = jnp.exp2(s - m_new)
        alpha = jnp.exp2(m - m_new)
        v_ext = jnp.concatenate([v_ref[c * tk:(c + 1) * tk, :], ones_col], axis=1)
        acc = alpha * acc + _dot(p.astype(BF16), v_ext)
        m, s = m_new, s_next
    o = acc[:, :HEAD] / acc[:, HEAD:HEAD + 1]
    o_ref[...] = (o * _silu(z_ref[...].astype(F32))).astype(o_ref.dtype)


def _flash(q, k, v, zsrc, z_col, og_prev, *, q_row0, n_q, k_row0, n_k, total_rows):
    heads = v.shape[1] // HEAD
    tq = _pick(n_q, (1024, 512, 256, 128))
    tk = _pick(n_k, (2816, 768, 512, 256, 128))
    assert q_row0 % tq == 0 and k_row0 % n_k == 0
    qb0, kb0, zcb = q_row0 // tq, k_row0 // n_k, z_col // HEAD
    in_specs = [pl.BlockSpec((tq, MLA_QK_PAD), lambda h, i: (qb0 + i, h)),
                pl.BlockSpec((n_k, MLA_QK_PAD), lambda h, i: (kb0, h)),
                pl.BlockSpec((n_k, HEAD), lambda h, i: (kb0, h)),
                pl.BlockSpec((tq, HEAD), lambda h, i: (qb0 + i, zcb + h))]
    args = [q, k, v, zsrc]
    aliases = {}
    if og_prev is not None:
        in_specs.append(pl.BlockSpec(memory_space=pl.ANY))
        args.append(og_prev)
        aliases = {4: 0}

    def kern(q_ref, k_ref, v_ref, z_ref, *rest):
        _flash_kernel(q_ref, k_ref, v_ref, z_ref, rest[-1], tk=tk, nk=n_k // tk)

    return pl.pallas_call(
        kern,
        grid=(heads, n_q // tq),
        in_specs=in_specs,
        out_specs=pl.BlockSpec((tq, HEAD), lambda h, i: (qb0 + i, h)),
        out_shape=jax.ShapeDtypeStruct((total_rows, heads * HEAD), BF16),
        input_output_aliases=aliases,
        compiler_params=_cparams(("arbitrary", "arbitrary")),
        name="mla_flash",
    )(*args)


def _mla_weights(w_in, w_q_up, w_kv_up):
    d = w_in.shape[0]
    q_rank = d // 4
    heads = d // HEAD
    r0, r1, r2 = q_rank, q_rank + MLA_KV_RANK, q_rank + MLA_KV_RANK + MLA_ROPE
    pad = MLA_QK_PAD - HEAD - MLA_ROPE
    scale = (HEAD + MLA_ROPE) ** -0.5 * math.log2(math.e)
    w_z = w_in[:, r2:].astype(BF16)
    w_ckv = jnp.concatenate([w_in[:, r0:r1], jnp.zeros((d, HEAD), F32), w_in[:, r1:r2],
                             jnp.zeros((d, pad), F32)], axis=1).astype(BF16)
    wq = (w_q_up * scale).reshape(q_rank, heads, HEAD + MLA_ROPE)
    wq = jnp.concatenate([wq, jnp.zeros((q_rank, heads, pad), F32)], axis=2)
    wq = wq.reshape(q_rank, heads * MLA_QK_PAD).astype(BF16)
    wkv = w_kv_up.reshape(MLA_KV_RANK, heads, 2 * HEAD)
    wk = wkv[:, :, :HEAD].reshape(MLA_KV_RANK, heads * HEAD).astype(BF16)
    wv = wkv[:, :, HEAD:].reshape(MLA_KV_RANK, heads * HEAD).astype(BF16)
    return w_z, w_ckv, wq, wk, wv


def _mla_mixer(h, tables, layer, w_in_all, g_q, w_q_up, g_kv, w_kv_up, w_out_all, n_lat, need_ctx):
    m, d = h.shape
    n_ctx = m - n_lat
    q_rank = d // 4
    c, s1, s2 = tables
    w_z, w_ckv, wq, wk, wv = _mla_weights(w_in_all[layer], w_q_up, w_kv_up)
    first = lambda j: 0

    cq = _matmul(h, w_in_all[layer][:, :q_rank].astype(BF16))
    z = _matmul(h, w_z)
    ckvkr = _matmul(h, w_ckv, epilogue=_ckv_epilogue, tn=w_ckv.shape[1],
                    extras=[(c, MLA_QK_PAD, first), (s1, MLA_QK_PAD, first),
                            (s2, MLA_QK_PAD, first)])
    q = _matmul(cq, wq, gain=g_q, epilogue=_rope_heads,
                extras=[(c, MLA_QK_PAD, first), (s1, MLA_QK_PAD, first),
                        (s2, MLA_QK_PAD, first)])
    kr_cb = (MLA_KV_RANK + HEAD) // HEAD
    kk = _matmul(ckvkr, wk, k=MLA_KV_RANK, gain=g_kv, epilogue=_k_epilogue,
                 extras=[(ckvkr, HEAD, lambda j: kr_cb)], out_widen=MLA_QK_PAD // HEAD)
    v = _matmul(ckvkr, wv, k=MLA_KV_RANK, gain=g_kv)

    og = _flash(q, kk, v, z, 0, None, q_row0=0, n_q=n_lat, k_row0=0, n_k=m, total_rows=m)
    if need_ctx:
        og = _flash(q, kk, v, z, 0, og, q_row0=n_lat, n_q=n_ctx, k_row0=n_lat, n_k=n_ctx,
                    total_rows=m)
    rows = m if need_ctx else n_lat
    return _matmul(og, w_out_all, rows=rows, w_cols=(layer, 0, d, None))


def _fill_window(scr, prev_ref, cur_ref, next_ref, n_lat_blocks):
    tm = cur_ref.shape[0]
    i = pl.program_id(0)
    first = jnp.logical_or(i == 0, i == n_lat_blocks)
    last = jnp.logical_or(i == n_lat_blocks - 1, i == pl.num_programs(0) - 1)
    scr[0:HALO, :] = jnp.where(first, 0.0, prev_ref[...].astype(F32))
    scr[HALO:HALO + tm, :] = cur_ref[...].astype(F32)
    scr[HALO + tm:, :] = jnp.where(last, 0.0, next_ref[...].astype(F32))


def _seg_position(tm, n_lat_blocks, n_lat, n_ctx):
    i = pl.program_id(0)
    is_ctx = i >= n_lat_blocks
    local0 = jnp.where(is_ctx, i - n_lat_blocks, i) * tm
    t = local0 + lax.broadcasted_iota(jnp.int32, (tm, 1), 0)
    return t, jnp.where(is_ctx, n_ctx, n_lat)


def _window_specs(tm, cols, n_rows, col_fn):
    hb = tm // HALO
    last = n_rows // HALO - 1
    return [pl.BlockSpec((HALO, cols), lambda i, j: (jnp.maximum(i * hb - 1, 0), col_fn(j))),
            pl.BlockSpec((tm, cols), lambda i, j: (i, col_fn(j))),
            pl.BlockSpec((HALO, cols), lambda i, j: (jnp.minimum((i + 1) * hb, last), col_fn(j)))]


def _gdn_conv_kernel(prev_ref, cur_ref, next_ref, w_ref, o_ref, scr, *, normalize,
                     n_scaled_blocks, n_lat_blocks):
    tm, tc = cur_ref.shape
    _fill_window(scr, prev_ref, cur_ref, next_ref, n_lat_blocks)
    r = GDN_CONV // 2
    acc = scr[HALO - r:HALO - r + tm, :] * w_ref[0:1, :]
    for j in range(1, GDN_CONV):
        acc = acc + scr[HALO + j - r:HALO + j - r + tm, :] * w_ref[j:j + 1, :]
    y = _silu(acc)
    if normalize:
        scale = jnp.where(pl.program_id(1) < n_scaled_blocks, HEAD ** -0.5, 1.0)
        parts = []
        for hh in range(tc // HEAD):
            u = y[:, hh * HEAD:(hh + 1) * HEAD]
            parts.append(u * (lax.rsqrt(jnp.sum(u * u, axis=-1, keepdims=True) + EPS) * scale))
        y = jnp.concatenate(parts, axis=1) if len(parts) > 1 else parts[0]
    o_ref[...] = y.astype(o_ref.dtype)


def _gdn_conv(src, conv_w, col0, ncols, normalize, n_scaled_cols, n_lat):
    m = src.shape[0]
    tm = SEG_TILE
    tc = _pick(ncols, (2048, 1024, 512, 256, 128))
    cb0 = col0 // tc
    specs = _window_specs(tm, tc, m, lambda j: cb0 + j)
    specs.append(pl.BlockSpec((8, tc), lambda i, j: (0, cb0 + j)))
    w8 = jnp.concatenate([conv_w, jnp.zeros((8 - GDN_CONV, conv_w.shape[1]), F32)], axis=0)
    return pl.pallas_call(
        functools.partial(_gdn_conv_kernel, normalize=normalize,
                          n_scaled_blocks=n_scaled_cols // tc, n_lat_blocks=n_lat // tm),
        grid=(m // tm, ncols // tc),
        in_specs=specs,
        out_specs=pl.BlockSpec((tm, tc), lambda i, j: (i, j)),
        out_shape=jax.ShapeDtypeStruct((m, ncols), BF16),
        scratch_shapes=[pltpu.VMEM((tm + 2 * HALO, tc), F32)],
        compiler_params=_cparams(("arbitrary", "arbitrary")),
        name="gdn_conv",
    )(src, src, src, w8)


def _split3(x):
    x1 = x.astype(BF16)
    r1 = x - x1.astype(F32)
    x2 = r1.astype(BF16)
    x3 = (r1 - x2.astype(F32)).astype(BF16)
    return x1, x2, x3


def _gdn_gate_kernel(ab_ref, alog_ref, dtb_ref, g_ref, beta_ref):
    tm, w = g_ref.shape
    ab = ab_ref[...]
    x = ab[:, :w] + dtb_ref[...]
    softplus = jnp.maximum(x, 0.0) + jnp.log1p(jnp.exp(-jnp.abs(x)))
    g = -jnp.exp(alog_ref[...]) * softplus
    beta_ref[...] = jax.nn.sigmoid(ab[:, w:])
    ch = GDN_CHUNK
    ri = lax.broadcasted_iota(jnp.int32, (ch, ch), 0)
    ci = lax.broadcasted_iota(jnp.int32, (ch, ch), 1)
    lower = (ci <= ri).astype(BF16)
    upper = (ci >= ri).astype(BF16)
    fwd_lane = lax.broadcasted_iota(jnp.int32, (ch, w), 1) < w // 2
    for cidx in range(tm // ch):
        parts = _split3(g[cidx * ch:(cidx + 1) * ch, :])
        pre = _dot(lower, parts[0]) + (_dot(lower, parts[1]) + _dot(lower, parts[2]))
        suf = _dot(upper, parts[0]) + (_dot(upper, parts[1]) + _dot(upper, parts[2]))
        g_ref[cidx * ch:(cidx + 1) * ch, :] = jnp.where(fwd_lane, pre, suf)


def _gdn_gates(ab, a_log, dt_bias):
    m, w4 = ab.shape
    w = w4 // 2
    tm = SEG_TILE
    return pl.pallas_call(
        _gdn_gate_kernel,
        grid=(m // tm,),
        in_specs=[pl.BlockSpec((tm, w4), lambda i: (i, 0)),
                  pl.BlockSpec((1, w), lambda i: (0, 0)),
                  pl.BlockSpec((1, w), lambda i: (0, 0))],
        out_specs=[pl.BlockSpec((tm, w), lambda i: (i, 0)),
                   pl.BlockSpec((tm, w), lambda i: (i, 0))],
        out_shape=[jax.ShapeDtypeStruct((m, w), F32), jax.ShapeDtypeStruct((m, w), F32)],
        compiler_params=_cparams(("arbitrary",)),
        name="gdn_gates",
    )(ab, a_log.reshape(1, w).astype(F32), dt_bias.reshape(1, w).astype(F32))


def _bmm(a, b):
    return lax.dot_general(a, b, (((2,), (1,)), ((0,), (0,))), preferred_element_type=F32)


def _unit_tri_inverse(a, ri, ci):
    def same(s):
        return (ri // s) == (ci // s)

    t = (ri == ci).astype(F32) - jnp.where(same(2), a, 0.0)
    s = 2
    while s < GDN_CHUNK:
        coupling = jnp.logical_and(same(2 * s), jnp.logical_not(same(s)))
        tb = t.astype(BF16)
        tl = _bmm(tb, jnp.where(coupling, a, 0.0).astype(BF16))
        t = t - _bmm(tl.astype(BF16), tb)
        s *= 2
    return t


def _gdn_kernel(q_ref, k_ref, v_ref, col_ref, row_ref, *rest, kg, fwd):
    o_ref, state = rest[-2:]
    ch = GDN_CHUNK
    p2 = 2 * ch

    @pl.when(pl.program_id(1) == 0)
    def _():
        state[...] = jnp.zeros_like(state)

    ri = lax.broadcasted_iota(jnp.int32, (1, p2, p2), 1)
    ci = lax.broadcasted_iota(jnp.int32, (1, p2, p2), 2)
    same_head = (ri // ch) == (ci // ch)
    ahead = ci - ri if fwd else ri - ci
    incl = jnp.logical_and(same_head, ahead <= 0)
    strict = jnp.logical_and(incl, ci != ri)
    first_lanes = lax.broadcasted_iota(jnp.int32, (1, 1, p2), 2) < ch

    def pair_rows(ref):
        x = jnp.stack([ref[:, p * HEAD:(p + 1) * HEAD] for p in range(kg)]).astype(F32)
        return jnp.concatenate([x, x], axis=1)

    def head_rows(ref):
        return jnp.stack([jnp.concatenate([ref[:, (2 * p) * HEAD:(2 * p + 1) * HEAD],
                                           ref[:, (2 * p + 1) * HEAD:(2 * p + 2) * HEAD]], axis=0)
                          for p in range(kg)]).astype(F32)

    k2 = pair_rows(k_ref)
    q2 = pair_rows(q_ref)
    vf = head_rows(v_ref)
    kt = jnp.stack([k2[p].T for p in range(kg)])
    kkqk = _bmm(jnp.concatenate([k2, q2], axis=1).astype(BF16), kt.astype(BF16))
    kk, qk = kkqk[:, :p2], kkqk[:, p2:]

    def col(j0):
        c = jnp.stack([col_ref[:, j0 + p:j0 + p + 1] for p in range(kg)])
        return jnp.broadcast_to(c, (kg, p2, p2))

    gcol, bcol = col(0), col(kg)
    grow = jnp.stack([row_ref[p:p + 1, :] for p in range(kg)])
    last = ch - 1 if fwd else 0
    tot_a = grow[:, :, last:last + 1]
    tot_b = grow[:, :, ch + last:ch + last + 1]
    tot_row = jnp.where(first_lanes, tot_a, tot_b)

    decay = jnp.exp(jnp.where(incl, gcol - grow, -jnp.inf))
    tinv = _unit_tri_inverse(jnp.where(strict, bcol * kk * decay, 0.0), ri, ci)

    eg = jnp.exp(gcol)
    rhs = jnp.concatenate([vf * bcol, k2 * (bcol * eg)], axis=2).astype(BF16)
    uw = _bmm(tinv.astype(BF16), rhs)
    u, w_b = uw[:, :, :HEAD], uw[:, :, HEAD:].astype(BF16)
    qk_d = jnp.where(incl, qk * decay, 0.0).astype(BF16)
    q_dec = (q2 * eg).astype(BF16)
    kt_dec = kt * jnp.exp(tot_row - grow)

    s_a = state[:, 0]
    s_b = state[:, 1]
    ws_a = _bmm(jnp.concatenate([w_b[:, :ch], q_dec[:, :ch]], axis=1), s_a.astype(BF16))
    ws_b = _bmm(jnp.concatenate([w_b[:, ch:], q_dec[:, ch:]], axis=1), s_b.astype(BF16))
    v_new = (u - jnp.concatenate([ws_a[:, :ch], ws_b[:, :ch]], axis=1)).astype(BF16)
    o = jnp.concatenate([ws_a[:, ch:], ws_b[:, ch:]], axis=1) + _bmm(qk_d, v_new)
    kd_a = jnp.where(first_lanes, kt_dec, 0.0).astype(BF16)
    kd_b = jnp.where(first_lanes, 0.0, kt_dec).astype(BF16)
    state[:, 0] = s_a * jnp.exp(tot_a) + _bmm(kd_a, v_new)
    state[:, 1] = s_b * jnp.exp(tot_b) + _bmm(kd_b, v_new)
    if len(rest) > 2:
        oprev_ref, z_ref, gn_ref = rest[:3]
        o = _rms(o + head_rows(oprev_ref)) * gn_ref[...] * _silu(head_rows(z_ref))
    for p in range(kg):
        o_ref[:, (2 * p) * HEAD:(2 * p + 1) * HEAD] = o[p, :ch].astype(o_ref.dtype)
        o_ref[:, (2 * p + 1) * HEAD:(2 * p + 2) * HEAD] = o[p, ch:].astype(o_ref.dtype)


def _gdn_gate_layouts(gcs, beta, kg):
    m, w = gcs.shape
    ch = GDN_CHUNK
    assert HEAD == 2 * ch
    nch, ngrp = m // ch, w // (4 * kg)
    g6 = gcs.reshape(nch, ch, 2, ngrp, kg, 2)
    b6 = beta.reshape(nch, ch, 2, ngrp, kg, 2)
    to_cols = lambda t: t.transpose(2, 3, 0, 5, 1, 4).reshape(2, ngrp, nch, 2 * ch, kg)
    cols = jnp.concatenate([to_cols(g6), to_cols(b6)], axis=4)
    rows = g6.transpose(2, 3, 0, 4, 5, 1).reshape(2, ngrp, nch, kg, 2 * ch)
    return cols, rows


def _gdn_scan(qk, v, cols, rows, n_lat, kg, fwd, fused=None):
    m = v.shape[0]
    vh = v.shape[1] // HEAD
    ch = GDN_CHUNK
    nch = m // ch
    lat_ch = n_lat // ch
    ngrp = vh // (2 * kg)
    nv = 2 * kg
    d = 0 if fwd else 1

    def chunk(n):
        return (n + lat_ch) % nch if fwd else nch - 1 - n

    head_spec = pl.BlockSpec((ch, nv * HEAD), lambda g, n: (chunk(n), g))
    in_specs = [pl.BlockSpec((ch, kg * HEAD), lambda g, n: (chunk(n), g)),
                pl.BlockSpec((ch, kg * HEAD), lambda g, n: (chunk(n), ngrp + g)),
                head_spec,
                pl.BlockSpec((None, None, None, 2 * ch, 2 * kg), lambda g, n: (d, g, chunk(n), 0, 0)),
                pl.BlockSpec((None, None, None, kg, 2 * ch), lambda g, n: (d, g, chunk(n), 0, 0))]
    args = [qk, qk, v, cols, rows]
    if fused is not None:
        o_other, zsrc, z_col, g_norm = fused
        zcb = z_col // (nv * HEAD)
        in_specs += [head_spec,
                     pl.BlockSpec((ch, nv * HEAD), lambda g, n: (chunk(n), zcb + g)),
                     pl.BlockSpec((1, HEAD), lambda g, n: (0, 0))]
        args += [o_other, zsrc, g_norm.reshape(1, HEAD).astype(F32)]
    return pl.pallas_call(
        functools.partial(_gdn_kernel, kg=kg, fwd=fwd),
        grid=(ngrp, nch),
        in_specs=in_specs,
        out_specs=head_spec,
        out_shape=jax.ShapeDtypeStruct((m, vh * HEAD), BF16),
        scratch_shapes=[pltpu.VMEM((kg, 2, HEAD, HEAD), F32)],
        compiler_params=_cparams(("arbitrary", "arbitrary")),
        name="gdn_scan",
    )(*args)


def _gdn_mixer(h, layer, w_in_all, conv_w, a_log, dt_bias, g_norm, w_out_all, n_lat):
    m, d = h.shape
    khs = d // HEAD
    vhs = 2 * khs
    kw, vw = khs * HEAD, vhs * HEAD
    qkv_w = 2 * kw + vw
    n_ab = 4 * vhs
    if n_ab % 256 == 0 and qkv_w % n_ab == 0:
        qkvz = _matmul(h, w_in_all, tn=n_ab, w_cols=(layer, 0, qkv_w + vw, (qkv_w, n_ab)))
        ab = _matmul(h, w_in_all, out_dtype=F32, tn=n_ab, w_cols=(layer, qkv_w, n_ab, None))
    else:
        w_in = w_in_all[layer]
        w_big = jnp.concatenate([w_in[:, :qkv_w], w_in[:, qkv_w + n_ab:]], axis=1).astype(BF16)
        qkvz = _matmul(h, w_big)
        ab = _matmul(h, w_in[:, qkv_w:qkv_w + n_ab].astype(BF16), out_dtype=F32, tn=n_ab)
    qk = _gdn_conv(qkvz, conv_w, 0, 2 * kw, True, kw, n_lat)
    v = _gdn_conv(qkvz, conv_w, 2 * kw, vw, False, 0, n_lat)
    gcs, beta = _gdn_gates(ab, a_log, dt_bias)
    kg = _pick(khs, (GDN_PAIRS, 4, 2, 1))
    cols, rows = _gdn_gate_layouts(gcs, beta, kg)
    o_fwd = _gdn_scan(qk, v, cols, rows, n_lat, kg, True)
    y = _gdn_scan(qk, v, cols, rows, n_lat, kg, False, fused=(o_fwd, qkvz, qkv_w, g_norm))
    return _matmul(y, w_out_all, w_cols=(layer, 0, d, None))


def _pool_kernel(prev_ref, cur_ref, next_ref, w_ref, sc_ref, z_ref, o_ref, scr, *,
                 n_lat_blocks, n_lat, n_ctx):
    tm, tc = cur_ref.shape
    _fill_window(scr, prev_ref, cur_ref, next_ref, n_lat_blocks)
    t, seg_len = _seg_position(tm, n_lat_blocks, n_lat, n_ctx)
    for g, window in enumerate(POOL_WINDOWS):
        @pl.when(pl.program_id(1) == g)
        def _(radius=window // 2):
            u = scr[HALO:HALO + tm, :]
            acc = u
            for dlt in range(1, radius + 1):
                acc = acc + (scr[HALO - dlt:HALO - dlt + tm, :] + scr[HALO + dlt:HALO + dlt + tm, :])
            cnt = jnp.minimum(t + radius + 1, seg_len) - jnp.maximum(t - radius, 0)
            mean_minus = acc / cnt.astype(F32) - u
            y = _dot(mean_minus.astype(BF16), w_ref[...]) * sc_ref[...]
            o_ref[...] = (y * _silu(z_ref[...].astype(F32))).astype(o_ref.dtype)


def _pool_mixer(h, layer, w_in_all, w_grp, scale, w_out_all, n_lat):
    m, d = h.shape
    ng, gw, _ = w_grp.shape
    assert tuple(w // 2 for w in POOL_WINDOWS) == tuple(2 ** g for g in range(ng))
    uz = _matmul(h, w_in_all, w_cols=(layer, 0, 2 * d, None))
    tm = SEG_TILE
    specs = _window_specs(tm, gw, m, lambda g: g)
    specs += [pl.BlockSpec((None, gw, gw), lambda i, g: (g, 0, 0)),
              pl.BlockSpec((1, gw), lambda i, g: (0, g)),
              pl.BlockSpec((tm, gw), lambda i, g: (i, ng + g))]
    yg = pl.pallas_call(
        functools.partial(_pool_kernel, n_lat_blocks=n_lat // tm, n_lat=n_lat, n_ctx=m - n_lat),
        grid=(m // tm, ng),
        in_specs=specs,
        out_specs=pl.BlockSpec((tm, gw), lambda i, g: (i, g)),
        out_shape=jax.ShapeDtypeStruct((m, d), BF16),
        scratch_shapes=[pltpu.VMEM((tm + 2 * HALO, gw), F32)],
        compiler_params=_cparams(("arbitrary", "arbitrary")),
        name="pool_group",
    )(uz, uz, uz, w_grp.astype(BF16), scale.reshape(1, d).astype(F32), uz)
    return _matmul(yg, w_out_all, w_cols=(layer, 0, d, None))


def kernel(x, c, ctx, c_ctx, ada_w, ada_b, norm_pre, norm_post, mla_w_in, mla_g_q, mla_w_q_up,
           mla_g_kv, mla_w_kv_up, mla_w_out, gdn_w_in, gdn_conv_w, gdn_a_log, gdn_dt_bias,
           gdn_g_norm, gdn_w_out, pool_w_in, pool_w_grp, pool_scale, pool_w_out):
    assert x.shape[0] == 1 and ctx.shape[0] == 1
    n_lat, d = x.shape[1], x.shape[2]
    n_ctx = ctx.shape[1]
    depth = ada_w.shape[0]
    n_mixers = 3
    assert n_lat % SEG_TILE == 0 and n_ctx % SEG_TILE == 0

    cond8 = jnp.concatenate([c.reshape(1, d), c_ctx.reshape(1, d), jnp.zeros((6, d), F32)], axis=0)
    mods = _adaln(cond8, ada_w, ada_b)
    tables = _rope_tables(n_lat, n_ctx)
    xs, h = _prenorm(x[0], ctx[0], norm_pre[0], mods[0])

    counts = [0, 0, 0]
    for i in range(depth):
        kind = i % n_mixers
        j = counts[kind]
        counts[kind] += 1
        need_ctx = i < depth - 1
        if kind == 0:
            y = _mla_mixer(h, tables, j, mla_w_in, mla_g_q[j], mla_w_q_up[j], mla_g_kv[j],
                           mla_w_kv_up[j], mla_w_out, n_lat, need_ctx)
        elif kind == 1:
            y = _gdn_mixer(h, j, gdn_w_in, gdn_conv_w[j], gdn_a_log[j], gdn_dt_bias[j],
                           gdn_g_norm[j], gdn_w_out, n_lat)
        else:
            y = _pool_mixer(h, j, pool_w_in, pool_w_grp[j], pool_scale[j], pool_w_out, n_lat)
        if need_ctx:
            xs, h = _residual(xs, y, norm_post[i], mods[i], n_lat, xs.shape[0],
                              norm_pre[i + 1], mods[i + 1])
        else:
            xs = _residual(xs, y, norm_post[i], mods[i], n_lat, n_lat)
    return xs[None]
```

```python
import functools
import math

import jax
import jax.numpy as jnp
from jax import lax
from jax.experimental import pallas as pl
from jax.experimental.pallas import tpu as pltpu

F32 = jnp.float32
BF16 = jnp.bfloat16
EPS = 1e-6

HEAD = 128
MLA_ROPE = 64
MLA_KV_RANK = 512
MLA_QK_PAD = 256
ROPE_BASE = 10000.0
GRID_W = 64
GDN_CONV = 5
GDN_CHUNK = 64
GDN_PAIRS = 16
POOL_WINDOWS = (2, 4, 8, 16)
SEG_TILE = 256
HALO = 16
VMEM_LIMIT = 56 * 1024 * 1024
MM_MAX_TK = 8192
MM_W_TILE_ELEMS = 2 * 1024 * 1024
MM_A_TILE_ELEMS = 6 * 1024 * 1024


def _pick(n, prefs):
    for p in prefs:
        if n % p == 0:
            return p
    return n


def _cparams(sem):
    return pltpu.CompilerParams(dimension_semantics=sem, vmem_limit_bytes=VMEM_LIMIT)


def _dot(a, b):
    return jnp.dot(a, b, preferred_element_type=F32)


def _dot_nt(a, b):
    return lax.dot_general(a, b, (((1,), (1,)), ((), ())), preferred_element_type=F32)


def _silu(x):
    return x * jax.nn.sigmoid(x)


def _adaln_kernel(c_ref, w_ref, b_ref, o_ref):
    a = _silu(c_ref[...]).astype(BF16)
    o_ref[...] = _dot(a, w_ref[...].astype(BF16)) + b_ref[...]


def _adaln(cond8, ada_w, ada_b):
    depth, d, n = ada_w.shape
    tn = _pick(n, (512, 256, 128))
    return pl.pallas_call(
        _adaln_kernel,
        grid=(depth, n // tn),
        in_specs=[pl.BlockSpec((8, d), lambda l, j: (0, 0)),
                  pl.BlockSpec((None, d, tn), lambda l, j: (l, 0, j)),
                  pl.BlockSpec((None, 1, tn), lambda l, j: (l, 0, j))],
        out_specs=pl.BlockSpec((None, 8, tn), lambda l, j: (l, 0, j)),
        out_shape=jax.ShapeDtypeStruct((depth, 8, n), F32),
        compiler_params=_cparams(("arbitrary", "arbitrary")),
        name="adaln",
    )(cond8, ada_w, ada_b.reshape(depth, 1, n))


def _rms(xf):
    return xf * lax.rsqrt(jnp.mean(xf * xf, axis=-1, keepdims=True) + EPS)


def _prenorm_kernel(xl_ref, xc_ref, g_ref, mod_ref, xs_ref, h_ref, *, n_lat_blocks, d):
    is_ctx = pl.program_id(0) >= n_lat_blocks
    x = jnp.where(is_ctx, xc_ref[...], xl_ref[...])
    mod = jnp.where(is_ctx, mod_ref[1:2, :], mod_ref[0:1, :])
    sh, sc = mod[:, :d], mod[:, d:2 * d]
    y = _rms(x) * g_ref[...]
    xs_ref[...] = x
    h_ref[...] = (y * (1.0 + sc) + sh).astype(h_ref.dtype)


def _prenorm(x_lat, x_ctx, g, mod):
    n_lat, d = x_lat.shape
    m = n_lat + x_ctx.shape[0]
    tm = SEG_TILE
    nlb = n_lat // tm
    row_spec = pl.BlockSpec((tm, d), lambda i: (i, 0))
    return pl.pallas_call(
        functools.partial(_prenorm_kernel, n_lat_blocks=nlb, d=d),
        grid=(m // tm,),
        in_specs=[pl.BlockSpec((tm, d), lambda i: (jnp.minimum(i, nlb - 1), 0)),
                  pl.BlockSpec((tm, d), lambda i: (jnp.maximum(i - nlb, 0), 0)),
                  pl.BlockSpec((1, d), lambda i: (0, 0)),
                  pl.BlockSpec((8, 3 * d), lambda i: (0, 0))],
        out_specs=[row_spec, row_spec],
        out_shape=[jax.ShapeDtypeStruct((m, d), F32), jax.ShapeDtypeStruct((m, d), BF16)],
        compiler_params=_cparams(("arbitrary",)),
        name="prenorm",
    )(x_lat, x_ctx, g.reshape(1, d), mod)


def _residual_kernel(x_ref, y_ref, g_ref, mod_ref, *rest, n_lat_blocks, d, with_next):
    is_ctx = pl.program_id(0) >= n_lat_blocks
    mod = jnp.where(is_ctx, mod_ref[1:2, :], mod_ref[0:1, :])
    gt = mod[:, 2 * d:]
    y = _rms(y_ref[...].astype(F32)) * g_ref[...]
    x_new = x_ref[...] + gt * y
    if not with_next:
        rest[0][...] = x_new
        return
    gn_ref, modn_ref, o_ref, h_ref = rest
    o_ref[...] = x_new
    modn = jnp.where(is_ctx, modn_ref[1:2, :], modn_ref[0:1, :])
    h = _rms(x_new) * gn_ref[...]
    h_ref[...] = (h * (1.0 + modn[:, d:2 * d]) + modn[:, :d]).astype(h_ref.dtype)


def _residual(x, y, g, mod, n_lat, rows, g_next=None, mod_next=None):
    d = x.shape[1]
    tm = SEG_TILE
    with_next = g_next is not None
    row_spec = pl.BlockSpec((tm, d), lambda i: (i, 0))
    vec_spec = pl.BlockSpec((1, d), lambda i: (0, 0))
    mod_spec = pl.BlockSpec((8, 3 * d), lambda i: (0, 0))
    in_specs = [row_spec, row_spec, vec_spec, mod_spec]
    args = [x, y, g.reshape(1, d), mod]
    out_specs, out_shape = row_spec, jax.ShapeDtypeStruct((rows, d), F32)
    if with_next:
        in_specs += [vec_spec, mod_spec]
        args += [g_next.reshape(1, d), mod_next]
        out_specs = [row_spec, row_spec]
        out_shape = [out_shape, jax.ShapeDtypeStruct((rows, d), BF16)]
    return pl.pallas_call(
        functools.partial(_residual_kernel, n_lat_blocks=n_lat // tm, d=d, with_next=with_next),
        grid=(rows // tm,),
        in_specs=in_specs,
        out_specs=out_specs,
        out_shape=out_shape,
        compiler_params=_cparams(("arbitrary",)),
        name="residual",
    )(*args)


def _mm_kernel(a_ref, w_ref, *rest, nk, has_gain, epilogue, n_extra):
    pos = 0
    g_ref = rest[pos] if has_gain else None
    pos += int(has_gain)
    extra = rest[pos:pos + n_extra]
    pos += n_extra
    o_ref = rest[pos]
    acc_ref = rest[pos + 1] if nk > 1 else None

    if has_gain:
        an_ref = rest[pos + 1]

        @pl.when(pl.program_id(1) == 0)
        def _():
            an_ref[...] = (_rms(a_ref[...].astype(F32)) * g_ref[...]).astype(BF16)

        a = an_ref[...]
    else:
        a = a_ref[...]
    r = _dot(a, w_ref[...].astype(BF16))

    def finish(res):
        if epilogue is not None:
            res = epilogue(res, *[e[...] for e in extra])
        o_ref[...] = res.astype(o_ref.dtype)

    if nk == 1:
        finish(r)
    else:
        k = pl.program_id(2)

        @pl.when(k == 0)
        def _():
            acc_ref[...] = r

        @pl.when(k > 0)
        def _():
            acc_ref[...] += r

        @pl.when(k == nk - 1)
        def _():
            finish(acc_ref[...])


def _matmul(a, w, *, rows=None, a_col=0, k=None, out_dtype=BF16, gain=None,
            epilogue=None, extras=(), tn=None, tm=None, out_widen=1, w_cols=None):
    if w_cols is not None:
        layer, col0, n, skip = w_cols
        k = w.shape[1]
    else:
        k = w.shape[0] if k is None else k
        n = w.shape[1]
    rows = a.shape[0] if rows is None else rows
    tk = k if k <= MM_MAX_TK else _pick(k, (4096, 2048, 1024, 512))
    nk = k // tk
    tm = tm or _pick(rows, [t for t in (1408, 1024, 768, 512, 256, 128, 64, 32, 16)
                            if t * tk <= MM_A_TILE_ELEMS] or [8])
    tn = tn or _pick(n, [t for t in (1024, 512, 256, 128) if tk * t <= MM_W_TILE_ELEMS] or [128])
    assert a_col % tk == 0 and (gain is None or nk == 1)
    a_cb = a_col // tk
    if w_cols is None:
        w_spec = pl.BlockSpec((tk, tn), lambda i, j, kk: (kk, j))
    else:
        skip_at, skip_n = (0, 0) if skip is None else skip
        assert col0 % tn == 0 and skip_at % tn == 0 and skip_n % tn == 0

        def w_block(j):
            jb = col0 // tn + j
            return jb + jnp.where(jb >= skip_at // tn, skip_n // tn, 0) if skip_n else jb

        w_spec = pl.BlockSpec((None, tk, tn), lambda i, j, kk: (layer, kk, w_block(j)))
    in_specs = [pl.BlockSpec((tm, tk), lambda i, j, kk: (i, a_cb + kk)), w_spec]
    args = [a, w]
    if gain is not None:
        in_specs.append(pl.BlockSpec((1, tk), lambda i, j, kk: (0, 0)))
        args.append(gain.reshape(1, tk).astype(F32))
    for arr, cols, cb in extras:
        in_specs.append(pl.BlockSpec((tm, cols), lambda i, j, kk, cb=cb: (i, cb(j))))
        args.append(arr)
    scratch = [pltpu.VMEM((tm, tn), F32)] if nk > 1 else []
    if gain is not None:
        scratch = [pltpu.VMEM((tm, tk), BF16)]
    return pl.pallas_call(
        functools.partial(_mm_kernel, nk=nk, has_gain=gain is not None,
                          epilogue=epilogue, n_extra=len(extras)),
        grid=(rows // tm, n // tn, nk),
        in_specs=in_specs,
        out_specs=pl.BlockSpec((tm, tn * out_widen), lambda i, j, kk: (i, j)),
        out_shape=jax.ShapeDtypeStruct((rows, n * out_widen), out_dtype),
        scratch_shapes=scratch,
        compiler_params=_cparams(("arbitrary", "arbitrary", "arbitrary")),
        name="matmul",
    )(*args)


def _rope_tables(n_lat, n_ctx):
    rows = n_lat // GRID_W
    row = jnp.repeat(jnp.arange(rows, dtype=F32), GRID_W)
    col = jnp.tile(jnp.arange(GRID_W, dtype=F32), rows)
    n_freq = MLA_ROPE // 4
    inv_freq = ROPE_BASE ** (-jnp.arange(n_freq, dtype=F32) / n_freq)
    ang = jnp.concatenate([row[:, None] * inv_freq, col[:, None] * inv_freq], axis=-1)
    ang = jnp.concatenate([ang, jnp.zeros((n_ctx, ang.shape[1]), F32)], axis=0)
    cos, sin = jnp.cos(ang), jnp.sin(ang)
    ta = jnp.concatenate([cos, sin, cos, -sin], axis=1)
    tb = jnp.concatenate([-sin, cos, sin, cos], axis=1)
    tq = jnp.concatenate([jnp.ones((ang.shape[0], HEAD), F32), ta], axis=1)
    return tq, ta, tb


def _q_epilogue(t, tq):
    reps = t.shape[1] // MLA_QK_PAD
    return t * (jnp.concatenate([tq] * reps, axis=1) if reps > 1 else tq)


def _k_epilogue(t, kr):
    krf = kr.astype(F32)
    parts = []
    for hh in range(t.shape[1] // HEAD):
        parts += [t[:, hh * HEAD:(hh + 1) * HEAD], krf]
    return jnp.concatenate(parts, axis=1)


def _ckv_epilogue(t, ta, tb):
    r = MLA_KV_RANK
    a, b = t[:, r:r + HEAD], t[:, r + HEAD:]
    return jnp.concatenate([t[:, :r], a * ta + b * tb, b], axis=1)


def _flash_kernel(q_ref, k_ref, v_ref, z_ref, o_ref, *, bounds):
    q = q_ref[...]
    tq = q.shape[0]
    nc = len(bounds) - 1
    m = jnp.full((tq, 1), -jnp.inf, F32)
    acc = jnp.zeros((tq, 2 * HEAD), F32)
    s = _dot_nt(q, k_ref[bounds[0]:bounds[1], :])
    for c in range(nc):
        lo, hi = bounds[c], bounds[c + 1]
        s_next = _dot_nt(q, k_ref[hi:bounds[c + 2], :]) if c + 1 < nc else None
        m_new = jnp.maximum(m, jnp.max(s, axis=-1, keepdims=True))
        p = jnp.exp2(s - m_new)
        alpha = jnp.exp2(m - m_new)
        ones_col = (lax.broadcasted_iota(jnp.int32, (hi - lo, HEAD), 1) == 0).astype(BF16)
        v_ext = jnp.concatenate([v_ref[lo:hi, :], ones_col], axis=1)
        acc = alpha * acc + _dot(p.astype(BF16), v_ext)
        m, s = m_new, s_next
    o = acc[:, :HEAD] / acc[:, HEAD:HEAD + 1]
    o_ref[...] = (o * _silu(z_ref[...].astype(F32))).astype(o_ref.dtype)


def _flash(q, k, v, zsrc, z_col, og_prev, *, q_row0, n_q, k_row0, n_k, total_rows):
    heads = v.shape[1] // HEAD
    tq = _pick(n_q, (1024, 512, 256, 128))
    tk = _pick(n_k, (2816, 768, 512, 256, 128))
    bounds = tuple(range(0, n_k + 1, tk))
    assert q_row0 % tq == 0 and k_row0 % n_k == 0
    qb0, kb0, zcb = q_row0 // tq, k_row0 // n_k, z_col // HEAD
    in_specs = [pl.BlockSpec((tq, MLA_QK_PAD), lambda h, i: (qb0 + i, h)),
                pl.BlockSpec((n_k, MLA_QK_PAD), lambda h, i: (kb0, h)),
                pl.BlockSpec((n_k, HEAD), lambda h, i: (kb0, h)),
                pl.BlockSpec((tq, HEAD), lambda h, i: (qb0 + i, zcb + h))]
    args = [q, k, v, zsrc]
    aliases = {}
    if og_prev is not None:
        in_specs.append(pl.BlockSpec(memory_space=pl.ANY))
        args.append(og_prev)
        aliases = {4: 0}

    def kern(q_ref, k_ref, v_ref, z_ref, *rest):
        _flash_kernel(q_ref, k_ref, v_ref, z_ref, rest[-1], bounds=bounds)

    return pl.pallas_call(
        kern,
        grid=(heads, n_q // tq),
        in_specs=in_specs,
        out_specs=pl.BlockSpec((tq, HEAD), lambda h, i: (qb0 + i, h)),
        out_shape=jax.ShapeDtypeStruct((total_rows, heads * HEAD), BF16),
        input_output_aliases=aliases,
        compiler_params=_cparams(("arbitrary", "arbitrary")),
        name="mla_flash",
    )(*args)


def _mla_weights(w_in, w_q_up, w_kv_up):
    d = w_in.shape[0]
    q_rank = d // 4
    heads = d // HEAD
    r0, r1, r2 = q_rank, q_rank + MLA_KV_RANK, q_rank + MLA_KV_RANK + MLA_ROPE
    half = MLA_ROPE // 2
    assert MLA_QK_PAD == HEAD + 2 * MLA_ROPE
    scale = (HEAD + MLA_ROPE) ** -0.5 * math.log2(math.e)
    w_z = w_in[:, r2:].astype(BF16)
    k1, k2 = w_in[:, r1:r1 + half], w_in[:, r1 + half:r2]
    w_ckv = jnp.concatenate([w_in[:, r0:r1], k1, k1, k2, k2, k2, k2, k1, k1], axis=1).astype(BF16)
    wq = (w_q_up * scale).reshape(q_rank, heads, HEAD + MLA_ROPE)
    x1, x2 = wq[:, :, HEAD:HEAD + half], wq[:, :, HEAD + half:]
    wq = jnp.concatenate([wq[:, :, :HEAD], x1, x1, x2, x2], axis=2)
    wq = wq.reshape(q_rank, heads * MLA_QK_PAD).astype(BF16)
    wkv = w_kv_up.reshape(MLA_KV_RANK, heads, 2 * HEAD)
    wk = wkv[:, :, :HEAD].reshape(MLA_KV_RANK, heads * HEAD).astype(BF16)
    wv = wkv[:, :, HEAD:].reshape(MLA_KV_RANK, heads * HEAD).astype(BF16)
    return w_z, w_ckv, wq, wk, wv


def _mla_mixer(h, tables, layer, w_in_all, g_q, w_q_up, g_kv, w_kv_up, w_out_all, n_lat, need_ctx):
    m, d = h.shape
    n_ctx = m - n_lat
    q_rank = d // 4
    tq, ta, tb = tables
    w_z, w_ckv, wq, wk, wv = _mla_weights(w_in_all[layer], w_q_up, w_kv_up)
    first = lambda j: 0

    cq = _matmul(h, w_in_all[layer][:, :q_rank].astype(BF16))
    z = _matmul(h, w_z)
    ckvkr = _matmul(h, w_ckv, epilogue=_ckv_epilogue, tn=w_ckv.shape[1],
                    extras=[(ta, HEAD, first), (tb, HEAD, first)])
    q = _matmul(cq, wq, gain=g_q, epilogue=_q_epilogue, extras=[(tq, MLA_QK_PAD, first)])
    kr_cb = MLA_KV_RANK // HEAD
    kk = _matmul(ckvkr, wk, k=MLA_KV_RANK, gain=g_kv, epilogue=_k_epilogue,
                 extras=[(ckvkr, HEAD, lambda j: kr_cb)], out_widen=MLA_QK_PAD // HEAD)
    v = _matmul(ckvkr, wv, k=MLA_KV_RANK, gain=g_kv)

    og = _flash(q, kk, v, z, 0, None, q_row0=0, n_q=n_lat, k_row0=0, n_k=m, total_rows=m)
    if need_ctx:
        og = _flash(q, kk, v, z, 0, og, q_row0=n_lat, n_q=n_ctx, k_row0=n_lat, n_k=n_ctx,
                    total_rows=m)
    rows = m if need_ctx else n_lat
    return _matmul(og, w_out_all, rows=rows, w_cols=(layer, 0, d, None))


def _fill_window(scr, prev_ref, cur_ref, next_ref, n_lat_blocks):
    tm = cur_ref.shape[0]
    i = pl.program_id(0)
    first = jnp.logical_or(i == 0, i == n_lat_blocks)
    last = jnp.logical_or(i == n_lat_blocks - 1, i == pl.num_programs(0) - 1)
    scr[0:HALO, :] = jnp.where(first, 0.0, prev_ref[...].astype(F32))
    scr[HALO:HALO + tm, :] = cur_ref[...].astype(F32)
    scr[HALO + tm:, :] = jnp.where(last, 0.0, next_ref[...].astype(F32))


def _seg_position(tm, n_lat_blocks, n_lat, n_ctx):
    i = pl.program_id(0)
    is_ctx = i >= n_lat_blocks
    local0 = jnp.where(is_ctx, i - n_lat_blocks, i) * tm
    t = local0 + lax.broadcasted_iota(jnp.int32, (tm, 1), 0)
    return t, jnp.where(is_ctx, n_ctx, n_lat)


def _window_specs(tm, cols, n_rows, col_fn):
    hb = tm // HALO
    last = n_rows // HALO - 1
    return [pl.BlockSpec((HALO, cols), lambda i, j: (jnp.maximum(i * hb - 1, 0), col_fn(j))),
            pl.BlockSpec((tm, cols), lambda i, j: (i, col_fn(j))),
            pl.BlockSpec((HALO, cols), lambda i, j: (jnp.minimum((i + 1) * hb, last), col_fn(j)))]


def _gdn_conv_kernel(prev_ref, cur_ref, next_ref, w_ref, o_ref, scr, *, normalize,
                     n_scaled_blocks, n_lat_blocks):
    tm, tc = cur_ref.shape
    _fill_window(scr, prev_ref, cur_ref, next_ref, n_lat_blocks)
    r = GDN_CONV // 2
    acc = scr[HALO - r:HALO - r + tm, :] * w_ref[0:1, :]
    for j in range(1, GDN_CONV):
        acc = acc + scr[HALO + j - r:HALO + j - r + tm, :] * w_ref[j:j + 1, :]
    y = _silu(acc)
    if normalize:
        scale = jnp.where(pl.program_id(1) < n_scaled_blocks, HEAD ** -0.5, 1.0)
        parts = []
        for hh in range(tc // HEAD):
            u = y[:, hh * HEAD:(hh + 1) * HEAD]
            parts.append(u * (lax.rsqrt(jnp.sum(u * u, axis=-1, keepdims=True) + EPS) * scale))
        y = jnp.concatenate(parts, axis=1) if len(parts) > 1 else parts[0]
    o_ref[...] = y.astype(o_ref.dtype)


def _gdn_conv(src, conv_w, col0, ncols, normalize, n_scaled_cols, n_lat):
    m = src.shape[0]
    tm = SEG_TILE
    tc = _pick(math.gcd(ncols, n_scaled_cols) if n_scaled_cols else ncols, (2048, 1024, 512, 256, 128))
    assert ncols % tc == 0 and n_scaled_cols % tc == 0 and col0 % tc == 0
    cb0 = col0 // tc
    specs = _window_specs(tm, tc, m, lambda j: cb0 + j)
    specs.append(pl.BlockSpec((8, tc), lambda i, j: (0, cb0 + j)))
    w8 = jnp.concatenate([conv_w, jnp.zeros((8 - GDN_CONV, conv_w.shape[1]), F32)], axis=0)
    return pl.pallas_call(
        functools.partial(_gdn_conv_kernel, normalize=normalize,
                          n_scaled_blocks=n_scaled_cols // tc, n_lat_blocks=n_lat // tm),
        grid=(m // tm, ncols // tc),
        in_specs=specs,
        out_specs=pl.BlockSpec((tm, tc), lambda i, j: (i, j)),
        out_shape=jax.ShapeDtypeStruct((m, ncols), BF16),
        scratch_shapes=[pltpu.VMEM((tm + 2 * HALO, tc), F32)],
        compiler_params=_cparams(("arbitrary", "arbitrary")),
        name="gdn_conv",
    )(src, src, src, w8)


def _split3(x):
    x1 = x.astype(BF16)
    r1 = x - x1.astype(F32)
    x2 = r1.astype(BF16)
    x3 = (r1 - x2.astype(F32)).astype(BF16)
    return x1, x2, x3


def _gdn_gate_kernel(ab_ref, alog_ref, dtb_ref, g_ref, beta_ref):
    tm, w = g_ref.shape
    ab = ab_ref[...]
    x = ab[:, :w] + dtb_ref[...]
    softplus = jnp.maximum(x, 0.0) + jnp.log1p(jnp.exp(-jnp.abs(x)))
    g = -jnp.exp(alog_ref[...]) * softplus
    beta_ref[...] = jax.nn.sigmoid(ab[:, w:])
    ch = GDN_CHUNK
    ri = lax.broadcasted_iota(jnp.int32, (ch, ch), 0)
    ci = lax.broadcasted_iota(jnp.int32, (ch, ch), 1)
    lower = (ci <= ri).astype(BF16)
    upper = (ci >= ri).astype(BF16)
    fwd_lane = lax.broadcasted_iota(jnp.int32, (ch, w), 1) < w // 2
    for cidx in range(tm // ch):
        parts = _split3(g[cidx * ch:(cidx + 1) * ch, :])
        pre = _dot(lower, parts[0]) + (_dot(lower, parts[1]) + _dot(lower, parts[2]))
        suf = _dot(upper, parts[0]) + (_dot(upper, parts[1]) + _dot(upper, parts[2]))
        g_ref[cidx * ch:(cidx + 1) * ch, :] = jnp.where(fwd_lane, pre, suf)


def _gdn_gates(ab, a_log, dt_bias):
    m, w4 = ab.shape
    w = w4 // 2
    tm = SEG_TILE
    return pl.pallas_call(
        _gdn_gate_kernel,
        grid=(m // tm,),
        in_specs=[pl.BlockSpec((tm, w4), lambda i: (i, 0)),
                  pl.BlockSpec((1, w), lambda i: (0, 0)),
                  pl.BlockSpec((1, w), lambda i: (0, 0))],
        out_specs=[pl.BlockSpec((tm, w), lambda i: (i, 0)),
                   pl.BlockSpec((tm, w), lambda i: (i, 0))],
        out_shape=[jax.ShapeDtypeStruct((m, w), F32), jax.ShapeDtypeStruct((m, w), F32)],
        compiler_params=_cparams(("arbitrary",)),
        name="gdn_gates",
    )(ab, a_log.reshape(1, w).astype(F32), dt_bias.reshape(1, w).astype(F32))


def _bmm(a, b):
    return lax.dot_general(a, b, (((2,), (1,)), ((0,), (0,))), preferred_element_type=F32)


def _unit_tri_inverse(a, ri, ci):
    def same(s):
        return (ri // s) == (ci // s)

    t = (ri == ci).astype(F32) - jnp.where(same(2), a, 0.0)
    s = 2
    while s < GDN_CHUNK:
        coupling = jnp.logical_and(same(2 * s), jnp.logical_not(same(s)))
        tb = t.astype(BF16)
        tl = _bmm(tb, jnp.where(coupling, a, 0.0).astype(BF16))
        t = t - _bmm(tl.astype(BF16), tb)
        s *= 2
    return t


def _gdn_kernel(q_ref, k_ref, v_ref, col_ref, row_ref, *rest, kg, fwd):
    o_ref, state = rest[-2:]
    ch = GDN_CHUNK
    p2 = 2 * ch

    @pl.when(pl.program_id(1) == 0)
    def _():
        state[...] = jnp.zeros_like(state)

    ri = lax.broadcasted_iota(jnp.int32, (1, p2, p2), 1)
    ci = lax.broadcasted_iota(jnp.int32, (1, p2, p2), 2)
    same_head = (ri // ch) == (ci // ch)
    ahead = ci - ri if fwd else ri - ci
    incl = jnp.logical_and(same_head, ahead <= 0)
    strict = jnp.logical_and(incl, ci != ri)
    first_lanes = lax.broadcasted_iota(jnp.int32, (1, 1, p2), 2) < ch

    def pair_rows(ref):
        x = jnp.stack([ref[:, p * HEAD:(p + 1) * HEAD] for p in range(kg)]).astype(F32)
        return jnp.concatenate([x, x], axis=1)

    def head_rows(ref):
        return jnp.stack([jnp.concatenate([ref[:, (2 * p) * HEAD:(2 * p + 1) * HEAD],
                                           ref[:, (2 * p + 1) * HEAD:(2 * p + 2) * HEAD]], axis=0)
                          for p in range(kg)]).astype(F32)

    k2 = pair_rows(k_ref)
    q2 = pair_rows(q_ref)
    vf = head_rows(v_ref)
    kt = jnp.stack([k2[p].T for p in range(kg)])
    kkqk = _bmm(jnp.concatenate([k2, q2], axis=1).astype(BF16), kt.astype(BF16))
    kk, qk = kkqk[:, :p2], kkqk[:, p2:]

    def col(j0):
        c = jnp.stack([col_ref[:, j0 + p:j0 + p + 1] for p in range(kg)])
        return jnp.broadcast_to(c, (kg, p2, p2))

    gcol, bcol = col(0), col(kg)
    grow = jnp.stack([row_ref[p:p + 1, :] for p in range(kg)])
    last = ch - 1 if fwd else 0
    tot_a = grow[:, :, last:last + 1]
    tot_b = grow[:, :, ch + last:ch + last + 1]
    tot_row = jnp.where(first_lanes, tot_a, tot_b)

    decay = jnp.exp(jnp.where(incl, gcol - grow, -jnp.inf))
    tinv = _unit_tri_inverse(jnp.where(strict, bcol * kk * decay, 0.0), ri, ci)

    eg = jnp.exp(gcol)
    rhs = jnp.concatenate([vf * bcol, k2 * (bcol * eg)], axis=2).astype(BF16)
    uw = _bmm(tinv.astype(BF16), rhs)
    u, w_b = uw[:, :, :HEAD], uw[:, :, HEAD:].astype(BF16)
    qk_d = jnp.where(incl, qk * decay, 0.0).astype(BF16)
    q_dec = (q2 * eg).astype(BF16)
    kt_dec = kt * jnp.exp(tot_row - grow)

    s_a = state[:, 0]
    s_b = state[:, 1]
    ws_a = _bmm(jnp.concatenate([w_b[:, :ch], q_dec[:, :ch]], axis=1), s_a.astype(BF16))
    ws_b = _bmm(jnp.concatenate([w_b[:, ch:], q_dec[:, ch:]], axis=1), s_b.astype(BF16))
    v_new = (u - jnp.concatenate([ws_a[:, :ch], ws_b[:, :ch]], axis=1)).astype(BF16)
    o = jnp.concatenate([ws_a[:, ch:], ws_b[:, ch:]], axis=1) + _bmm(qk_d, v_new)
    kd_a = jnp.where(first_lanes, kt_dec, 0.0).astype(BF16)
    kd_b = jnp.where(first_lanes, 0.0, kt_dec).astype(BF16)
    state[:, 0] = s_a * jnp.exp(tot_a) + _bmm(kd_a, v_new)
    state[:, 1] = s_b * jnp.exp(tot_b) + _bmm(kd_b, v_new)
    if len(rest) > 2:
        oprev_ref, z_ref, gn_ref = rest[:3]
        o = _rms(o + head_rows(oprev_ref)) * gn_ref[...] * _silu(head_rows(z_ref))
    for p in range(kg):
        o_ref[:, (2 * p) * HEAD:(2 * p + 1) * HEAD] = o[p, :ch].astype(o_ref.dtype)
        o_ref[:, (2 * p + 1) * HEAD:(2 * p + 2) * HEAD] = o[p, ch:].astype(o_ref.dtype)


def _gdn_gate_layouts(gcs, beta, kg):
    m, w = gcs.shape
    ch = GDN_CHUNK
    assert HEAD == 2 * ch
    nch, ngrp = m // ch, w // (4 * kg)
    g6 = gcs.reshape(nch, ch, 2, ngrp, kg, 2)
    b6 = beta.reshape(nch, ch, 2, ngrp, kg, 2)
    to_cols = lambda t: t.transpose(2, 3, 0, 5, 1, 4).reshape(2, ngrp, nch, 2 * ch, kg)
    cols = jnp.concatenate([to_cols(g6), to_cols(b6)], axis=4)
    rows = g6.transpose(2, 3, 0, 4, 5, 1).reshape(2, ngrp, nch, kg, 2 * ch)
    return cols, rows


def _gdn_scan(qk, v, cols, rows, n_lat, kg, fwd, fused=None):
    m = v.shape[0]
    vh = v.shape[1] // HEAD
    ch = GDN_CHUNK
    nch = m // ch
    lat_ch = n_lat // ch
    ngrp = vh // (2 * kg)
    nv = 2 * kg
    d = 0 if fwd else 1

    def chunk(n):
        return (n + lat_ch) % nch if fwd else nch - 1 - n

    head_spec = pl.BlockSpec((ch, nv * HEAD), lambda g, n: (chunk(n), g))
    in_specs = [pl.BlockSpec((ch, kg * HEAD), lambda g, n: (chunk(n), g)),
                pl.BlockSpec((ch, kg * HEAD), lambda g, n: (chunk(n), ngrp + g)),
                head_spec,
                pl.BlockSpec((None, None, None, 2 * ch, 2 * kg), lambda g, n: (d, g, chunk(n), 0, 0)),
                pl.BlockSpec((None, None, None, kg, 2 * ch), lambda g, n: (d, g, chunk(n), 0, 0))]
    args = [qk, qk, v, cols, rows]
    if fused is not None:
        o_other, zsrc, z_col, g_norm = fused
        zcb = z_col // (nv * HEAD)
        in_specs += [head_spec,
                     pl.BlockSpec((ch, nv * HEAD), lambda g, n: (chunk(n), zcb + g)),
                     pl.BlockSpec((1, HEAD), lambda g, n: (0, 0))]
        args += [o_other, zsrc, g_norm.reshape(1, HEAD).astype(F32)]
    return pl.pallas_call(
        functools.partial(_gdn_kernel, kg=kg, fwd=fwd),
        grid=(ngrp, nch),
        in_specs=in_specs,
        out_specs=head_spec,
        out_shape=jax.ShapeDtypeStruct((m, vh * HEAD), BF16),
        scratch_shapes=[pltpu.VMEM((kg, 2, HEAD, HEAD), F32)],
        compiler_params=_cparams(("arbitrary", "arbitrary")),
        name="gdn_scan",
    )(*args)


def _gdn_mixer(h, layer, w_in_all, conv_w, a_log, dt_bias, g_norm, w_out_all, n_lat):
    m, d = h.shape
    khs = d // HEAD
    vhs = 2 * khs
    kw, vw = khs * HEAD, vhs * HEAD
    qkv_w = 2 * kw + vw
    n_ab = 4 * vhs
    if n_ab % 256 == 0 and qkv_w % n_ab == 0:
        qkvz = _matmul(h, w_in_all, tn=n_ab, w_cols=(layer, 0, qkv_w + vw, (qkv_w, n_ab)))
        ab = _matmul(h, w_in_all, out_dtype=F32, tn=n_ab, w_cols=(layer, qkv_w, n_ab, None))
    else:
        w_in = w_in_all[layer]
        w_big = jnp.concatenate([w_in[:, :qkv_w], w_in[:, qkv_w + n_ab:]], axis=1).astype(BF16)
        qkvz = _matmul(h, w_big)
        ab = _matmul(h, w_in[:, qkv_w:qkv_w + n_ab].astype(BF16), out_dtype=F32, tn=n_ab)
    qk = _gdn_conv(qkvz, conv_w, 0, 2 * kw, True, kw, n_lat)
    v = _gdn_conv(qkvz, conv_w, 2 * kw, vw, False, 0, n_lat)
    gcs, beta = _gdn_gates(ab, a_log, dt_bias)
    kg = _pick(khs, (GDN_PAIRS, 4, 2, 1))
    cols, rows = _gdn_gate_layouts(gcs, beta, kg)
    o_fwd = _gdn_scan(qk, v, cols, rows, n_lat, kg, True)
    y = _gdn_scan(qk, v, cols, rows, n_lat, kg, False, fused=(o_fwd, qkvz, qkv_w, g_norm))
    return _matmul(y, w_out_all, w_cols=(layer, 0, d, None))


def _pool_kernel(prev_ref, cur_ref, next_ref, w_ref, sc_ref, z_ref, o_ref, scr, *,
                 n_lat_blocks, n_lat, n_ctx):
    tm, tc = cur_ref.shape
    _fill_window(scr, prev_ref, cur_ref, next_ref, n_lat_blocks)
    t, seg_len = _seg_position(tm, n_lat_blocks, n_lat, n_ctx)
    for g, window in enumerate(POOL_WINDOWS):
        @pl.when(pl.program_id(1) == g)
        def _(radius=window // 2):
            u = scr[HALO:HALO + tm, :]
            acc = u
            for dlt in range(1, radius + 1):
                acc = acc + (scr[HALO - dlt:HALO - dlt + tm, :] + scr[HALO + dlt:HALO + dlt + tm, :])
            cnt = jnp.minimum(t + radius + 1, seg_len) - jnp.maximum(t - radius, 0)
            mean_minus = acc / cnt.astype(F32) - u
            y = _dot(mean_minus.astype(BF16), w_ref[...]) * sc_ref[...]
            o_ref[...] = (y * _silu(z_ref[...].astype(F32))).astype(o_ref.dtype)


def _pool_mixer(h, layer, w_in_all, w_grp, scale, w_out_all, n_lat):
    m, d = h.shape
    ng, gw, _ = w_grp.shape
    assert tuple(w // 2 for w in POOL_WINDOWS) == tuple(2 ** g for g in range(ng))
    uz = _matmul(h, w_in_all, w_cols=(layer, 0, 2 * d, None))
    tm = SEG_TILE
    specs = _window_specs(tm, gw, m, lambda g: g)
    specs += [pl.BlockSpec((None, gw, gw), lambda i, g: (g, 0, 0)),
              pl.BlockSpec((1, gw), lambda i, g: (0, g)),
              pl.BlockSpec((tm, gw), lambda i, g: (i, ng + g))]
    yg = pl.pallas_call(
        functools.partial(_pool_kernel, n_lat_blocks=n_lat // tm, n_lat=n_lat, n_ctx=m - n_lat),
        grid=(m // tm, ng),
        in_specs=specs,
        out_specs=pl.BlockSpec((tm, gw), lambda i, g: (i, g)),
        out_shape=jax.ShapeDtypeStruct((m, d), BF16),
        scratch_shapes=[pltpu.VMEM((tm + 2 * HALO, gw), F32)],
        compiler_params=_cparams(("arbitrary", "arbitrary")),
        name="pool_group",
    )(uz, uz, uz, w_grp.astype(BF16), scale.reshape(1, d).astype(F32), uz)
    return _matmul(yg, w_out_all, w_cols=(layer, 0, d, None))


def kernel(x, c, ctx, c_ctx, ada_w, ada_b, norm_pre, norm_post, mla_w_in, mla_g_q, mla_w_q_up,
           mla_g_kv, mla_w_kv_up, mla_w_out, gdn_w_in, gdn_conv_w, gdn_a_log, gdn_dt_bias,
           gdn_g_norm, gdn_w_out, pool_w_in, pool_w_grp, pool_scale, pool_w_out):
    assert x.shape[0] == 1 and ctx.shape[0] == 1
    n_lat, d = x.shape[1], x.shape[2]
    n_ctx = ctx.shape[1]
    depth = ada_w.shape[0]
    n_mixers = 3
    assert n_lat % SEG_TILE == 0 and n_ctx % SEG_TILE == 0

    cond8 = jnp.concatenate([c.reshape(1, d), c_ctx.reshape(1, d), jnp.zeros((6, d), F32)], axis=0)
    mods = _adaln(cond8, ada_w, ada_b)
    tables = _rope_tables(n_lat, n_ctx)
    xs, h = _prenorm(x[0], ctx[0], norm_pre[0], mods[0])

    counts = [0, 0, 0]
    for i in range(depth):
        kind = i % n_mixers
        j = counts[kind]
        counts[kind] += 1
        need_ctx = i < depth - 1
        if kind == 0:
            y = _mla_mixer(h, tables, j, mla_w_in, mla_g_q[j], mla_w_q_up[j], mla_g_kv[j],
                           mla_w_kv_up[j], mla_w_out, n_lat, need_ctx)
        elif kind == 1:
            y = _gdn_mixer(h, j, gdn_w_in, gdn_conv_w[j], gdn_a_log[j], gdn_dt_bias[j],
                           gdn_g_norm[j], gdn_w_out, n_lat)
        else:
            y = _pool_mixer(h, j, pool_w_in, pool_w_grp[j], pool_scale[j], pool_w_out, n_lat)
        if need_ctx:
            xs, h = _residual(xs, y, norm_post[i], mods[i], n_lat, xs.shape[0],
                              norm_pre[i + 1], mods[i + 1])
        else:
            xs = _residual(xs, y, norm_post[i], mods[i], n_lat, n_lat)
    return xs[None]
```

```python
import functools
import math

import jax
import jax.numpy as jnp
from jax import lax
from jax.experimental import pallas as pl
from jax.experimental.pallas import tpu as pltpu

F32 = jnp.float32
BF16 = jnp.bfloat16
EPS = 1e-6

HEAD = 128
MLA_ROPE = 64
MLA_KV_RANK = 512
MLA_QK_PAD = 256
ROPE_BASE = 10000.0
GRID_W = 64
GDN_CONV = 5
GDN_CHUNK = 64
GDN_PAIRS = 16
POOL_WINDOWS = (2, 4, 8, 16)
SEG_TILE = 256
HALO = 16
VMEM_LIMIT = 56 * 1024 * 1024
MM_MAX_TK = 8192
MM_W_TILE_ELEMS = 2 * 1024 * 1024
MM_A_TILE_ELEMS = 6 * 1024 * 1024


def _pick(n, prefs):
    for p in prefs:
        if n % p == 0:
            return p
    return n


def _cparams(sem):
    return pltpu.CompilerParams(dimension_semantics=sem, vmem_limit_bytes=VMEM_LIMIT)


def _dot(a, b):
    return jnp.dot(a, b, preferred_element_type=F32)


def _dot_nt(a, b):
    return lax.dot_general(a, b, (((1,), (1,)), ((), ())), preferred_element_type=F32)


def _silu(x):
    return x * jax.nn.sigmoid(x)


def _adaln_kernel(c_ref, w_ref, b_ref, o_ref):
    a = _silu(c_ref[...]).astype(BF16)
    o_ref[...] = _dot(a, w_ref[...].astype(BF16)) + b_ref[...]


def _adaln(cond8, ada_w, ada_b):
    depth, d, n = ada_w.shape
    tn = _pick(n, (512, 256, 128))
    return pl.pallas_call(
        _adaln_kernel,
        grid=(depth, n // tn),
        in_specs=[pl.BlockSpec((8, d), lambda l, j: (0, 0)),
                  pl.BlockSpec((None, d, tn), lambda l, j: (l, 0, j)),
                  pl.BlockSpec((None, 1, tn), lambda l, j: (l, 0, j))],
        out_specs=pl.BlockSpec((None, 8, tn), lambda l, j: (l, 0, j)),
        out_shape=jax.ShapeDtypeStruct((depth, 8, n), F32),
        compiler_params=_cparams(("arbitrary", "arbitrary")),
        name="adaln",
    )(cond8, ada_w, ada_b.reshape(depth, 1, n))


def _rms(xf):
    return xf * lax.rsqrt(jnp.mean(xf * xf, axis=-1, keepdims=True) + EPS)


def _prenorm_kernel(xl_ref, xc_ref, g_ref, mod_ref, xs_ref, h_ref, *, n_lat_blocks, d):
    is_ctx = pl.program_id(0) >= n_lat_blocks
    x = jnp.where(is_ctx, xc_ref[...], xl_ref[...])
    mod = jnp.where(is_ctx, mod_ref[1:2, :], mod_ref[0:1, :])
    sh, sc = mod[:, :d], mod[:, d:2 * d]
    y = _rms(x) * g_ref[...]
    xs_ref[...] = x
    h_ref[...] = (y * (1.0 + sc) + sh).astype(h_ref.dtype)


def _prenorm(x_lat, x_ctx, g, mod):
    n_lat, d = x_lat.shape
    m = n_lat + x_ctx.shape[0]
    tm = SEG_TILE
    nlb = n_lat // tm
    row_spec = pl.BlockSpec((tm, d), lambda i: (i, 0))
    return pl.pallas_call(
        functools.partial(_prenorm_kernel, n_lat_blocks=nlb, d=d),
        grid=(m // tm,),
        in_specs=[pl.BlockSpec((tm, d), lambda i: (jnp.minimum(i, nlb - 1), 0)),
                  pl.BlockSpec((tm, d), lambda i: (jnp.maximum(i - nlb, 0), 0)),
                  pl.BlockSpec((1, d), lambda i: (0, 0)),
                  pl.BlockSpec((8, 3 * d), lambda i: (0, 0))],
        out_specs=[row_spec, row_spec],
        out_shape=[jax.ShapeDtypeStruct((m, d), F32), jax.ShapeDtypeStruct((m, d), BF16)],
        compiler_params=_cparams(("arbitrary",)),
        name="prenorm",
    )(x_lat, x_ctx, g.reshape(1, d), mod)


def _residual_kernel(x_ref, y_ref, g_ref, mod_ref, *rest, n_lat_blocks, d, with_next):
    is_ctx = pl.program_id(0) >= n_lat_blocks
    mod = jnp.where(is_ctx, mod_ref[1:2, :], mod_ref[0:1, :])
    gt = mod[:, 2 * d:]
    y = _rms(y_ref[...].astype(F32)) * g_ref[...]
    x_new = x_ref[...] + gt * y
    if not with_next:
        rest[0][...] = x_new
        return
    gn_ref, modn_ref, o_ref, h_ref = rest
    o_ref[...] = x_new
    modn = jnp.where(is_ctx, modn_ref[1:2, :], modn_ref[0:1, :])
    h = _rms(x_new) * gn_ref[...]
    h_ref[...] = (h * (1.0 + modn[:, d:2 * d]) + modn[:, :d]).astype(h_ref.dtype)


def _residual(x, y, g, mod, n_lat, rows, g_next=None, mod_next=None):
    d = x.shape[1]
    tm = SEG_TILE
    with_next = g_next is not None
    row_spec = pl.BlockSpec((tm, d), lambda i: (i, 0))
    vec_spec = pl.BlockSpec((1, d), lambda i: (0, 0))
    mod_spec = pl.BlockSpec((8, 3 * d), lambda i: (0, 0))
    in_specs = [row_spec, row_spec, vec_spec, mod_spec]
    args = [x, y, g.reshape(1, d), mod]
    out_specs, out_shape = row_spec, jax.ShapeDtypeStruct((rows, d), F32)
    if with_next:
        in_specs += [vec_spec, mod_spec]
        args += [g_next.reshape(1, d), mod_next]
        out_specs = [row_spec, row_spec]
        out_shape = [out_shape, jax.ShapeDtypeStruct((rows, d), BF16)]
    return pl.pallas_call(
        functools.partial(_residual_kernel, n_lat_blocks=n_lat // tm, d=d, with_next=with_next),
        grid=(rows // tm,),
        in_specs=in_specs,
        out_specs=out_specs,
        out_shape=out_shape,
        compiler_params=_cparams(("arbitrary",)),
        name="residual",
    )(*args)


def _mm_kernel(a_ref, w_ref, *rest, nk, has_gain, epilogue, n_extra):
    pos = 0
    g_ref = rest[pos] if has_gain else None
    pos += int(has_gain)
    extra = rest[pos:pos + n_extra]
    pos += n_extra
    o_ref = rest[pos]
    acc_ref = rest[pos + 1] if nk > 1 else None

    if has_gain:
        an_ref = rest[pos + 1]

        @pl.when(pl.program_id(1) == 0)
        def _():
            an_ref[...] = (_rms(a_ref[...].astype(F32)) * g_ref[...]).astype(BF16)

        a = an_ref[...]
    else:
        a = a_ref[...]
    r = _dot(a, w_ref[...].astype(BF16))

    def finish(res):
        if epilogue is not None:
            res = epilogue(res, *[e[...] for e in extra])
        o_ref[...] = res.astype(o_ref.dtype)

    if nk == 1:
        finish(r)
    else:
        k = pl.program_id(2)

        @pl.when(k == 0)
        def _():
            acc_ref[...] = r

        @pl.when(k > 0)
        def _():
            acc_ref[...] += r

        @pl.when(k == nk - 1)
        def _():
            finish(acc_ref[...])


def _matmul(a, w, *, rows=None, a_col=0, k=None, out_dtype=BF16, gain=None,
            epilogue=None, extras=(), tn=None, tm=None, out_widen=1, w_cols=None):
    if w_cols is not None:
        layer, col0, n = w_cols
        k = w.shape[1]
    else:
        k = w.shape[0] if k is None else k
        n = w.shape[1]
    rows = a.shape[0] if rows is None else rows
    tk = k if k <= MM_MAX_TK else _pick(k, (4096, 2048, 1024, 512))
    nk = k // tk
    tm = tm or _pick(rows, [t for t in (1408, 1024, 768, 512, 256, 128, 64, 32, 16)
                            if t * tk <= MM_A_TILE_ELEMS] or [8])
    tn = tn or _pick(n, [t for t in (1024, 512, 256, 128) if tk * t <= MM_W_TILE_ELEMS] or [128])
    assert a_col % tk == 0 and (gain is None or nk == 1)
    a_cb = a_col // tk
    if w_cols is None:
        w_spec = pl.BlockSpec((tk, tn), lambda i, j, kk: (kk, j))
    else:
        assert col0 % tn == 0
        w_spec = pl.BlockSpec((None, tk, tn), lambda i, j, kk: (layer, kk, col0 // tn + j))
    in_specs = [pl.BlockSpec((tm, tk), lambda i, j, kk: (i, a_cb + kk)), w_spec]
    args = [a, w]
    if gain is not None:
        in_specs.append(pl.BlockSpec((1, tk), lambda i, j, kk: (0, 0)))
        args.append(gain.reshape(1, tk).astype(F32))
    for arr, cols, cb in extras:
        in_specs.append(pl.BlockSpec((tm, cols), lambda i, j, kk, cb=cb: (i, cb(j))))
        args.append(arr)
    scratch = [pltpu.VMEM((tm, tn), F32)] if nk > 1 else []
    if gain is not None:
        scratch = [pltpu.VMEM((tm, tk), BF16)]
    return pl.pallas_call(
        functools.partial(_mm_kernel, nk=nk, has_gain=gain is not None,
                          epilogue=epilogue, n_extra=len(extras)),
        grid=(rows // tm, n // tn, nk),
        in_specs=in_specs,
        out_specs=pl.BlockSpec((tm, tn * out_widen), lambda i, j, kk: (i, j)),
        out_shape=jax.ShapeDtypeStruct((rows, n * out_widen), out_dtype),
        scratch_shapes=scratch,
        compiler_params=_cparams(("arbitrary", "arbitrary", "arbitrary")),
        name="matmul",
    )(*args)


def _rope_tables(n_lat, n_ctx):
    rows = n_lat // GRID_W
    row = jnp.repeat(jnp.arange(rows, dtype=F32), GRID_W)
    col = jnp.tile(jnp.arange(GRID_W, dtype=F32), rows)
    n_freq = MLA_ROPE // 4
    inv_freq = ROPE_BASE ** (-jnp.arange(n_freq, dtype=F32) / n_freq)
    ang = jnp.concatenate([row[:, None] * inv_freq, col[:, None] * inv_freq], axis=-1)
    ang = jnp.concatenate([ang, jnp.zeros((n_ctx, ang.shape[1]), F32)], axis=0)
    cos, sin = jnp.cos(ang), jnp.sin(ang)
    ta = jnp.concatenate([cos, sin, cos, -sin], axis=1)
    tb = jnp.concatenate([-sin, cos, sin, cos], axis=1)
    tq = jnp.concatenate([jnp.ones((ang.shape[0], HEAD), F32), ta], axis=1)
    return tq, ta, tb


def _q_epilogue(t, tq):
    reps = t.shape[1] // MLA_QK_PAD
    return t * (jnp.concatenate([tq] * reps, axis=1) if reps > 1 else tq)


def _k_epilogue(t, kr):
    krf = kr.astype(F32)
    parts = []
    for hh in range(t.shape[1] // HEAD):
        parts += [t[:, hh * HEAD:(hh + 1) * HEAD], krf]
    return jnp.concatenate(parts, axis=1)


def _ckv_epilogue(t, ta, tb):
    r = MLA_KV_RANK
    a, b = t[:, r:r + HEAD], t[:, r + HEAD:]
    return jnp.concatenate([t[:, :r], a * ta + b * tb, b], axis=1)


def _flash_kernel(q_ref, k_ref, v_ref, z_ref, o_ref, *, bounds):
    q = q_ref[...]
    tq = q.shape[0]
    nc = len(bounds) - 1
    m = jnp.full((tq, 1), -jnp.inf, F32)
    acc = jnp.zeros((tq, 2 * HEAD), F32)
    s = _dot_nt(q, k_ref[bounds[0]:bounds[1], :])
    for c in range(nc):
        lo, hi = bounds[c], bounds[c + 1]
        s_next = _dot_nt(q, k_ref[hi:bounds[c + 2], :]) if c + 1 < nc else None
        m_new = jnp.maximum(m, jnp.max(s, axis=-1, keepdims=True))
        p = jnp.exp2(s - m_new)
        alpha = jnp.exp2(m - m_new)
        ones_col = (lax.broadcasted_iota(jnp.int32, (hi - lo, HEAD), 1) == 0).astype(BF16)
        v_ext = jnp.concatenate([v_ref[lo:hi, :], ones_col], axis=1)
        acc = alpha * acc + _dot(p.astype(BF16), v_ext)
        m, s = m_new, s_next
    o = acc[:, :HEAD] / acc[:, HEAD:HEAD + 1]
    o_ref[...] = (o * _silu(z_ref[...].astype(F32))).astype(o_ref.dtype)


def _flash(q, k, v, zsrc, z_col, og_prev, *, q_row0, n_q, k_row0, n_k, total_rows):
    heads = v.shape[1] // HEAD
    tq = _pick(n_q, (1024, 512, 256, 128))
    tk = _pick(n_k, (2816, 768, 512, 256, 128))
    bounds = tuple(range(0, n_k + 1, tk))
    assert q_row0 % tq == 0 and k_row0 % n_k == 0
    qb0, kb0, zcb = q_row0 // tq, k_row0 // n_k, z_col // HEAD
    in_specs = [pl.BlockSpec((tq, MLA_QK_PAD), lambda h, i: (qb0 + i, h)),
                pl.BlockSpec((n_k, MLA_QK_PAD), lambda h, i: (kb0, h)),
                pl.BlockSpec((n_k, HEAD), lambda h, i: (kb0, h)),
                pl.BlockSpec((tq, HEAD), lambda h, i: (qb0 + i, zcb + h))]
    args = [q, k, v, zsrc]
    aliases = {}
    if og_prev is not None:
        in_specs.append(pl.BlockSpec(memory_space=pl.ANY))
        args.append(og_prev)
        aliases = {4: 0}

    def kern(q_ref, k_ref, v_ref, z_ref, *rest):
        _flash_kernel(q_ref, k_ref, v_ref, z_ref, rest[-1], bounds=bounds)

    return pl.pallas_call(
        kern,
        grid=(heads, n_q // tq),
        in_specs=in_specs,
        out_specs=pl.BlockSpec((tq, HEAD), lambda h, i: (qb0 + i, h)),
        out_shape=jax.ShapeDtypeStruct((total_rows, heads * HEAD), BF16),
        input_output_aliases=aliases,
        compiler_params=_cparams(("arbitrary", "arbitrary")),
        name="mla_flash",
    )(*args)


def _mla_weights(w_in, w_q_up, w_kv_up):
    d = w_in.shape[0]
    q_rank = d // 4
    heads = d // HEAD
    r0, r1, r2 = q_rank, q_rank + MLA_KV_RANK, q_rank + MLA_KV_RANK + MLA_ROPE
    half = MLA_ROPE // 2
    assert MLA_QK_PAD == HEAD + 2 * MLA_ROPE
    scale = (HEAD + MLA_ROPE) ** -0.5 * math.log2(math.e)
    w_z = w_in[:, r2:].astype(BF16)
    k1, k2 = w_in[:, r1:r1 + half], w_in[:, r1 + half:r2]
    w_ckv = jnp.concatenate([w_in[:, r0:r1], k1, k1, k2, k2, k2, k2, k1, k1], axis=1).astype(BF16)
    wq = (w_q_up * scale).reshape(q_rank, heads, HEAD + MLA_ROPE)
    x1, x2 = wq[:, :, HEAD:HEAD + half], wq[:, :, HEAD + half:]
    wq = jnp.concatenate([wq[:, :, :HEAD], x1, x1, x2, x2], axis=2)
    wq = wq.reshape(q_rank, heads * MLA_QK_PAD).astype(BF16)
    wkv = w_kv_up.reshape(MLA_KV_RANK, heads, 2 * HEAD)
    wk = wkv[:, :, :HEAD].reshape(MLA_KV_RANK, heads * HEAD).astype(BF16)
    wv = wkv[:, :, HEAD:].reshape(MLA_KV_RANK, heads * HEAD).astype(BF16)
    return w_z, w_ckv, wq, wk, wv


def _mla_mixer(h, tables, layer, w_in_all, g_q, w_q_up, g_kv, w_kv_up, w_out_all, n_lat, need_ctx):
    m, d = h.shape
    n_ctx = m - n_lat
    q_rank = d // 4
    tq, ta, tb = tables
    w_z, w_ckv, wq, wk, wv = _mla_weights(w_in_all[layer], w_q_up, w_kv_up)
    first = lambda j: 0

    cq = _matmul(h, w_in_all[layer][:, :q_rank].astype(BF16))
    z = _matmul(h, w_z)
    ckvkr = _matmul(h, w_ckv, epilogue=_ckv_epilogue, tn=w_ckv.shape[1],
                    extras=[(ta, HEAD, first), (tb, HEAD, first)])
    q = _matmul(cq, wq, gain=g_q, epilogue=_q_epilogue, extras=[(tq, MLA_QK_PAD, first)])
    kr_cb = MLA_KV_RANK // HEAD
    kk = _matmul(ckvkr, wk, k=MLA_KV_RANK, gain=g_kv, epilogue=_k_epilogue,
                 extras=[(ckvkr, HEAD, lambda j: kr_cb)], out_widen=MLA_QK_PAD // HEAD)
    v = _matmul(ckvkr, wv, k=MLA_KV_RANK, gain=g_kv)

    og = _flash(q, kk, v, z, 0, None, q_row0=0, n_q=n_lat, k_row0=0, n_k=m, total_rows=m)
    if need_ctx:
        og = _flash(q, kk, v, z, 0, og, q_row0=n_lat, n_q=n_ctx, k_row0=n_lat, n_k=n_ctx,
                    total_rows=m)
    rows = m if need_ctx else n_lat
    return _matmul(og, w_out_all, rows=rows, w_cols=(layer, 0, d))


def _fill_window(scr, prev_ref, cur_ref, next_ref, n_lat_blocks):
    tm = cur_ref.shape[0]
    i = pl.program_id(0)
    first = jnp.logical_or(i == 0, i == n_lat_blocks)
    last = jnp.logical_or(i == n_lat_blocks - 1, i == pl.num_programs(0) - 1)
    scr[0:HALO, :] = jnp.where(first, 0.0, prev_ref[...].astype(F32))
    scr[HALO:HALO + tm, :] = cur_ref[...].astype(F32)
    scr[HALO + tm:, :] = jnp.where(last, 0.0, next_ref[...].astype(F32))


def _seg_position(tm, n_lat_blocks, n_lat, n_ctx):
    i = pl.program_id(0)
    is_ctx = i >= n_lat_blocks
    local0 = jnp.where(is_ctx, i - n_lat_blocks, i) * tm
    t = local0 + lax.broadcasted_iota(jnp.int32, (tm, 1), 0)
    return t, jnp.where(is_ctx, n_ctx, n_lat)


def _window_specs(tm, cols, n_rows, col_fn):
    hb = tm // HALO
    last = n_rows // HALO - 1
    return [pl.BlockSpec((HALO, cols), lambda i, j: (jnp.maximum(i * hb - 1, 0), col_fn(j))),
            pl.BlockSpec((tm, cols), lambda i, j: (i, col_fn(j))),
            pl.BlockSpec((HALO, cols), lambda i, j: (jnp.minimum((i + 1) * hb, last), col_fn(j)))]


def _gdn_conv_kernel(prev_ref, cur_ref, next_ref, w_ref, o_ref, scr, *, normalize,
                     n_scaled_blocks, n_lat_blocks):
    tm, tc = cur_ref.shape
    _fill_window(scr, prev_ref, cur_ref, next_ref, n_lat_blocks)
    r = GDN_CONV // 2
    acc = scr[HALO - r:HALO - r + tm, :] * w_ref[0:1, :]
    for j in range(1, GDN_CONV):
        acc = acc + scr[HALO + j - r:HALO + j - r + tm, :] * w_ref[j:j + 1, :]
    y = _silu(acc)
    if normalize:
        scale = jnp.where(pl.program_id(1) < n_scaled_blocks, HEAD ** -0.5, 1.0)
        parts = []
        for hh in range(tc // HEAD):
            u = y[:, hh * HEAD:(hh + 1) * HEAD]
            parts.append(u * (lax.rsqrt(jnp.sum(u * u, axis=-1, keepdims=True) + EPS) * scale))
        y = jnp.concatenate(parts, axis=1) if len(parts) > 1 else parts[0]
    o_ref[...] = y.astype(o_ref.dtype)


def _gdn_conv(src, conv_w, col0, ncols, normalize, n_scaled_cols, n_lat):
    m = src.shape[0]
    tm = SEG_TILE
    tc = _pick(math.gcd(ncols, n_scaled_cols) if n_scaled_cols else ncols, (2048, 1024, 512, 256, 128))
    assert ncols % tc == 0 and n_scaled_cols % tc == 0 and col0 % tc == 0
    cb0 = col0 // tc
    specs = _window_specs(tm, tc, m, lambda j: cb0 + j)
    specs.append(pl.BlockSpec((8, tc), lambda i, j: (0, cb0 + j)))
    w8 = jnp.concatenate([conv_w, jnp.zeros((8 - GDN_CONV, conv_w.shape[1]), F32)], axis=0)
    return pl.pallas_call(
        functools.partial(_gdn_conv_kernel, normalize=normalize,
                          n_scaled_blocks=n_scaled_cols // tc, n_lat_blocks=n_lat // tm),
        grid=(m // tm, ncols // tc),
        in_specs=specs,
        out_specs=pl.BlockSpec((tm, tc), lambda i, j: (i, j)),
        out_shape=jax.ShapeDtypeStruct((m, ncols), BF16),
        scratch_shapes=[pltpu.VMEM((tm + 2 * HALO, tc), F32)],
        compiler_params=_cparams(("arbitrary", "arbitrary")),
        name="gdn_conv",
    )(src, src, src, w8)


def _split3(x):
    x1 = x.astype(BF16)
    r1 = x - x1.astype(F32)
    x2 = r1.astype(BF16)
    x3 = (r1 - x2.astype(F32)).astype(BF16)
    return x1, x2, x3


def _gdn_gate_kernel(ab_ref, alog_ref, dtb_ref, g_ref, beta_ref):
    tm, w = g_ref.shape
    ab = ab_ref[...]
    x = ab[:, :w] + dtb_ref[...]
    softplus = jnp.maximum(x, 0.0) + jnp.log1p(jnp.exp(-jnp.abs(x)))
    g = -jnp.exp(alog_ref[...]) * softplus
    beta_ref[...] = jax.nn.sigmoid(ab[:, w:])
    ch = GDN_CHUNK
    ri = lax.broadcasted_iota(jnp.int32, (ch, ch), 0)
    ci = lax.broadcasted_iota(jnp.int32, (ch, ch), 1)
    lower = (ci <= ri).astype(BF16)
    upper = (ci >= ri).astype(BF16)
    fwd_lane = lax.broadcasted_iota(jnp.int32, (ch, w), 1) < w // 2
    for cidx in range(tm // ch):
        parts = _split3(g[cidx * ch:(cidx + 1) * ch, :])
        pre = _dot(lower, parts[0]) + (_dot(lower, parts[1]) + _dot(lower, parts[2]))
        suf = _dot(upper, parts[0]) + (_dot(upper, parts[1]) + _dot(upper, parts[2]))
        g_ref[cidx * ch:(cidx + 1) * ch, :] = jnp.where(fwd_lane, pre, suf)


def _gdn_gates(ab, a_log, dt_bias):
    m, w4 = ab.shape
    w = w4 // 2
    tm = SEG_TILE
    return pl.pallas_call(
        _gdn_gate_kernel,
        grid=(m // tm,),
        in_specs=[pl.BlockSpec((tm, w4), lambda i: (i, 0)),
                  pl.BlockSpec((1, w), lambda i: (0, 0)),
                  pl.BlockSpec((1, w), lambda i: (0, 0))],
        out_specs=[pl.BlockSpec((tm, w), lambda i: (i, 0)),
                   pl.BlockSpec((tm, w), lambda i: (i, 0))],
        out_shape=[jax.ShapeDtypeStruct((m, w), F32), jax.ShapeDtypeStruct((m, w), F32)],
        compiler_params=_cparams(("arbitrary",)),
        name="gdn_gates",
    )(ab, a_log.reshape(1, w).astype(F32), dt_bias.reshape(1, w).astype(F32))


def _bmm(a, b):
    return lax.dot_general(a, b, (((2,), (1,)), ((0,), (0,))), preferred_element_type=F32)


def _unit_tri_inverse(a, ri, ci):
    def same(s):
        return (ri // s) == (ci // s)

    t = (ri == ci).astype(F32) - jnp.where(same(2), a, 0.0)
    s = 2
    while s < GDN_CHUNK:
        coupling = jnp.logical_and(same(2 * s), jnp.logical_not(same(s)))
        tb = t.astype(BF16)
        tl = _bmm(tb, jnp.where(coupling, a, 0.0).astype(BF16))
        t = t - _bmm(tl.astype(BF16), tb)
        s *= 2
    return t


def _gdn_kernel(q_ref, k_ref, v_ref, col_ref, row_ref, *rest, kg, fwd):
    o_ref, state = rest[-2:]
    ch = GDN_CHUNK
    p2 = 2 * ch

    @pl.when(pl.program_id(1) == 0)
    def _():
        state[...] = jnp.zeros_like(state)

    ri = lax.broadcasted_iota(jnp.int32, (1, p2, p2), 1)
    ci = lax.broadcasted_iota(jnp.int32, (1, p2, p2), 2)
    same_head = (ri // ch) == (ci // ch)
    ahead = ci - ri if fwd else ri - ci
    incl = jnp.logical_and(same_head, ahead <= 0)
    strict = jnp.logical_and(incl, ci != ri)
    first_lanes = lax.broadcasted_iota(jnp.int32, (1, 1, p2), 2) < ch

    def pair_rows(ref):
        x = jnp.stack([ref[:, p * HEAD:(p + 1) * HEAD] for p in range(kg)]).astype(F32)
        return jnp.concatenate([x, x], axis=1)

    def head_rows(ref):
        return jnp.stack([jnp.concatenate([ref[:, (2 * p) * HEAD:(2 * p + 1) * HEAD],
                                           ref[:, (2 * p + 1) * HEAD:(2 * p + 2) * HEAD]], axis=0)
                          for p in range(kg)]).astype(F32)

    k2 = pair_rows(k_ref)
    q2 = pair_rows(q_ref)
    vf = head_rows(v_ref)
    kt = jnp.stack([k2[p].T for p in range(kg)])
    kkqk = _bmm(jnp.concatenate([k2, q2], axis=1).astype(BF16), kt.astype(BF16))
    kk, qk = kkqk[:, :p2], kkqk[:, p2:]

    def col(j0):
        c = jnp.stack([col_ref[:, j0 + p:j0 + p + 1] for p in range(kg)])
        return jnp.broadcast_to(c, (kg, p2, p2))

    gcol, bcol = col(0), col(kg)
    grow = jnp.stack([row_ref[p:p + 1, :] for p in range(kg)])
    last = ch - 1 if fwd else 0
    tot_a = grow[:, :, last:last + 1]
    tot_b = grow[:, :, ch + last:ch + last + 1]
    tot_row = jnp.where(first_lanes, tot_a, tot_b)

    decay = jnp.exp(jnp.where(incl, gcol - grow, -jnp.inf))
    tinv = _unit_tri_inverse(jnp.where(strict, bcol * kk * decay, 0.0), ri, ci)

    eg = jnp.exp(gcol)
    rhs = jnp.concatenate([vf * bcol, k2 * (bcol * eg)], axis=2).astype(BF16)
    uw = _bmm(tinv.astype(BF16), rhs)
    u, w_b = uw[:, :, :HEAD], uw[:, :, HEAD:].astype(BF16)
    qk_d = jnp.where(incl, qk * decay, 0.0).astype(BF16)
    q_dec = (q2 * eg).astype(BF16)
    kt_dec = kt * jnp.exp(tot_row - grow)

    s_a = state[:, 0]
    s_b = state[:, 1]
    ws_a = _bmm(jnp.concatenate([w_b[:, :ch], q_dec[:, :ch]], axis=1), s_a.astype(BF16))
    ws_b = _bmm(jnp.concatenate([w_b[:, ch:], q_dec[:, ch:]], axis=1), s_b.astype(BF16))
    v_new = (u - jnp.concatenate([ws_a[:, :ch], ws_b[:, :ch]], axis=1)).astype(BF16)
    o = jnp.concatenate([ws_a[:, ch:], ws_b[:, ch:]], axis=1) + _bmm(qk_d, v_new)
    kd_a = jnp.where(first_lanes, kt_dec, 0.0).astype(BF16)
    kd_b = jnp.where(first_lanes, 0.0, kt_dec).astype(BF16)
    state[:, 0] = s_a * jnp.exp(tot_a) + _bmm(kd_a, v_new)
    state[:, 1] = s_b * jnp.exp(tot_b) + _bmm(kd_b, v_new)
    if len(rest) > 2:
        oprev_ref, z_ref, gn_ref = rest[:3]
        o = _rms(o + head_rows(oprev_ref)) * gn_ref[...] * _silu(head_rows(z_ref))
    for p in range(kg):
        o_ref[:, (2 * p) * HEAD:(2 * p + 1) * HEAD] = o[p, :ch].astype(o_ref.dtype)
        o_ref[:, (2 * p + 1) * HEAD:(2 * p + 2) * HEAD] = o[p, ch:].astype(o_ref.dtype)


def _gdn_gate_layouts(gcs, beta, kg):
    m, w = gcs.shape
    ch = GDN_CHUNK
    assert HEAD == 2 * ch
    nch, ngrp = m // ch, w // (4 * kg)
    g6 = gcs.reshape(nch, ch, 2, ngrp, kg, 2)
    b6 = beta.reshape(nch, ch, 2, ngrp, kg, 2)
    to_cols = lambda t: t.transpose(2, 3, 0, 5, 1, 4).reshape(2, ngrp, nch, 2 * ch, kg)
    cols = jnp.concatenate([to_cols(g6), to_cols(b6)], axis=4)
    rows = g6.transpose(2, 3, 0, 4, 5, 1).reshape(2, ngrp, nch, kg, 2 * ch)
    return cols, rows


def _gdn_scan(qk, v, cols, rows, n_lat, kg, fwd, fused=None):
    m = v.shape[0]
    vh = v.shape[1] // HEAD
    ch = GDN_CHUNK
    nch = m // ch
    lat_ch = n_lat // ch
    ngrp = vh // (2 * kg)
    nv = 2 * kg
    d = 0 if fwd else 1

    def chunk(n):
        return (n + lat_ch) % nch if fwd else nch - 1 - n

    head_spec = pl.BlockSpec((ch, nv * HEAD), lambda g, n: (chunk(n), g))
    in_specs = [pl.BlockSpec((ch, kg * HEAD), lambda g, n: (chunk(n), g)),
                pl.BlockSpec((ch, kg * HEAD), lambda g, n: (chunk(n), ngrp + g)),
                head_spec,
                pl.BlockSpec((None, None, None, 2 * ch, 2 * kg), lambda g, n: (d, g, chunk(n), 0, 0)),
                pl.BlockSpec((None, None, None, kg, 2 * ch), lambda g, n: (d, g, chunk(n), 0, 0))]
    args = [qk, qk, v, cols, rows]
    if fused is not None:
        o_other, zsrc, z_col, g_norm = fused
        zcb = z_col // (nv * HEAD)
        in_specs += [head_spec,
                     pl.BlockSpec((ch, nv * HEAD), lambda g, n: (chunk(n), zcb + g)),
                     pl.BlockSpec((1, HEAD), lambda g, n: (0, 0))]
        args += [o_other, zsrc, g_norm.reshape(1, HEAD).astype(F32)]
    return pl.pallas_call(
        functools.partial(_gdn_kernel, kg=kg, fwd=fwd),
        grid=(ngrp, nch),
        in_specs=in_specs,
        out_specs=head_spec,
        out_shape=jax.ShapeDtypeStruct((m, vh * HEAD), BF16),
        scratch_shapes=[pltpu.VMEM((kg, 2, HEAD, HEAD), F32)],
        compiler_params=_cparams(("arbitrary", "arbitrary")),
        name="gdn_scan",
    )(*args)


def _gdn_mixer(h, layer, w_in_all, conv_w, a_log, dt_bias, g_norm, w_out_all, n_lat):
    m, d = h.shape
    khs = d // HEAD
    vhs = 2 * khs
    kw, vw = khs * HEAD, vhs * HEAD
    qkv_w = 2 * kw + vw
    n_ab = 4 * vhs
    if n_ab % 256 == 0 and qkv_w % n_ab == 0:
        qkv = _matmul(h, w_in_all, w_cols=(layer, 0, qkv_w))
        ab = _matmul(h, w_in_all, out_dtype=F32, tn=n_ab, w_cols=(layer, qkv_w, n_ab))
        z_src = _matmul(h, w_in_all, tn=n_ab, w_cols=(layer, qkv_w + n_ab, vw))
        z_col = 0
    else:
        w_in = w_in_all[layer]
        w_big = jnp.concatenate([w_in[:, :qkv_w], w_in[:, qkv_w + n_ab:]], axis=1).astype(BF16)
        qkv = z_src = _matmul(h, w_big)
        z_col = qkv_w
        ab = _matmul(h, w_in[:, qkv_w:qkv_w + n_ab].astype(BF16), out_dtype=F32, tn=n_ab)
    qk = _gdn_conv(qkv, conv_w, 0, 2 * kw, True, kw, n_lat)
    v = _gdn_conv(qkv, conv_w, 2 * kw, vw, False, 0, n_lat)
    gcs, beta = _gdn_gates(ab, a_log, dt_bias)
    kg = _pick(khs, (GDN_PAIRS, 4, 2, 1))
    cols, rows = _gdn_gate_layouts(gcs, beta, kg)
    o_fwd = _gdn_scan(qk, v, cols, rows, n_lat, kg, True)
    y = _gdn_scan(qk, v, cols, rows, n_lat, kg, False, fused=(o_fwd, z_src, z_col, g_norm))
    return _matmul(y, w_out_all, w_cols=(layer, 0, d))


def _pool_kernel(prev_ref, cur_ref, next_ref, w_ref, sc_ref, z_ref, o_ref, scr, *,
                 n_lat_blocks, n_lat, n_ctx):
    tm, tc = cur_ref.shape
    _fill_window(scr, prev_ref, cur_ref, next_ref, n_lat_blocks)
    t, seg_len = _seg_position(tm, n_lat_blocks, n_lat, n_ctx)
    for g, window in enumerate(POOL_WINDOWS):
        @pl.when(pl.program_id(1) == g)
        def _(radius=window // 2):
            u = scr[HALO:HALO + tm, :]
            acc = u
            for dlt in range(1, radius + 1):
                acc = acc + (scr[HALO - dlt:HALO - dlt + tm, :] + scr[HALO + dlt:HALO + dlt + tm, :])
            cnt = jnp.minimum(t + radius + 1, seg_len) - jnp.maximum(t - radius, 0)
            mean_minus = acc / cnt.astype(F32) - u
            y = _dot(mean_minus.astype(BF16), w_ref[...]) * sc_ref[...]
            o_ref[...] = (y * _silu(z_ref[...].astype(F32))).astype(o_ref.dtype)


def _pool_mixer(h, layer, w_in_all, w_grp, scale, w_out_all, n_lat):
    m, d = h.shape
    ng, gw, _ = w_grp.shape
    assert tuple(w // 2 for w in POOL_WINDOWS) == tuple(2 ** g for g in range(ng))
    uz = _matmul(h, w_in_all, w_cols=(layer, 0, 2 * d))
    tm = SEG_TILE
    specs = _window_specs(tm, gw, m, lambda g: g)
    specs += [pl.BlockSpec((None, gw, gw), lambda i, g: (g, 0, 0)),
              pl.BlockSpec((1, gw), lambda i, g: (0, g)),
              pl.BlockSpec((tm, gw), lambda i, g: (i, ng + g))]
    yg = pl.pallas_call(
        functools.partial(_pool_kernel, n_lat_blocks=n_lat // tm, n_lat=n_lat, n_ctx=m - n_lat),
        grid=(m // tm, ng),
        in_specs=specs,
        out_specs=pl.BlockSpec((tm, gw), lambda i, g: (i, g)),
        out_shape=jax.ShapeDtypeStruct((m, d), BF16),
        scratch_shapes=[pltpu.VMEM((tm + 2 * HALO, gw), F32)],
        compiler_params=_cparams(("arbitrary", "arbitrary")),
        name="pool_group",
    )(uz, uz, uz, w_grp.astype(BF16), scale.reshape(1, d).astype(F32), uz)
    return _matmul(yg, w_out_all, w_cols=(layer, 0, d))


def kernel(x, c, ctx, c_ctx, ada_w, ada_b, norm_pre, norm_post, mla_w_in, mla_g_q, mla_w_q_up,
           mla_g_kv, mla_w_kv_up, mla_w_out, gdn_w_in, gdn_conv_w, gdn_a_log, gdn_dt_bias,
           gdn_g_norm, gdn_w_out, pool_w_in, pool_w_grp, pool_scale, pool_w_out):
    assert x.shape[0] == 1 and ctx.shape[0] == 1
    n_lat, d = x.shape[1], x.shape[2]
    n_ctx = ctx.shape[1]
    depth = ada_w.shape[0]
    n_mixers = 3
    assert n_lat % SEG_TILE == 0 and n_ctx % SEG_TILE == 0

    cond8 = jnp.concatenate([c.reshape(1, d), c_ctx.reshape(1, d), jnp.zeros((6, d), F32)], axis=0)
    mods = _adaln(cond8, ada_w, ada_b)
    tables = _rope_tables(n_lat, n_ctx)
    xs, h = _prenorm(x[0], ctx[0], norm_pre[0], mods[0])

    counts = [0, 0, 0]
    for i in range(depth):
        kind = i % n_mixers
        j = counts[kind]
        counts[kind] += 1
        need_ctx = i < depth - 1
        if kind == 0:
            y = _mla_mixer(h, tables, j, mla_w_in, mla_g_q[j], mla_w_q_up[j], mla_g_kv[j],
                           mla_w_kv_up[j], mla_w_out, n_lat, need_ctx)
        elif kind == 1:
            y = _gdn_mixer(h, j, gdn_w_in, gdn_conv_w[j], gdn_a_log[j], gdn_dt_bias[j],
                           gdn_g_norm[j], gdn_w_out, n_lat)
        else:
            y = _pool_mixer(h, j, pool_w_in, pool_w_grp[j], pool_scale[j], pool_w_out, n_lat)
        if need_ctx:
            xs, h = _residual(xs, y, norm_post[i], mods[i], n_lat, xs.shape[0],
                              norm_pre[i + 1], mods[i + 1])
        else:
            xs = _residual(xs, y, norm_post[i], mods[i], n_lat, n_lat)
    return xs[None]
```

```python
import functools
import math

import jax
import jax.numpy as jnp
from jax import lax
from jax.experimental import pallas as pl
from jax.experimental.pallas import tpu as pltpu

F32 = jnp.float32
BF16 = jnp.bfloat16
EPS = 1e-6

HEAD = 128
MLA_ROPE = 64
MLA_KV_RANK = 512
MLA_QK_PAD = 256
ROPE_BASE = 10000.0
GRID_W = 64
GDN_CONV = 5
GDN_CHUNK = 64
GDN_PAIRS = 16
POOL_WINDOWS = (2, 4, 8, 16)
SEG_TILE = 256
HALO = 16
VMEM_LIMIT = 56 * 1024 * 1024
MM_MAX_TK = 8192
MM_W_TILE_ELEMS = 2 * 1024 * 1024
MM_A_TILE_ELEMS = 6 * 1024 * 1024


def _pick(n, prefs):
    for p in prefs:
        if n % p == 0:
            return p
    return n


def _cparams(sem):
    return pltpu.CompilerParams(dimension_semantics=sem, vmem_limit_bytes=VMEM_LIMIT)


def _dot(a, b):
    return jnp.dot(a, b, preferred_element_type=F32)


def _dot_nt(a, b):
    return lax.dot_general(a, b, (((1,), (1,)), ((), ())), preferred_element_type=F32)


def _silu(x):
    return x * jax.nn.sigmoid(x)


def _adaln_kernel(c_ref, w_ref, b_ref, o_ref):
    a = _silu(c_ref[...]).astype(BF16)
    o_ref[...] = _dot(a, w_ref[...].astype(BF16)) + b_ref[...]


def _adaln(cond8, ada_w, ada_b):
    depth, d, n = ada_w.shape
    tn = _pick(n, (512, 256, 128))
    return pl.pallas_call(
        _adaln_kernel,
        grid=(depth, n // tn),
        in_specs=[pl.BlockSpec((8, d), lambda l, j: (0, 0)),
                  pl.BlockSpec((None, d, tn), lambda l, j: (l, 0, j)),
                  pl.BlockSpec((None, 1, tn), lambda l, j: (l, 0, j))],
        out_specs=pl.BlockSpec((None, 8, tn), lambda l, j: (l, 0, j)),
        out_shape=jax.ShapeDtypeStruct((depth, 8, n), F32),
        compiler_params=_cparams(("arbitrary", "arbitrary")),
        name="adaln",
    )(cond8, ada_w, ada_b.reshape(depth, 1, n))


def _rms(xf):
    return xf * lax.rsqrt(jnp.mean(xf * xf, axis=-1, keepdims=True) + EPS)


def _prenorm_kernel(xl_ref, xc_ref, g_ref, mod_ref, xs_ref, h_ref, *, n_lat_blocks, d):
    is_ctx = pl.program_id(0) >= n_lat_blocks
    x = jnp.where(is_ctx, xc_ref[...], xl_ref[...])
    mod = jnp.where(is_ctx, mod_ref[1:2, :], mod_ref[0:1, :])
    sh, sc = mod[:, :d], mod[:, d:2 * d]
    y = _rms(x) * g_ref[...]
    xs_ref[...] = x
    h_ref[...] = (y * (1.0 + sc) + sh).astype(h_ref.dtype)


def _prenorm(x_lat, x_ctx, g, mod):
    n_lat, d = x_lat.shape
    m = n_lat + x_ctx.shape[0]
    tm = SEG_TILE
    nlb = n_lat // tm
    row_spec = pl.BlockSpec((tm, d), lambda i: (i, 0))
    return pl.pallas_call(
        functools.partial(_prenorm_kernel, n_lat_blocks=nlb, d=d),
        grid=(m // tm,),
        in_specs=[pl.BlockSpec((tm, d), lambda i: (jnp.minimum(i, nlb - 1), 0)),
                  pl.BlockSpec((tm, d), lambda i: (jnp.maximum(i - nlb, 0), 0)),
                  pl.BlockSpec((1, d), lambda i: (0, 0)),
                  pl.BlockSpec((8, 3 * d), lambda i: (0, 0))],
        out_specs=[row_spec, row_spec],
        out_shape=[jax.ShapeDtypeStruct((m, d), F32), jax.ShapeDtypeStruct((m, d), BF16)],
        compiler_params=_cparams(("arbitrary",)),
        name="prenorm",
    )(x_lat, x_ctx, g.reshape(1, d), mod)


def _residual_kernel(x_ref, y_ref, g_ref, mod_ref, *rest, n_lat_blocks, d, with_next):
    is_ctx = pl.program_id(0) >= n_lat_blocks
    mod = jnp.where(is_ctx, mod_ref[1:2, :], mod_ref[0:1, :])
    gt = mod[:, 2 * d:]
    y = _rms(y_ref[...].astype(F32)) * g_ref[...]
    x_new = x_ref[...] + gt * y
    if not with_next:
        rest[0][...] = x_new
        return
    gn_ref, modn_ref, o_ref, h_ref = rest
    o_ref[...] = x_new
    modn = jnp.where(is_ctx, modn_ref[1:2, :], modn_ref[0:1, :])
    h = _rms(x_new) * gn_ref[...]
    h_ref[...] = (h * (1.0 + modn[:, d:2 * d]) + modn[:, :d]).astype(h_ref.dtype)


def _residual(x, y, g, mod, n_lat, rows, g_next=None, mod_next=None):
    d = x.shape[1]
    tm = SEG_TILE
    with_next = g_next is not None
    row_spec = pl.BlockSpec((tm, d), lambda i: (i, 0))
    vec_spec = pl.BlockSpec((1, d), lambda i: (0, 0))
    mod_spec = pl.BlockSpec((8, 3 * d), lambda i: (0, 0))
    in_specs = [row_spec, row_spec, vec_spec, mod_spec]
    args = [x, y, g.reshape(1, d), mod]
    out_specs, out_shape = row_spec, jax.ShapeDtypeStruct((rows, d), F32)
    if with_next:
        in_specs += [vec_spec, mod_spec]
        args += [g_next.reshape(1, d), mod_next]
        out_specs = [row_spec, row_spec]
        out_shape = [out_shape, jax.ShapeDtypeStruct((rows, d), BF16)]
    return pl.pallas_call(
        functools.partial(_residual_kernel, n_lat_blocks=n_lat // tm, d=d, with_next=with_next),
        grid=(rows // tm,),
        in_specs=in_specs,
        out_specs=out_specs,
        out_shape=out_shape,
        compiler_params=_cparams(("arbitrary",)),
        name="residual",
    )(*args)


def _mm_kernel(a_ref, w_ref, *rest, nk, has_gain, epilogue, n_extra):
    pos = 0
    g_ref = rest[pos] if has_gain else None
    pos += int(has_gain)
    extra = rest[pos:pos + n_extra]
    pos += n_extra
    o_ref = rest[pos]
    acc_ref = rest[pos + 1] if nk > 1 else None

    if has_gain:
        an_ref = rest[pos + 1]

        @pl.when(pl.program_id(1) == 0)
        def _():
            an_ref[...] = (_rms(a_ref[...].astype(F32)) * g_ref[...]).astype(BF16)

        a = an_ref[...]
    else:
        a = a_ref[...]
    r = _dot(a, w_ref[...].astype(BF16))

    def finish(res):
        if epilogue is not None:
            res = epilogue(res, *[e[...] for e in extra])
        o_ref[...] = res.astype(o_ref.dtype)

    if nk == 1:
        finish(r)
    else:
        k = pl.program_id(2)

        @pl.when(k == 0)
        def _():
            acc_ref[...] = r

        @pl.when(k > 0)
        def _():
            acc_ref[...] += r

        @pl.when(k == nk - 1)
        def _():
            finish(acc_ref[...])


def _matmul(a, w, *, rows=None, a_col=0, k=None, out_dtype=BF16, gain=None,
            epilogue=None, extras=(), tn=None, tm=None, out_widen=1, w_cols=None):
    if w_cols is not None:
        layer, col0, n = w_cols
        k = w.shape[1]
    else:
        k = w.shape[0] if k is None else k
        n = w.shape[1]
    rows = a.shape[0] if rows is None else rows
    tk = k if k <= MM_MAX_TK else _pick(k, (4096, 2048, 1024, 512))
    nk = k // tk
    tm = tm or _pick(rows, [t for t in (1408, 1024, 768, 512, 256, 128, 64, 32, 16)
                            if t * tk <= 2 * MM_A_TILE_ELEMS] or [8])
    a_mode = pl.Buffered(1) if tm * tk > MM_A_TILE_ELEMS and nk == 1 else None
    tn = tn or _pick(n, [t for t in (1024, 512, 256, 128) if tk * t <= MM_W_TILE_ELEMS] or [128])
    assert a_col % tk == 0 and (gain is None or nk == 1)
    a_cb = a_col // tk
    if w_cols is None:
        w_spec = pl.BlockSpec((tk, tn), lambda i, j, kk: (kk, j))
    else:
        assert col0 % tn == 0
        w_spec = pl.BlockSpec((None, tk, tn), lambda i, j, kk: (layer, kk, col0 // tn + j))
    in_specs = [pl.BlockSpec((tm, tk), lambda i, j, kk: (i, a_cb + kk), pipeline_mode=a_mode), w_spec]
    args = [a, w]
    if gain is not None:
        in_specs.append(pl.BlockSpec((1, tk), lambda i, j, kk: (0, 0)))
        args.append(gain.reshape(1, tk).astype(F32))
    for arr, cols, cb in extras:
        in_specs.append(pl.BlockSpec((tm, cols), lambda i, j, kk, cb=cb: (i, cb(j))))
        args.append(arr)
    scratch = [pltpu.VMEM((tm, tn), F32)] if nk > 1 else []
    if gain is not None:
        scratch = [pltpu.VMEM((tm, tk), BF16)]
    return pl.pallas_call(
        functools.partial(_mm_kernel, nk=nk, has_gain=gain is not None,
                          epilogue=epilogue, n_extra=len(extras)),
        grid=(rows // tm, n // tn, nk),
        in_specs=in_specs,
        out_specs=pl.BlockSpec((tm, tn * out_widen), lambda i, j, kk: (i, j)),
        out_shape=jax.ShapeDtypeStruct((rows, n * out_widen), out_dtype),
        scratch_shapes=scratch,
        compiler_params=_cparams(("arbitrary", "arbitrary", "arbitrary")),
        name="matmul",
    )(*args)


def _rope_tables(n_lat, n_ctx):
    rows = n_lat // GRID_W
    row = jnp.repeat(jnp.arange(rows, dtype=F32), GRID_W)
    col = jnp.tile(jnp.arange(GRID_W, dtype=F32), rows)
    n_freq = MLA_ROPE // 4
    inv_freq = ROPE_BASE ** (-jnp.arange(n_freq, dtype=F32) / n_freq)
    ang = jnp.concatenate([row[:, None] * inv_freq, col[:, None] * inv_freq], axis=-1)
    ang = jnp.concatenate([ang, jnp.zeros((n_ctx, ang.shape[1]), F32)], axis=0)
    cos, sin = jnp.cos(ang), jnp.sin(ang)
    ta = jnp.concatenate([cos, sin, cos, -sin], axis=1)
    tb = jnp.concatenate([-sin, cos, sin, cos], axis=1)
    tq = jnp.concatenate([jnp.ones((ang.shape[0], HEAD), F32), ta], axis=1)
    return tq, ta, tb


def _q_epilogue(t, tq):
    reps = t.shape[1] // MLA_QK_PAD
    return t * (jnp.concatenate([tq] * reps, axis=1) if reps > 1 else tq)


def _k_epilogue(t, kr):
    krf = kr.astype(F32)
    parts = []
    for hh in range(t.shape[1] // HEAD):
        parts += [t[:, hh * HEAD:(hh + 1) * HEAD], krf]
    return jnp.concatenate(parts, axis=1)


def _ckv_epilogue(t, ta, tb):
    r = MLA_KV_RANK
    a, b = t[:, r:r + HEAD], t[:, r + HEAD:]
    return jnp.concatenate([t[:, :r], a * ta + b * tb, b], axis=1)


def _flash_kernel(q_ref, k_ref, v_ref, z_ref, o_ref, *, bounds):
    q = q_ref[...]
    tq = q.shape[0]
    nc = len(bounds) - 1
    m = jnp.full((tq, 1), -jnp.inf, F32)
    acc = jnp.zeros((tq, 2 * HEAD), F32)
    s = _dot_nt(q, k_ref[bounds[0]:bounds[1], :])
    for c in range(nc):
        lo, hi = bounds[c], bounds[c + 1]
        s_next = _dot_nt(q, k_ref[hi:bounds[c + 2], :]) if c + 1 < nc else None
        m_new = jnp.maximum(m, jnp.max(s, axis=-1, keepdims=True))
        p = jnp.exp2(s - m_new)
        alpha = jnp.exp2(m - m_new)
        ones_col = (lax.broadcasted_iota(jnp.int32, (hi - lo, HEAD), 1) == 0).astype(BF16)
        v_ext = jnp.concatenate([v_ref[lo:hi, :], ones_col], axis=1)
        acc = alpha * acc + _dot(p.astype(BF16), v_ext)
        m, s = m_new, s_next
    o = acc[:, :HEAD] / acc[:, HEAD:HEAD + 1]
    o_ref[...] = (o * _silu(z_ref[...].astype(F32))).astype(o_ref.dtype)


def _flash(q, k, v, zsrc, z_col, og_prev, *, q_row0, n_q, k_row0, n_k, total_rows):
    heads = v.shape[1] // HEAD
    tq = _pick(n_q, (1024, 512, 256, 128))
    tk = _pick(n_k, (2816, 768, 512, 256, 128))
    bounds = tuple(range(0, n_k + 1, tk))
    assert q_row0 % tq == 0 and k_row0 % n_k == 0
    qb0, kb0, zcb = q_row0 // tq, k_row0 // n_k, z_col // HEAD
    in_specs = [pl.BlockSpec((tq, MLA_QK_PAD), lambda h, i: (qb0 + i, h)),
                pl.BlockSpec((n_k, MLA_QK_PAD), lambda h, i: (kb0, h)),
                pl.BlockSpec((n_k, HEAD), lambda h, i: (kb0, h)),
                pl.BlockSpec((tq, HEAD), lambda h, i: (qb0 + i, zcb + h))]
    args = [q, k, v, zsrc]
    aliases = {}
    if og_prev is not None:
        in_specs.append(pl.BlockSpec(memory_space=pl.ANY))
        args.append(og_prev)
        aliases = {4: 0}

    def kern(q_ref, k_ref, v_ref, z_ref, *rest):
        _flash_kernel(q_ref, k_ref, v_ref, z_ref, rest[-1], bounds=bounds)

    return pl.pallas_call(
        kern,
        grid=(heads, n_q // tq),
        in_specs=in_specs,
        out_specs=pl.BlockSpec((tq, HEAD), lambda h, i: (qb0 + i, h)),
        out_shape=jax.ShapeDtypeStruct((total_rows, heads * HEAD), BF16),
        input_output_aliases=aliases,
        compiler_params=_cparams(("arbitrary", "arbitrary")),
        name="mla_flash",
    )(*args)


def _mla_weights(w_in, w_q_up, w_kv_up):
    d = w_in.shape[0]
    q_rank = d // 4
    heads = d // HEAD
    r0, r1, r2 = q_rank, q_rank + MLA_KV_RANK, q_rank + MLA_KV_RANK + MLA_ROPE
    half = MLA_ROPE // 2
    assert MLA_QK_PAD == HEAD + 2 * MLA_ROPE
    scale = (HEAD + MLA_ROPE) ** -0.5 * math.log2(math.e)
    w_z = w_in[:, r2:].astype(BF16)
    k1, k2 = w_in[:, r1:r1 + half], w_in[:, r1 + half:r2]
    w_ckv = jnp.concatenate([w_in[:, r0:r1], k1, k1, k2, k2, k2, k2, k1, k1], axis=1).astype(BF16)
    wq = (w_q_up * scale).reshape(q_rank, heads, HEAD + MLA_ROPE)
    x1, x2 = wq[:, :, HEAD:HEAD + half], wq[:, :, HEAD + half:]
    wq = jnp.concatenate([wq[:, :, :HEAD], x1, x1, x2, x2], axis=2)
    wq = wq.reshape(q_rank, heads * MLA_QK_PAD).astype(BF16)
    wkv = w_kv_up.reshape(MLA_KV_RANK, heads, 2 * HEAD)
    wk = wkv[:, :, :HEAD].reshape(MLA_KV_RANK, heads * HEAD).astype(BF16)
    wv = wkv[:, :, HEAD:].reshape(MLA_KV_RANK, heads * HEAD).astype(BF16)
    return w_z, w_ckv, wq, wk, wv


def _mla_mixer(h, tables, layer, w_in_all, g_q, w_q_up, g_kv, w_kv_up, w_out_all, n_lat, need_ctx):
    m, d = h.shape
    n_ctx = m - n_lat
    q_rank = d // 4
    tq, ta, tb = tables
    w_z, w_ckv, wq, wk, wv = _mla_weights(w_in_all[layer], w_q_up, w_kv_up)
    first = lambda j: 0

    cq = _matmul(h, w_in_all[layer][:, :q_rank].astype(BF16))
    z = _matmul(h, w_z)
    ckvkr = _matmul(h, w_ckv, epilogue=_ckv_epilogue, tn=w_ckv.shape[1],
                    extras=[(ta, HEAD, first), (tb, HEAD, first)])
    q = _matmul(cq, wq, gain=g_q, epilogue=_q_epilogue, extras=[(tq, MLA_QK_PAD, first)])
    kr_cb = MLA_KV_RANK // HEAD
    kk = _matmul(ckvkr, wk, k=MLA_KV_RANK, gain=g_kv, epilogue=_k_epilogue,
                 extras=[(ckvkr, HEAD, lambda j: kr_cb)], out_widen=MLA_QK_PAD // HEAD)
    v = _matmul(ckvkr, wv, k=MLA_KV_RANK, gain=g_kv)

    og = _flash(q, kk, v, z, 0, None, q_row0=0, n_q=n_lat, k_row0=0, n_k=m, total_rows=m)
    if need_ctx:
        og = _flash(q, kk, v, z, 0, og, q_row0=n_lat, n_q=n_ctx, k_row0=n_lat, n_k=n_ctx,
                    total_rows=m)
    rows = m if need_ctx else n_lat
    return _matmul(og, w_out_all, rows=rows, w_cols=(layer, 0, d))


def _fill_window(scr, prev_ref, cur_ref, next_ref, n_lat_blocks):
    tm = cur_ref.shape[0]
    i = pl.program_id(0)
    first = jnp.logical_or(i == 0, i == n_lat_blocks)
    last = jnp.logical_or(i == n_lat_blocks - 1, i == pl.num_programs(0) - 1)
    scr[0:HALO, :] = jnp.where(first, 0.0, prev_ref[...].astype(F32))
    scr[HALO:HALO + tm, :] = cur_ref[...].astype(F32)
    scr[HALO + tm:, :] = jnp.where(last, 0.0, next_ref[...].astype(F32))


def _seg_position(tm, n_lat_blocks, n_lat, n_ctx):
    i = pl.program_id(0)
    is_ctx = i >= n_lat_blocks
    local0 = jnp.where(is_ctx, i - n_lat_blocks, i) * tm
    t = local0 + lax.broadcasted_iota(jnp.int32, (tm, 1), 0)
    return t, jnp.where(is_ctx, n_ctx, n_lat)


def _window_specs(tm, cols, n_rows, col_fn):
    hb = tm // HALO
    last = n_rows // HALO - 1
    return [pl.BlockSpec((HALO, cols), lambda i, j: (jnp.maximum(i * hb - 1, 0), col_fn(j))),
            pl.BlockSpec((tm, cols), lambda i, j: (i, col_fn(j))),
            pl.BlockSpec((HALO, cols), lambda i, j: (jnp.minimum((i + 1) * hb, last), col_fn(j)))]


def _gdn_conv_kernel(prev_ref, cur_ref, next_ref, w_ref, o_ref, scr, *, normalize,
                     n_scaled_blocks, n_lat_blocks):
    tm, tc = cur_ref.shape
    _fill_window(scr, prev_ref, cur_ref, next_ref, n_lat_blocks)
    r = GDN_CONV // 2
    acc = scr[HALO - r:HALO - r + tm, :] * w_ref[0:1, :]
    for j in range(1, GDN_CONV):
        acc = acc + scr[HALO + j - r:HALO + j - r + tm, :] * w_ref[j:j + 1, :]
    y = _silu(acc)
    if normalize:
        scale = jnp.where(pl.program_id(1) < n_scaled_blocks, HEAD ** -0.5, 1.0)
        parts = []
        for hh in range(tc // HEAD):
            u = y[:, hh * HEAD:(hh + 1) * HEAD]
            parts.append(u * (lax.rsqrt(jnp.sum(u * u, axis=-1, keepdims=True) + EPS) * scale))
        y = jnp.concatenate(parts, axis=1) if len(parts) > 1 else parts[0]
    o_ref[...] = y.astype(o_ref.dtype)


def _gdn_conv(src, conv_w, col0, ncols, normalize, n_scaled_cols, n_lat):
    m = src.shape[0]
    tm = SEG_TILE
    tc = _pick(math.gcd(ncols, n_scaled_cols) if n_scaled_cols else ncols, (2048, 1024, 512, 256, 128))
    assert ncols % tc == 0 and n_scaled_cols % tc == 0 and col0 % tc == 0
    cb0 = col0 // tc
    specs = _window_specs(tm, tc, m, lambda j: cb0 + j)
    specs.append(pl.BlockSpec((8, tc), lambda i, j: (0, cb0 + j)))
    w8 = jnp.concatenate([conv_w, jnp.zeros((8 - GDN_CONV, conv_w.shape[1]), F32)], axis=0)
    return pl.pallas_call(
        functools.partial(_gdn_conv_kernel, normalize=normalize,
                          n_scaled_blocks=n_scaled_cols // tc, n_lat_blocks=n_lat // tm),
        grid=(m // tm, ncols // tc),
        in_specs=specs,
        out_specs=pl.BlockSpec((tm, tc), lambda i, j: (i, j)),
        out_shape=jax.ShapeDtypeStruct((m, ncols), BF16),
        scratch_shapes=[pltpu.VMEM((tm + 2 * HALO, tc), F32)],
        compiler_params=_cparams(("arbitrary", "arbitrary")),
        name="gdn_conv",
    )(src, src, src, w8)


def _split3(x):
    x1 = x.astype(BF16)
    r1 = x - x1.astype(F32)
    x2 = r1.astype(BF16)
    x3 = (r1 - x2.astype(F32)).astype(BF16)
    return x1, x2, x3


def _gdn_gate_kernel(ab_ref, alog_ref, dtb_ref, g_ref, beta_ref):
    tm, w = g_ref.shape
    ab = ab_ref[...]
    x = ab[:, :w] + dtb_ref[...]
    softplus = jnp.maximum(x, 0.0) + jnp.log1p(jnp.exp(-jnp.abs(x)))
    g = -jnp.exp(alog_ref[...]) * softplus
    beta_ref[...] = jax.nn.sigmoid(ab[:, w:])
    ch = GDN_CHUNK
    ri = lax.broadcasted_iota(jnp.int32, (ch, ch), 0)
    ci = lax.broadcasted_iota(jnp.int32, (ch, ch), 1)
    lower = (ci <= ri).astype(BF16)
    upper = (ci >= ri).astype(BF16)
    fwd_lane = lax.broadcasted_iota(jnp.int32, (ch, w), 1) < w // 2
    for cidx in range(tm // ch):
        parts = _split3(g[cidx * ch:(cidx + 1) * ch, :])
        pre = _dot(lower, parts[0]) + (_dot(lower, parts[1]) + _dot(lower, parts[2]))
        suf = _dot(upper, parts[0]) + (_dot(upper, parts[1]) + _dot(upper, parts[2]))
        g_ref[cidx * ch:(cidx + 1) * ch, :] = jnp.where(fwd_lane, pre, suf)


def _gdn_gates(ab, a_log, dt_bias):
    m, w4 = ab.shape
    w = w4 // 2
    tm = SEG_TILE
    return pl.pallas_call(
        _gdn_gate_kernel,
        grid=(m // tm,),
        in_specs=[pl.BlockSpec((tm, w4), lambda i: (i, 0)),
                  pl.BlockSpec((1, w), lambda i: (0, 0)),
                  pl.BlockSpec((1, w), lambda i: (0, 0))],
        out_specs=[pl.BlockSpec((tm, w), lambda i: (i, 0)),
                   pl.BlockSpec((tm, w), lambda i: (i, 0))],
        out_shape=[jax.ShapeDtypeStruct((m, w), F32), jax.ShapeDtypeStruct((m, w), F32)],
        compiler_params=_cparams(("arbitrary",)),
        name="gdn_gates",
    )(ab, a_log.reshape(1, w).astype(F32), dt_bias.reshape(1, w).astype(F32))


def _bmm(a, b):
    return lax.dot_general(a, b, (((2,), (1,)), ((0,), (0,))), preferred_element_type=F32)


def _unit_tri_inverse(a, ri, ci):
    def same(s):
        return (ri // s) == (ci // s)

    t = (ri == ci).astype(F32) - jnp.where(same(2), a, 0.0)
    s = 2
    while s < GDN_CHUNK:
        coupling = jnp.logical_and(same(2 * s), jnp.logical_not(same(s)))
        tb = t.astype(BF16)
        tl = _bmm(tb, jnp.where(coupling, a, 0.0).astype(BF16))
        t = t - _bmm(tl.astype(BF16), tb)
        s *= 2
    return t


def _gdn_kernel(q_ref, k_ref, v_ref, col_ref, row_ref, *rest, kg, fwd):
    o_ref, state = rest[-2:]
    ch = GDN_CHUNK
    p2 = 2 * ch

    @pl.when(pl.program_id(1) == 0)
    def _():
        state[...] = jnp.zeros_like(state)

    ri = lax.broadcasted_iota(jnp.int32, (1, p2, p2), 1)
    ci = lax.broadcasted_iota(jnp.int32, (1, p2, p2), 2)
    same_head = (ri // ch) == (ci // ch)
    ahead = ci - ri if fwd else ri - ci
    incl = jnp.logical_and(same_head, ahead <= 0)
    strict = jnp.logical_and(incl, ci != ri)
    first_lanes = lax.broadcasted_iota(jnp.int32, (1, 1, p2), 2) < ch

    def pair_rows(ref):
        x = jnp.stack([ref[:, p * HEAD:(p + 1) * HEAD] for p in range(kg)]).astype(F32)
        return jnp.concatenate([x, x], axis=1)

    def head_rows(ref):
        return jnp.stack([jnp.concatenate([ref[:, (2 * p) * HEAD:(2 * p + 1) * HEAD],
                                           ref[:, (2 * p + 1) * HEAD:(2 * p + 2) * HEAD]], axis=0)
                          for p in range(kg)]).astype(F32)

    k2 = pair_rows(k_ref)
    q2 = pair_rows(q_ref)
    vf = head_rows(v_ref)
    kt = jnp.stack([k2[p].T for p in range(kg)])
    kkqk = _bmm(jnp.concatenate([k2, q2], axis=1).astype(BF16), kt.astype(BF16))
    kk, qk = kkqk[:, :p2], kkqk[:, p2:]

    def col(j0):
        c = jnp.stack([col_ref[:, j0 + p:j0 + p + 1] for p in range(kg)])
        return jnp.broadcast_to(c, (kg, p2, p2))

    gcol, bcol = col(0), col(kg)
    grow = jnp.stack([row_ref[p:p + 1, :] for p in range(kg)])
    last = ch - 1 if fwd else 0
    tot_a = grow[:, :, last:last + 1]
    tot_b = grow[:, :, ch + last:ch + last + 1]
    tot_row = jnp.where(first_lanes, tot_a, tot_b)

    decay = jnp.exp(jnp.where(incl, gcol - grow, -jnp.inf))
    tinv = _unit_tri_inverse(jnp.where(strict, bcol * kk * decay, 0.0), ri, ci)

    eg = jnp.exp(gcol)
    rhs = jnp.concatenate([vf * bcol, k2 * (bcol * eg)], axis=2).astype(BF16)
    uw = _bmm(tinv.astype(BF16), rhs)
    u, w_b = uw[:, :, :HEAD], uw[:, :, HEAD:].astype(BF16)
    qk_d = jnp.where(incl, qk * decay, 0.0).astype(BF16)
    q_dec = (q2 * eg).astype(BF16)
    kt_dec = kt * jnp.exp(tot_row - grow)

    s_a = state[:, 0]
    s_b = state[:, 1]
    ws_a = _bmm(jnp.concatenate([w_b[:, :ch], q_dec[:, :ch]], axis=1), s_a.astype(BF16))
    ws_b = _bmm(jnp.concatenate([w_b[:, ch:], q_dec[:, ch:]], axis=1), s_b.astype(BF16))
    v_new = (u - jnp.concatenate([ws_a[:, :ch], ws_b[:, :ch]], axis=1)).astype(BF16)
    o = jnp.concatenate([ws_a[:, ch:], ws_b[:, ch:]], axis=1) + _bmm(qk_d, v_new)
    kd_a = jnp.where(first_lanes, kt_dec, 0.0).astype(BF16)
    kd_b = jnp.where(first_lanes, 0.0, kt_dec).astype(BF16)
    state[:, 0] = s_a * jnp.exp(tot_a) + _bmm(kd_a, v_new)
    state[:, 1] = s_b * jnp.exp(tot_b) + _bmm(kd_b, v_new)
    if len(rest) > 2:
        oprev_ref, z_ref, gn_ref = rest[:3]
        o = _rms(o + head_rows(oprev_ref)) * gn_ref[...] * _silu(head_rows(z_ref))
    for p in range(kg):
        o_ref[:, (2 * p) * HEAD:(2 * p + 1) * HEAD] = o[p, :ch].astype(o_ref.dtype)
        o_ref[:, (2 * p + 1) * HEAD:(2 * p + 2) * HEAD] = o[p, ch:].astype(o_ref.dtype)


def _gdn_gate_layouts(gcs, beta, kg):
    m, w = gcs.shape
    ch = GDN_CHUNK
    assert HEAD == 2 * ch
    nch, ngrp = m // ch, w // (4 * kg)
    g6 = gcs.reshape(nch, ch, 2, ngrp, kg, 2)
    b6 = beta.reshape(nch, ch, 2, ngrp, kg, 2)
    to_cols = lambda t: t.transpose(2, 3, 0, 5, 1, 4).reshape(2, ngrp, nch, 2 * ch, kg)
    cols = jnp.concatenate([to_cols(g6), to_cols(b6)], axis=4)
    rows = g6.transpose(2, 3, 0, 4, 5, 1).reshape(2, ngrp, nch, kg, 2 * ch)
    return cols, rows


def _gdn_scan(qk, v, cols, rows, n_lat, kg, fwd, fused=None):
    m = v.shape[0]
    vh = v.shape[1] // HEAD
    ch = GDN_CHUNK
    nch = m // ch
    lat_ch = n_lat // ch
    ngrp = vh // (2 * kg)
    nv = 2 * kg
    d = 0 if fwd else 1

    def chunk(n):
        return (n + lat_ch) % nch if fwd else nch - 1 - n

    head_spec = pl.BlockSpec((ch, nv * HEAD), lambda g, n: (chunk(n), g))
    in_specs = [pl.BlockSpec((ch, kg * HEAD), lambda g, n: (chunk(n), g)),
                pl.BlockSpec((ch, kg * HEAD), lambda g, n: (chunk(n), ngrp + g)),
                head_spec,
                pl.BlockSpec((None, None, None, 2 * ch, 2 * kg), lambda g, n: (d, g, chunk(n), 0, 0)),
                pl.BlockSpec((None, None, None, kg, 2 * ch), lambda g, n: (d, g, chunk(n), 0, 0))]
    args = [qk, qk, v, cols, rows]
    if fused is not None:
        o_other, zsrc, z_col, g_norm = fused
        zcb = z_col // (nv * HEAD)
        in_specs += [head_spec,
                     pl.BlockSpec((ch, nv * HEAD), lambda g, n: (chunk(n), zcb + g)),
                     pl.BlockSpec((1, HEAD), lambda g, n: (0, 0))]
        args += [o_other, zsrc, g_norm.reshape(1, HEAD).astype(F32)]
    return pl.pallas_call(
        functools.partial(_gdn_kernel, kg=kg, fwd=fwd),
        grid=(ngrp, nch),
        in_specs=in_specs,
        out_specs=head_spec,
        out_shape=jax.ShapeDtypeStruct((m, vh * HEAD), BF16),
        scratch_shapes=[pltpu.VMEM((kg, 2, HEAD, HEAD), F32)],
        compiler_params=_cparams(("arbitrary", "arbitrary")),
        name="gdn_scan",
    )(*args)


def _gdn_mixer(h, layer, w_in_all, conv_w, a_log, dt_bias, g_norm, w_out_all, n_lat):
    m, d = h.shape
    khs = d // HEAD
    vhs = 2 * khs
    kw, vw = khs * HEAD, vhs * HEAD
    qkv_w = 2 * kw + vw
    n_ab = 4 * vhs
    if n_ab % 256 == 0 and qkv_w % n_ab == 0:
        qkv = _matmul(h, w_in_all, w_cols=(layer, 0, qkv_w))
        ab = _matmul(h, w_in_all, out_dtype=F32, tn=n_ab, w_cols=(layer, qkv_w, n_ab))
        z_src = _matmul(h, w_in_all, tn=n_ab, w_cols=(layer, qkv_w + n_ab, vw))
        z_col = 0
    else:
        w_in = w_in_all[layer]
        w_big = jnp.concatenate([w_in[:, :qkv_w], w_in[:, qkv_w + n_ab:]], axis=1).astype(BF16)
        qkv = z_src = _matmul(h, w_big)
        z_col = qkv_w
        ab = _matmul(h, w_in[:, qkv_w:qkv_w + n_ab].astype(BF16), out_dtype=F32, tn=n_ab)
    qk = _gdn_conv(qkv, conv_w, 0, 2 * kw, True, kw, n_lat)
    v = _gdn_conv(qkv, conv_w, 2 * kw, vw, False, 0, n_lat)
    gcs, beta = _gdn_gates(ab, a_log, dt_bias)
    kg = _pick(khs, (GDN_PAIRS, 4, 2, 1))
    cols, rows = _gdn_gate_layouts(gcs, beta, kg)
    o_fwd = _gdn_scan(qk, v, cols, rows, n_lat, kg, True)
    y = _gdn_scan(qk, v, cols, rows, n_lat, kg, False, fused=(o_fwd, z_src, z_col, g_norm))
    return _matmul(y, w_out_all, w_cols=(layer, 0, d))


def _pool_kernel(prev_ref, cur_ref, next_ref, w_ref, sc_ref, z_ref, o_ref, scr, *,
                 n_lat_blocks, n_lat, n_ctx):
    tm, tc = cur_ref.shape
    _fill_window(scr, prev_ref, cur_ref, next_ref, n_lat_blocks)
    t, seg_len = _seg_position(tm, n_lat_blocks, n_lat, n_ctx)
    for g, window in enumerate(POOL_WINDOWS):
        @pl.when(pl.program_id(1) == g)
        def _(radius=window // 2):
            u = scr[HALO:HALO + tm, :]
            acc = u
            for dlt in range(1, radius + 1):
                acc = acc + (scr[HALO - dlt:HALO - dlt + tm, :] + scr[HALO + dlt:HALO + dlt + tm, :])
            cnt = jnp.minimum(t + radius + 1, seg_len) - jnp.maximum(t - radius, 0)
            mean_minus = acc / cnt.astype(F32) - u
            y = _dot(mean_minus.astype(BF16), w_ref[...]) * sc_ref[...]
            o_ref[...] = (y * _silu(z_ref[...].astype(F32))).astype(o_ref.dtype)


def _pool_mixer(h, layer, w_in_all, w_grp, scale, w_out_all, n_lat):
    m, d = h.shape
    ng, gw, _ = w_grp.shape
    assert tuple(w // 2 for w in POOL_WINDOWS) == tuple(2 ** g for g in range(ng))
    uz = _matmul(h, w_in_all, w_cols=(layer, 0, 2 * d))
    tm = SEG_TILE
    specs = _window_specs(tm, gw, m, lambda g: g)
    specs += [pl.BlockSpec((None, gw, gw), lambda i, g: (g, 0, 0)),
              pl.BlockSpec((1, gw), lambda i, g: (0, g)),
              pl.BlockSpec((tm, gw), lambda i, g: (i, ng + g))]
    yg = pl.pallas_call(
        functools.partial(_pool_kernel, n_lat_blocks=n_lat // tm, n_lat=n_lat, n_ctx=m - n_lat),
        grid=(m // tm, ng),
        in_specs=specs,
        out_specs=pl.BlockSpec((tm, gw), lambda i, g: (i, g)),
        out_shape=jax.ShapeDtypeStruct((m, d), BF16),
        scratch_shapes=[pltpu.VMEM((tm + 2 * HALO, gw), F32)],
        compiler_params=_cparams(("arbitrary", "arbitrary")),
        name="pool_group",
    )(uz, uz, uz, w_grp.astype(BF16), scale.reshape(1, d).astype(F32), uz)
    return _matmul(yg, w_out_all, w_cols=(layer, 0, d))


def kernel(x, c, ctx, c_ctx, ada_w, ada_b, norm_pre, norm_post, mla_w_in, mla_g_q, mla_w_q_up,
           mla_g_kv, mla_w_kv_up, mla_w_out, gdn_w_in, gdn_conv_w, gdn_a_log, gdn_dt_bias,
           gdn_g_norm, gdn_w_out, pool_w_in, pool_w_grp, pool_scale, pool_w_out):
    assert x.shape[0] == 1 and ctx.shape[0] == 1
    n_lat, d = x.shape[1], x.shape[2]
    n_ctx = ctx.shape[1]
    depth = ada_w.shape[0]
    n_mixers = 3
    assert n_lat % SEG_TILE == 0 and n_ctx % SEG_TILE == 0

    cond8 = jnp.concatenate([c.reshape(1, d), c_ctx.reshape(1, d), jnp.zeros((6, d), F32)], axis=0)
    mods = _adaln(cond8, ada_w, ada_b)
    tables = _rope_tables(n_lat, n_ctx)
    xs, h = _prenorm(x[0], ctx[0], norm_pre[0], mods[0])

    counts = [0, 0, 0]
    for i in range(depth):
        kind = i % n_mixers
        j = counts[kind]
        counts[kind] += 1
        need_ctx = i < depth - 1
        if kind == 0:
            y = _mla_mixer(h, tables, j, mla_w_in, mla_g_q[j], mla_w_q_up[j], mla_g_kv[j],
                           mla_w_kv_up[j], mla_w_out, n_lat, need_ctx)
        elif kind == 1:
            y = _gdn_mixer(h, j, gdn_w_in, gdn_conv_w[j], gdn_a_log[j], gdn_dt_bias[j],
                           gdn_g_norm[j], gdn_w_out, n_lat)
        else:
            y = _pool_mixer(h, j, pool_w_in, pool_w_grp[j], pool_scale[j], pool_w_out, n_lat)
        if need_ctx:
            xs, h = _residual(xs, y, norm_post[i], mods[i], n_lat, xs.shape[0],
                              norm_pre[i + 1], mods[i + 1])
        else:
            xs = _residual(xs, y, norm_post[i], mods[i], n_lat, n_lat)
    return xs[None]
```

```python
import functools
import math

import jax
import jax.numpy as jnp
from jax import lax
from jax.experimental import pallas as pl
from jax.experimental.pallas import tpu as pltpu

F32 = jnp.float32
BF16 = jnp.bfloat16
EPS = 1e-6

HEAD = 128
MLA_ROPE = 64
MLA_KV_RANK = 512
MLA_QK_PAD = 256
ROPE_BASE = 10000.0
GRID_W = 64
GDN_CONV = 5
GDN_CHUNK = 64
GDN_PAIRS = 16
POOL_WINDOWS = (2, 4, 8, 16)
SEG_TILE = 256
HALO = 16
VMEM_LIMIT = 56 * 1024 * 1024
MM_MAX_TK = 8192
MM_W_TILE_ELEMS = 2 * 1024 * 1024
MM_A_TILE_ELEMS = 6 * 1024 * 1024


def _pick(n, prefs):
    for p in prefs:
        if n % p == 0:
            return p
    return n


def _cparams(sem):
    return pltpu.CompilerParams(dimension_semantics=sem, vmem_limit_bytes=VMEM_LIMIT)


def _dot(a, b):
    return jnp.dot(a, b, preferred_element_type=F32)


def _dot_nt(a, b):
    return lax.dot_general(a, b, (((1,), (1,)), ((), ())), preferred_element_type=F32)


def _silu(x):
    return x * jax.nn.sigmoid(x)


def _adaln_kernel(c_ref, w_ref, b_ref, o_ref):
    a = _silu(c_ref[...]).astype(BF16)
    o_ref[...] = _dot(a, w_ref[...].astype(BF16)) + b_ref[...]


def _adaln(cond8, ada_w, ada_b):
    depth, d, n = ada_w.shape
    tn = _pick(n, (512, 256, 128))
    return pl.pallas_call(
        _adaln_kernel,
        grid=(depth, n // tn),
        in_specs=[pl.BlockSpec((8, d), lambda l, j: (0, 0)),
                  pl.BlockSpec((None, d, tn), lambda l, j: (l, 0, j)),
                  pl.BlockSpec((None, 1, tn), lambda l, j: (l, 0, j))],
        out_specs=pl.BlockSpec((None, 8, tn), lambda l, j: (l, 0, j)),
        out_shape=jax.ShapeDtypeStruct((depth, 8, n), F32),
        compiler_params=_cparams(("arbitrary", "arbitrary")),
        name="adaln",
    )(cond8, ada_w, ada_b.reshape(depth, 1, n))


def _rms(xf):
    return xf * lax.rsqrt(jnp.mean(xf * xf, axis=-1, keepdims=True) + EPS)


def _prenorm_kernel(xl_ref, xc_ref, g_ref, mod_ref, xs_ref, h_ref, *, n_lat_blocks, d):
    is_ctx = pl.program_id(0) >= n_lat_blocks
    x = jnp.where(is_ctx, xc_ref[...], xl_ref[...])
    mod = jnp.where(is_ctx, mod_ref[1:2, :], mod_ref[0:1, :])
    sh, sc = mod[:, :d], mod[:, d:2 * d]
    y = _rms(x) * g_ref[...]
    xs_ref[...] = x
    h_ref[...] = (y * (1.0 + sc) + sh).astype(h_ref.dtype)


def _prenorm(x_lat, x_ctx, g, mod):
    n_lat, d = x_lat.shape
    m = n_lat + x_ctx.shape[0]
    tm = SEG_TILE
    nlb = n_lat // tm
    row_spec = pl.BlockSpec((tm, d), lambda i: (i, 0))
    return pl.pallas_call(
        functools.partial(_prenorm_kernel, n_lat_blocks=nlb, d=d),
        grid=(m // tm,),
        in_specs=[pl.BlockSpec((tm, d), lambda i: (jnp.minimum(i, nlb - 1), 0)),
                  pl.BlockSpec((tm, d), lambda i: (jnp.maximum(i - nlb, 0), 0)),
                  pl.BlockSpec((1, d), lambda i: (0, 0)),
                  pl.BlockSpec((8, 3 * d), lambda i: (0, 0))],
        out_specs=[row_spec, row_spec],
        out_shape=[jax.ShapeDtypeStruct((m, d), F32), jax.ShapeDtypeStruct((m, d), BF16)],
        compiler_params=_cparams(("arbitrary",)),
        name="prenorm",
    )(x_lat, x_ctx, g.reshape(1, d), mod)


def _residual_kernel(x_ref, y_ref, g_ref, mod_ref, *rest, n_lat_blocks, d, with_next):
    is_ctx = pl.program_id(0) >= n_lat_blocks
    mod = jnp.where(is_ctx, mod_ref[1:2, :], mod_ref[0:1, :])
    gt = mod[:, 2 * d:]
    y = _rms(y_ref[...].astype(F32)) * g_ref[...]
    x_new = x_ref[...] + gt * y
    if not with_next:
        rest[0][...] = x_new
        return
    gn_ref, modn_ref, o_ref, h_ref = rest
    o_ref[...] = x_new
    modn = jnp.where(is_ctx, modn_ref[1:2, :], modn_ref[0:1, :])
    h = _rms(x_new) * gn_ref[...]
    h_ref[...] = (h * (1.0 + modn[:, d:2 * d]) + modn[:, :d]).astype(h_ref.dtype)


def _residual(x, y, g, mod, n_lat, rows, g_next=None, mod_next=None):
    d = x.shape[1]
    tm = SEG_TILE
    with_next = g_next is not None
    row_spec = pl.BlockSpec((tm, d), lambda i: (i, 0))
    vec_spec = pl.BlockSpec((1, d), lambda i: (0, 0))
    mod_spec = pl.BlockSpec((8, 3 * d), lambda i: (0, 0))
    in_specs = [row_spec, row_spec, vec_spec, mod_spec]
    args = [x, y, g.reshape(1, d), mod]
    out_specs, out_shape = row_spec, jax.ShapeDtypeStruct((rows, d), F32)
    if with_next:
        in_specs += [vec_spec, mod_spec]
        args += [g_next.reshape(1, d), mod_next]
        out_specs = [row_spec, row_spec]
        out_shape = [out_shape, jax.ShapeDtypeStruct((rows, d), BF16)]
    return pl.pallas_call(
        functools.partial(_residual_kernel, n_lat_blocks=n_lat // tm, d=d, with_next=with_next),
        grid=(rows // tm,),
        in_specs=in_specs,
        out_specs=out_specs,
        out_shape=out_shape,
        compiler_params=_cparams(("arbitrary",)),
        name="residual",
    )(*args)


def _mm_kernel(a_ref, w_ref, *rest, nk, has_gain, epilogue, n_extra):
    pos = 0
    g_ref = rest[pos] if has_gain else None
    pos += int(has_gain)
    extra = rest[pos:pos + n_extra]
    pos += n_extra
    o_ref = rest[pos]
    acc_ref = rest[pos + 1] if nk > 1 else None

    if has_gain:
        an_ref = rest[pos + 1]

        @pl.when(pl.program_id(1) == 0)
        def _():
            an_ref[...] = (_rms(a_ref[...].astype(F32)) * g_ref[...]).astype(BF16)

        a = an_ref[...]
    else:
        a = a_ref[...]
    r = _dot(a, w_ref[...].astype(BF16))

    def finish(res):
        if epilogue is not None:
            res = epilogue(res, *[e[...] for e in extra])
        o_ref[...] = res.astype(o_ref.dtype)

    if nk == 1:
        finish(r)
    else:
        k = pl.program_id(2)

        @pl.when(k == 0)
        def _():
            acc_ref[...] = r

        @pl.when(k > 0)
        def _():
            acc_ref[...] += r

        @pl.when(k == nk - 1)
        def _():
            finish(acc_ref[...])


def _matmul(a, w, *, rows=None, a_col=0, k=None, out_dtype=BF16, gain=None,
            epilogue=None, extras=(), tn=None, tm=None, out_widen=1, w_cols=None):
    if w_cols is not None:
        layer, col0, n = w_cols
        k = w.shape[1]
    else:
        k = w.shape[0] if k is None else k
        n = w.shape[1]
    rows = a.shape[0] if rows is None else rows
    tk = k if k <= MM_MAX_TK else _pick(k, (4096, 2048, 1024, 512))
    nk = k // tk
    tm = tm or _pick(rows, [t for t in (1408, 1024, 768, 512, 256, 128, 64, 32, 16)
                            if t * tk <= 2 * MM_A_TILE_ELEMS] or [8])
    a_mode = pl.Buffered(1) if tm * tk > MM_A_TILE_ELEMS and nk == 1 else None
    tn = tn or _pick(n, [t for t in (1024, 512, 256, 128) if tk * t <= MM_W_TILE_ELEMS] or [128])
    assert a_col % tk == 0 and (gain is None or nk == 1)
    a_cb = a_col // tk
    if w_cols is None:
        w_spec = pl.BlockSpec((tk, tn), lambda i, j, kk: (kk, j))
    else:
        assert col0 % tn == 0
        w_spec = pl.BlockSpec((None, tk, tn), lambda i, j, kk: (layer, kk, col0 // tn + j))
    in_specs = [pl.BlockSpec((tm, tk), lambda i, j, kk: (i, a_cb + kk), pipeline_mode=a_mode), w_spec]
    args = [a, w]
    if gain is not None:
        in_specs.append(pl.BlockSpec((1, tk), lambda i, j, kk: (0, 0)))
        args.append(gain.reshape(1, tk).astype(F32))
    for arr, cols, cb in extras:
        in_specs.append(pl.BlockSpec((tm, cols), lambda i, j, kk, cb=cb: (i, cb(j))))
        args.append(arr)
    scratch = [pltpu.VMEM((tm, tn), F32)] if nk > 1 else []
    if gain is not None:
        scratch = [pltpu.VMEM((tm, tk), BF16)]
    return pl.pallas_call(
        functools.partial(_mm_kernel, nk=nk, has_gain=gain is not None,
                          epilogue=epilogue, n_extra=len(extras)),
        grid=(rows // tm, n // tn, nk),
        in_specs=in_specs,
        out_specs=pl.BlockSpec((tm, tn * out_widen), lambda i, j, kk: (i, j)),
        out_shape=jax.ShapeDtypeStruct((rows, n * out_widen), out_dtype),
        scratch_shapes=scratch,
        compiler_params=_cparams(("arbitrary", "arbitrary", "arbitrary")),
        name="matmul",
    )(*args)


def _rope_tables(n_lat, n_ctx):
    rows = n_lat // GRID_W
    row = jnp.repeat(jnp.arange(rows, dtype=F32), GRID_W)
    col = jnp.tile(jnp.arange(GRID_W, dtype=F32), rows)
    n_freq = MLA_ROPE // 4
    inv_freq = ROPE_BASE ** (-jnp.arange(n_freq, dtype=F32) / n_freq)
    ang = jnp.concatenate([row[:, None] * inv_freq, col[:, None] * inv_freq], axis=-1)
    ang = jnp.concatenate([ang, jnp.zeros((n_ctx, ang.shape[1]), F32)], axis=0)
    cos, sin = jnp.cos(ang), jnp.sin(ang)
    ta = jnp.concatenate([cos, sin, cos, -sin], axis=1)
    tb = jnp.concatenate([-sin, cos, sin, cos], axis=1)
    tq = jnp.concatenate([jnp.ones((ang.shape[0], HEAD), F32), ta], axis=1)
    return tq, ta, tb


def _q_epilogue(t, tq):
    reps = t.shape[1] // MLA_QK_PAD
    return t * (jnp.concatenate([tq] * reps, axis=1) if reps > 1 else tq)


def _k_epilogue(t, kr):
    krf = kr.astype(F32)
    parts = []
    for hh in range(t.shape[1] // HEAD):
        parts += [t[:, hh * HEAD:(hh + 1) * HEAD], krf]
    return jnp.concatenate(parts, axis=1)


def _ckv_epilogue(t, ta, tb):
    r = MLA_KV_RANK
    a, b = t[:, r:r + HEAD], t[:, r + HEAD:]
    return jnp.concatenate([t[:, :r], a * ta + b * tb, b], axis=1)


def _flash_kernel(q_ref, k_ref, v_ref, z_ref, o_ref, *, bounds):
    q = q_ref[...]
    tq = q.shape[0]
    nc = len(bounds) - 1
    m = jnp.full((tq, 1), -jnp.inf, F32)
    acc = jnp.zeros((tq, 2 * HEAD), F32)
    s = _dot_nt(q, k_ref[bounds[0]:bounds[1], :])
    for c in range(nc):
        lo, hi = bounds[c], bounds[c + 1]
        s_next = _dot_nt(q, k_ref[hi:bounds[c + 2], :]) if c + 1 < nc else None
        m_new = jnp.maximum(m, jnp.max(s, axis=-1, keepdims=True))
        p = jnp.exp2(s - m_new)
        alpha = jnp.exp2(m - m_new)
        ones_col = (lax.broadcasted_iota(jnp.int32, (hi - lo, HEAD), 1) == 0).astype(BF16)
        v_ext = jnp.concatenate([v_ref[lo:hi, :], ones_col], axis=1)
        acc = alpha * acc + _dot(p.astype(BF16), v_ext)
        m, s = m_new, s_next
    o = acc[:, :HEAD] / acc[:, HEAD:HEAD + 1]
    o_ref[...] = (o * _silu(z_ref[...].astype(F32))).astype(o_ref.dtype)


def _flash(q, k, v, zsrc, z_col, og_prev, *, q_row0, n_q, k_row0, n_k, total_rows):
    heads = v.shape[1] // HEAD
    tq = _pick(n_q, (1024, 512, 256, 128))
    tk = _pick(n_k, (2816, 768, 512, 256, 128))
    bounds = tuple(range(0, n_k + 1, tk))
    assert q_row0 % tq == 0 and k_row0 % n_k == 0
    qb0, kb0, zcb = q_row0 // tq, k_row0 // n_k, z_col // HEAD
    in_specs = [pl.BlockSpec((tq, MLA_QK_PAD), lambda h, i: (qb0 + i, h)),
                pl.BlockSpec((n_k, MLA_QK_PAD), lambda h, i: (kb0, h)),
                pl.BlockSpec((n_k, HEAD), lambda h, i: (kb0, h)),
                pl.BlockSpec((tq, HEAD), lambda h, i: (qb0 + i, zcb + h))]
    args = [q, k, v, zsrc]
    aliases = {}
    if og_prev is not None:
        in_specs.append(pl.BlockSpec(memory_space=pl.ANY))
        args.append(og_prev)
        aliases = {4: 0}

    def kern(q_ref, k_ref, v_ref, z_ref, *rest):
        _flash_kernel(q_ref, k_ref, v_ref, z_ref, rest[-1], bounds=bounds)

    return pl.pallas_call(
        kern,
        grid=(heads, n_q // tq),
        in_specs=in_specs,
        out_specs=pl.BlockSpec((tq, HEAD), lambda h, i: (qb0 + i, h)),
        out_shape=jax.ShapeDtypeStruct((total_rows, heads * HEAD), BF16),
        input_output_aliases=aliases,
        compiler_params=_cparams(("arbitrary", "arbitrary")),
        name="mla_flash",
    )(*args)


def _mla_weights(w_in, w_q_up, w_kv_up):
    d = w_in.shape[0]
    q_rank = d // 4
    heads = d // HEAD
    r0, r1, r2 = q_rank, q_rank + MLA_KV_RANK, q_rank + MLA_KV_RANK + MLA_ROPE
    half = MLA_ROPE // 2
    assert MLA_QK_PAD == HEAD + 2 * MLA_ROPE
    scale = (HEAD + MLA_ROPE) ** -0.5 * math.log2(math.e)
    w_z = w_in[:, r2:].astype(BF16)
    k1, k2 = w_in[:, r1:r1 + half], w_in[:, r1 + half:r2]
    w_ckv = jnp.concatenate([w_in[:, r0:r1], k1, k1, k2, k2, k2, k2, k1, k1], axis=1).astype(BF16)
    wq = (w_q_up * scale).reshape(q_rank, heads, HEAD + MLA_ROPE)
    x1, x2 = wq[:, :, HEAD:HEAD + half], wq[:, :, HEAD + half:]
    wq = jnp.concatenate([wq[:, :, :HEAD], x1, x1, x2, x2], axis=2)
    wq = wq.reshape(q_rank, heads * MLA_QK_PAD).astype(BF16)
    wkv = w_kv_up.reshape(MLA_KV_RANK, heads, 2 * HEAD)
    wk = wkv[:, :, :HEAD].reshape(MLA_KV_RANK, heads * HEAD).astype(BF16)
    wv = wkv[:, :, HEAD:].reshape(MLA_KV_RANK, heads * HEAD).astype(BF16)
    return w_z, w_ckv, wq, wk, wv


def _mla_mixer(h, tables, layer, w_in_all, g_q, w_q_up, g_kv, w_kv_up, w_out_all, n_lat, need_ctx):
    m, d = h.shape
    n_ctx = m - n_lat
    q_rank = d // 4
    tq, ta, tb = tables
    w_z, w_ckv, wq, wk, wv = _mla_weights(w_in_all[layer], w_q_up, w_kv_up)
    first = lambda j: 0

    cq = _matmul(h, w_in_all[layer][:, :q_rank].astype(BF16))
    z = _matmul(h, w_z)
    ckvkr = _matmul(h, w_ckv, epilogue=_ckv_epilogue, tn=w_ckv.shape[1],
                    extras=[(ta, HEAD, first), (tb, HEAD, first)])
    q = _matmul(cq, wq, gain=g_q, epilogue=_q_epilogue, extras=[(tq, MLA_QK_PAD, first)])
    kr_cb = MLA_KV_RANK // HEAD
    kk = _matmul(ckvkr, wk, k=MLA_KV_RANK, gain=g_kv, epilogue=_k_epilogue,
                 extras=[(ckvkr, HEAD, lambda j: kr_cb)], out_widen=MLA_QK_PAD // HEAD)
    v = _matmul(ckvkr, wv, k=MLA_KV_RANK, gain=g_kv)

    og = _flash(q, kk, v, z, 0, None, q_row0=0, n_q=n_lat, k_row0=0, n_k=m, total_rows=m)
    if need_ctx:
        og = _flash(q, kk, v, z, 0, og, q_row0=n_lat, n_q=n_ctx, k_row0=n_lat, n_k=n_ctx,
                    total_rows=m)
    rows = m if need_ctx else n_lat
    return _matmul(og, w_out_all, rows=rows, w_cols=(layer, 0, d))


def _fill_window(scr, prev_ref, cur_ref, next_ref, n_lat_blocks, row_axis=0):
    tm = cur_ref.shape[0]
    i = pl.program_id(row_axis)
    first = jnp.logical_or(i == 0, i == n_lat_blocks)
    last = jnp.logical_or(i == n_lat_blocks - 1, i == pl.num_programs(row_axis) - 1)
    scr[0:HALO, :] = jnp.where(first, 0.0, prev_ref[...].astype(F32))
    scr[HALO:HALO + tm, :] = cur_ref[...].astype(F32)
    scr[HALO + tm:, :] = jnp.where(last, 0.0, next_ref[...].astype(F32))


def _seg_position(tm, n_lat_blocks, n_lat, n_ctx, row_axis=0):
    i = pl.program_id(row_axis)
    is_ctx = i >= n_lat_blocks
    local0 = jnp.where(is_ctx, i - n_lat_blocks, i) * tm
    t = local0 + lax.broadcasted_iota(jnp.int32, (tm, 1), 0)
    return t, jnp.where(is_ctx, n_ctx, n_lat)


def _window_specs(tm, cols, n_rows, col_fn):
    hb = tm // HALO
    last = n_rows // HALO - 1
    return [pl.BlockSpec((HALO, cols), lambda i, j: (jnp.maximum(i * hb - 1, 0), col_fn(j))),
            pl.BlockSpec((tm, cols), lambda i, j: (i, col_fn(j))),
            pl.BlockSpec((HALO, cols), lambda i, j: (jnp.minimum((i + 1) * hb, last), col_fn(j)))]


def _gdn_conv_kernel(prev_ref, cur_ref, next_ref, w_ref, o_ref, scr, *, normalize,
                     n_scaled_blocks, n_lat_blocks):
    tm, tc = cur_ref.shape
    _fill_window(scr, prev_ref, cur_ref, next_ref, n_lat_blocks)
    r = GDN_CONV // 2
    acc = scr[HALO - r:HALO - r + tm, :] * w_ref[0:1, :]
    for j in range(1, GDN_CONV):
        acc = acc + scr[HALO + j - r:HALO + j - r + tm, :] * w_ref[j:j + 1, :]
    y = _silu(acc)
    if normalize:
        scale = jnp.where(pl.program_id(1) < n_scaled_blocks, HEAD ** -0.5, 1.0)
        parts = []
        for hh in range(tc // HEAD):
            u = y[:, hh * HEAD:(hh + 1) * HEAD]
            parts.append(u * (lax.rsqrt(jnp.sum(u * u, axis=-1, keepdims=True) + EPS) * scale))
        y = jnp.concatenate(parts, axis=1) if len(parts) > 1 else parts[0]
    o_ref[...] = y.astype(o_ref.dtype)


def _gdn_conv(src, conv_w, col0, ncols, normalize, n_scaled_cols, n_lat):
    m = src.shape[0]
    tm = SEG_TILE
    tc = _pick(math.gcd(ncols, n_scaled_cols) if n_scaled_cols else ncols, (2048, 1024, 512, 256, 128))
    assert ncols % tc == 0 and n_scaled_cols % tc == 0 and col0 % tc == 0
    cb0 = col0 // tc
    specs = _window_specs(tm, tc, m, lambda j: cb0 + j)
    specs.append(pl.BlockSpec((8, tc), lambda i, j: (0, cb0 + j)))
    w8 = jnp.concatenate([conv_w, jnp.zeros((8 - GDN_CONV, conv_w.shape[1]), F32)], axis=0)
    return pl.pallas_call(
        functools.partial(_gdn_conv_kernel, normalize=normalize,
                          n_scaled_blocks=n_scaled_cols // tc, n_lat_blocks=n_lat // tm),
        grid=(m // tm, ncols // tc),
        in_specs=specs,
        out_specs=pl.BlockSpec((tm, tc), lambda i, j: (i, j)),
        out_shape=jax.ShapeDtypeStruct((m, ncols), BF16),
        scratch_shapes=[pltpu.VMEM((tm + 2 * HALO, tc), F32)],
        compiler_params=_cparams(("arbitrary", "arbitrary")),
        name="gdn_conv",
    )(src, src, src, w8)


def _split3(x):
    x1 = x.astype(BF16)
    r1 = x - x1.astype(F32)
    x2 = r1.astype(BF16)
    x3 = (r1 - x2.astype(F32)).astype(BF16)
    return x1, x2, x3


def _gdn_gate_kernel(ab_ref, alog_ref, dtb_ref, g_ref, beta_ref):
    tm, w = g_ref.shape
    ab = ab_ref[...]
    x = ab[:, :w] + dtb_ref[...]
    softplus = jnp.maximum(x, 0.0) + jnp.log1p(jnp.exp(-jnp.abs(x)))
    g = -jnp.exp(alog_ref[...]) * softplus
    beta_ref[...] = jax.nn.sigmoid(ab[:, w:])
    ch = GDN_CHUNK
    ri = lax.broadcasted_iota(jnp.int32, (ch, ch), 0)
    ci = lax.broadcasted_iota(jnp.int32, (ch, ch), 1)
    lower = (ci <= ri).astype(BF16)
    upper = (ci >= ri).astype(BF16)
    fwd_lane = lax.broadcasted_iota(jnp.int32, (ch, w), 1) < w // 2
    for cidx in range(tm // ch):
        parts = _split3(g[cidx * ch:(cidx + 1) * ch, :])
        pre = _dot(lower, parts[0]) + (_dot(lower, parts[1]) + _dot(lower, parts[2]))
        suf = _dot(upper, parts[0]) + (_dot(upper, parts[1]) + _dot(upper, parts[2]))
        g_ref[cidx * ch:(cidx + 1) * ch, :] = jnp.where(fwd_lane, pre, suf)


def _gdn_gates(ab, a_log, dt_bias):
    m, w4 = ab.shape
    w = w4 // 2
    tm = SEG_TILE
    return pl.pallas_call(
        _gdn_gate_kernel,
        grid=(m // tm,),
        in_specs=[pl.BlockSpec((tm, w4), lambda i: (i, 0)),
                  pl.BlockSpec((1, w), lambda i: (0, 0)),
                  pl.BlockSpec((1, w), lambda i: (0, 0))],
        out_specs=[pl.BlockSpec((tm, w), lambda i: (i, 0)),
                   pl.BlockSpec((tm, w), lambda i: (i, 0))],
        out_shape=[jax.ShapeDtypeStruct((m, w), F32), jax.ShapeDtypeStruct((m, w), F32)],
        compiler_params=_cparams(("arbitrary",)),
        name="gdn_gates",
    )(ab, a_log.reshape(1, w).astype(F32), dt_bias.reshape(1, w).astype(F32))


def _bmm(a, b):
    return lax.dot_general(a, b, (((2,), (1,)), ((0,), (0,))), preferred_element_type=F32)


def _unit_tri_inverse(a, ri, ci):
    def same(s):
        return (ri // s) == (ci // s)

    t = (ri == ci).astype(F32) - jnp.where(same(2), a, 0.0)
    s = 2
    while s < GDN_CHUNK:
        coupling = jnp.logical_and(same(2 * s), jnp.logical_not(same(s)))
        tb = t.astype(BF16)
        tl = _bmm(tb, jnp.where(coupling, a, 0.0).astype(BF16))
        t = t - _bmm(tl.astype(BF16), tb)
        s *= 2
    return t


def _gdn_kernel(q_ref, k_ref, v_ref, col_ref, row_ref, *rest, kg, fwd):
    o_ref, state = rest[-2:]
    ch = GDN_CHUNK
    p2 = 2 * ch

    @pl.when(pl.program_id(1) == 0)
    def _():
        state[...] = jnp.zeros_like(state)

    ri = lax.broadcasted_iota(jnp.int32, (1, p2, p2), 1)
    ci = lax.broadcasted_iota(jnp.int32, (1, p2, p2), 2)
    same_head = (ri // ch) == (ci // ch)
    ahead = ci - ri if fwd else ri - ci
    incl = jnp.logical_and(same_head, ahead <= 0)
    strict = jnp.logical_and(incl, ci != ri)
    first_lanes = lax.broadcasted_iota(jnp.int32, (1, 1, p2), 2) < ch

    def pair_rows(ref):
        x = jnp.stack([ref[:, p * HEAD:(p + 1) * HEAD] for p in range(kg)]).astype(F32)
        return jnp.concatenate([x, x], axis=1)

    def head_rows(ref):
        return jnp.stack([jnp.concatenate([ref[:, (2 * p) * HEAD:(2 * p + 1) * HEAD],
                                           ref[:, (2 * p + 1) * HEAD:(2 * p + 2) * HEAD]], axis=0)
                          for p in range(kg)]).astype(F32)

    k2 = pair_rows(k_ref)
    q2 = pair_rows(q_ref)
    vf = head_rows(v_ref)
    kt = jnp.stack([k2[p].T for p in range(kg)])
    kkqk = _bmm(jnp.concatenate([k2, q2], axis=1).astype(BF16), kt.astype(BF16))
    kk, qk = kkqk[:, :p2], kkqk[:, p2:]

    def col(j0):
        c = jnp.stack([col_ref[:, j0 + p:j0 + p + 1] for p in range(kg)])
        return jnp.broadcast_to(c, (kg, p2, p2))

    gcol, bcol = col(0), col(kg)
    grow = jnp.stack([row_ref[p:p + 1, :] for p in range(kg)])
    last = ch - 1 if fwd else 0
    tot_a = grow[:, :, last:last + 1]
    tot_b = grow[:, :, ch + last:ch + last + 1]
    tot_row = jnp.where(first_lanes, tot_a, tot_b)

    decay = jnp.exp(jnp.where(incl, gcol - grow, -jnp.inf))
    tinv = _unit_tri_inverse(jnp.where(strict, bcol * kk * decay, 0.0), ri, ci)

    eg = jnp.exp(gcol)
    rhs = jnp.concatenate([vf * bcol, k2 * (bcol * eg)], axis=2).astype(BF16)
    uw = _bmm(tinv.astype(BF16), rhs)
    u, w_b = uw[:, :, :HEAD], uw[:, :, HEAD:].astype(BF16)
    qk_d = jnp.where(incl, qk * decay, 0.0).astype(BF16)
    q_dec = (q2 * eg).astype(BF16)
    kt_dec = kt * jnp.exp(tot_row - grow)

    s_a = state[:, 0]
    s_b = state[:, 1]
    ws_a = _bmm(jnp.concatenate([w_b[:, :ch], q_dec[:, :ch]], axis=1), s_a.astype(BF16))
    ws_b = _bmm(jnp.concatenate([w_b[:, ch:], q_dec[:, ch:]], axis=1), s_b.astype(BF16))
    v_new = (u - jnp.concatenate([ws_a[:, :ch], ws_b[:, :ch]], axis=1)).astype(BF16)
    o = jnp.concatenate([ws_a[:, ch:], ws_b[:, ch:]], axis=1) + _bmm(qk_d, v_new)
    kd_a = jnp.where(first_lanes, kt_dec, 0.0).astype(BF16)
    kd_b = jnp.where(first_lanes, 0.0, kt_dec).astype(BF16)
    state[:, 0] = s_a * jnp.exp(tot_a) + _bmm(kd_a, v_new)
    state[:, 1] = s_b * jnp.exp(tot_b) + _bmm(kd_b, v_new)
    if len(rest) > 2:
        oprev_ref, z_ref, gn_ref = rest[:3]
        o = _rms(o + head_rows(oprev_ref)) * gn_ref[...] * _silu(head_rows(z_ref))
    for p in range(kg):
        o_ref[:, (2 * p) * HEAD:(2 * p + 1) * HEAD] = o[p, :ch].astype(o_ref.dtype)
        o_ref[:, (2 * p + 1) * HEAD:(2 * p + 2) * HEAD] = o[p, ch:].astype(o_ref.dtype)


def _gdn_gate_layouts(gcs, beta, kg):
    m, w = gcs.shape
    ch = GDN_CHUNK
    assert HEAD == 2 * ch
    nch, ngrp = m // ch, w // (4 * kg)
    g6 = gcs.reshape(nch, ch, 2, ngrp, kg, 2)
    b6 = beta.reshape(nch, ch, 2, ngrp, kg, 2)
    to_cols = lambda t: t.transpose(2, 3, 0, 5, 1, 4).reshape(2, ngrp, nch, 2 * ch, kg)
    cols = jnp.concatenate([to_cols(g6), to_cols(b6)], axis=4)
    rows = g6.transpose(2, 3, 0, 4, 5, 1).reshape(2, ngrp, nch, kg, 2 * ch)
    return cols, rows


def _gdn_scan(qk, v, cols, rows, n_lat, kg, fwd, fused=None):
    m = v.shape[0]
    vh = v.shape[1] // HEAD
    ch = GDN_CHUNK
    nch = m // ch
    lat_ch = n_lat // ch
    ngrp = vh // (2 * kg)
    nv = 2 * kg
    d = 0 if fwd else 1

    def chunk(n):
        return (n + lat_ch) % nch if fwd else nch - 1 - n

    head_spec = pl.BlockSpec((ch, nv * HEAD), lambda g, n: (chunk(n), g))
    in_specs = [pl.BlockSpec((ch, kg * HEAD), lambda g, n: (chunk(n), g)),
                pl.BlockSpec((ch, kg * HEAD), lambda g, n: (chunk(n), ngrp + g)),
                head_spec,
                pl.BlockSpec((None, None, None, 2 * ch, 2 * kg), lambda g, n: (d, g, chunk(n), 0, 0)),
                pl.BlockSpec((None, None, None, kg, 2 * ch), lambda g, n: (d, g, chunk(n), 0, 0))]
    args = [qk, qk, v, cols, rows]
    if fused is not None:
        o_other, zsrc, z_col, g_norm = fused
        zcb = z_col // (nv * HEAD)
        in_specs += [head_spec,
                     pl.BlockSpec((ch, nv * HEAD), lambda g, n: (chunk(n), zcb + g)),
                     pl.BlockSpec((1, HEAD), lambda g, n: (0, 0))]
        args += [o_other, zsrc, g_norm.reshape(1, HEAD).astype(F32)]
    return pl.pallas_call(
        functools.partial(_gdn_kernel, kg=kg, fwd=fwd),
        grid=(ngrp, nch),
        in_specs=in_specs,
        out_specs=head_spec,
        out_shape=jax.ShapeDtypeStruct((m, vh * HEAD), BF16),
        scratch_shapes=[pltpu.VMEM((kg, 2, HEAD, HEAD), F32)],
        compiler_params=_cparams(("arbitrary", "arbitrary")),
        name="gdn_scan",
    )(*args)


def _gdn_mixer(h, layer, w_in_all, conv_w, a_log, dt_bias, g_norm, w_out_all, n_lat):
    m, d = h.shape
    khs = d // HEAD
    vhs = 2 * khs
    kw, vw = khs * HEAD, vhs * HEAD
    qkv_w = 2 * kw + vw
    n_ab = 4 * vhs
    if n_ab % 256 == 0 and qkv_w % n_ab == 0:
        qkv = _matmul(h, w_in_all, w_cols=(layer, 0, qkv_w))
        ab = _matmul(h, w_in_all, out_dtype=F32, tn=n_ab, w_cols=(layer, qkv_w, n_ab))
        z_src = _matmul(h, w_in_all, tn=n_ab, w_cols=(layer, qkv_w + n_ab, vw))
        z_col = 0
    else:
        w_in = w_in_all[layer]
        w_big = jnp.concatenate([w_in[:, :qkv_w], w_in[:, qkv_w + n_ab:]], axis=1).astype(BF16)
        qkv = z_src = _matmul(h, w_big)
        z_col = qkv_w
        ab = _matmul(h, w_in[:, qkv_w:qkv_w + n_ab].astype(BF16), out_dtype=F32, tn=n_ab)
    qk = _gdn_conv(qkv, conv_w, 0, 2 * kw, True, kw, n_lat)
    v = _gdn_conv(qkv, conv_w, 2 * kw, vw, False, 0, n_lat)
    gcs, beta = _gdn_gates(ab, a_log, dt_bias)
    kg = _pick(khs, (GDN_PAIRS, 4, 2, 1))
    cols, rows = _gdn_gate_layouts(gcs, beta, kg)
    o_fwd = _gdn_scan(qk, v, cols, rows, n_lat, kg, True)
    y = _gdn_scan(qk, v, cols, rows, n_lat, kg, False, fused=(o_fwd, z_src, z_col, g_norm))
    return _matmul(y, w_out_all, w_cols=(layer, 0, d))


def _pool_kernel(prev_ref, cur_ref, next_ref, w_ref, sc_ref, z_ref, o_ref, scr, *,
                 n_lat_blocks, n_lat, n_ctx):
    tm, tc = cur_ref.shape
    _fill_window(scr, prev_ref, cur_ref, next_ref, n_lat_blocks, row_axis=1)
    t, seg_len = _seg_position(tm, n_lat_blocks, n_lat, n_ctx, row_axis=1)
    for g, window in enumerate(POOL_WINDOWS):
        @pl.when(pl.program_id(0) == g)
        def _(radius=window // 2):
            u = scr[HALO:HALO + tm, :]
            acc = u
            for dlt in range(1, radius + 1):
                acc = acc + (scr[HALO - dlt:HALO - dlt + tm, :] + scr[HALO + dlt:HALO + dlt + tm, :])
            cnt = jnp.minimum(t + radius + 1, seg_len) - jnp.maximum(t - radius, 0)
            mean_minus = acc / cnt.astype(F32) - u
            y = _dot(mean_minus.astype(BF16), w_ref[...]) * sc_ref[...]
            o_ref[...] = (y * _silu(z_ref[...].astype(F32))).astype(o_ref.dtype)


def _pool_mixer(h, layer, w_in_all, w_grp, scale, w_out_all, n_lat):
    m, d = h.shape
    ng, gw, _ = w_grp.shape
    assert tuple(w // 2 for w in POOL_WINDOWS) == tuple(2 ** g for g in range(ng))
    uz = _matmul(h, w_in_all, w_cols=(layer, 0, 2 * d))
    tm = SEG_TILE
    hb, last = tm // HALO, m // HALO - 1
    specs = [pl.BlockSpec((HALO, gw), lambda g, i: (jnp.maximum(i * hb - 1, 0), g)),
             pl.BlockSpec((tm, gw), lambda g, i: (i, g)),
             pl.BlockSpec((HALO, gw), lambda g, i: (jnp.minimum((i + 1) * hb, last), g)),
             pl.BlockSpec((None, gw, gw), lambda g, i: (g, 0, 0)),
             pl.BlockSpec((1, gw), lambda g, i: (0, g)),
             pl.BlockSpec((tm, gw), lambda g, i: (i, ng + g))]
    yg = pl.pallas_call(
        functools.partial(_pool_kernel, n_lat_blocks=n_lat // tm, n_lat=n_lat, n_ctx=m - n_lat),
        grid=(ng, m // tm),
        in_specs=specs,
        out_specs=pl.BlockSpec((tm, gw), lambda g, i: (i, g)),
        out_shape=jax.ShapeDtypeStruct((m, d), BF16),
        scratch_shapes=[pltpu.VMEM((tm + 2 * HALO, gw), F32)],
        compiler_params=_cparams(("arbitrary", "arbitrary")),
        name="pool_group",
    )(uz, uz, uz, w_grp.astype(BF16), scale.reshape(1, d).astype(F32), uz)
    return _matmul(yg, w_out_all, w_cols=(layer, 0, d))


def kernel(x, c, ctx, c_ctx, ada_w, ada_b, norm_pre, norm_post, mla_w_in, mla_g_q, mla_w_q_up,
           mla_g_kv, mla_w_kv_up, mla_w_out, gdn_w_in, gdn_conv_w, gdn_a_log, gdn_dt_bias,
           gdn_g_norm, gdn_w_out, pool_w_in, pool_w_grp, pool_scale, pool_w_out):
    assert x.shape[0] == 1 and ctx.shape[0] == 1
    n_lat, d = x.shape[1], x.shape[2]
    n_ctx = ctx.shape[1]
    depth = ada_w.shape[0]
    n_mixers = 3
    assert n_lat % SEG_TILE == 0 and n_ctx % SEG_TILE == 0

    cond8 = jnp.concatenate([c.reshape(1, d), c_ctx.reshape(1, d), jnp.zeros((6, d), F32)], axis=0)
    mods = _adaln(cond8, ada_w, ada_b)
    tables = _rope_tables(n_lat, n_ctx)
    xs, h = _prenorm(x[0], ctx[0], norm_pre[0], mods[0])

    counts = [0, 0, 0]
    for i in range(depth):
        kind = i % n_mixers
        j = counts[kind]
        counts[kind] += 1
        need_ctx = i < depth - 1
        if kind == 0:
            y = _mla_mixer(h, tables, j, mla_w_in, mla_g_q[j], mla_w_q_up[j], mla_g_kv[j],
                           mla_w_kv_up[j], mla_w_out, n_lat, need_ctx)
        elif kind == 1:
            y = _gdn_mixer(h, j, gdn_w_in, gdn_conv_w[j], gdn_a_log[j], gdn_dt_bias[j],
                           gdn_g_norm[j], gdn_w_out, n_lat)
        else:
            y = _pool_mixer(h, j, pool_w_in, pool_w_grp[j], pool_scale[j], pool_w_out, n_lat)
        if need_ctx:
            xs, h = _residual(xs, y, norm_post[i], mods[i], n_lat, xs.shape[0],
                              norm_pre[i + 1], mods[i + 1])
        else:
            xs = _residual(xs, y, norm_post[i], mods[i], n_lat, n_lat)
    return xs[None]
```
